```python
import jax, jax.numpy as jnp
from jax import lax
import numpy as np

D_MODEL = 2048
BATCH = 4
SEQ = 4096
DEPTH = 2

GRID_W = 64
CTX_LEN = 256
Q_BLOCK = 128
ROPE_THETA = 10000.0
LN_EPS = 1e-5
RMS_EPS = 1e-6
NEG_INF = -1e30

A_HEADS = 4
A_NOPE = 128
A_ROPE = 64
A_V = 128
A_Q_LORA = 384
A_KV_LORA = 128
B_HEADS = 8
B_KV_HEADS = 2
B_DIM = 64
B_WINDOW = 128
C_HEADS = 4
C_DIM = 128
NA_ROWS = 8
NA_COLS = 16
D_HEADS = 4
D_KV_HEADS = 2
D_DIM = 128
N_BRANCH = 4
BRANCH_W = 512
PEER_HEADS = 8
PEER_KEYS = 128
PEER_EXPERTS = PEER_KEYS * PEER_KEYS
PEER_QDIM = 256
PEER_TOPK = 16
PEER_TOK_BLOCK = 128

A_SCALE = (A_NOPE + A_ROPE) ** -0.5
B_SCALE = B_DIM ** -0.5
C_SCALE = C_DIM ** -0.5
D_SCALE = D_DIM ** -0.5
ALPHA = (2 * DEPTH) ** 0.25
BETA = (8 * DEPTH) ** -0.25

IN_SIZES = (A_Q_LORA, A_KV_LORA, A_ROPE,
            B_HEADS * B_DIM, B_KV_HEADS * B_DIM, B_KV_HEADS * B_DIM,
            C_HEADS * C_DIM, C_HEADS * C_DIM, C_HEADS * C_DIM,
            D_HEADS * D_DIM, D_KV_HEADS * D_DIM, D_KV_HEADS * D_DIM)
IN_WIDTH = sum(IN_SIZES)

kernel_name = "hybrid_parallel_mla_swa_natten_gqa_peer_block"


def layer_norm(x, g, b):
    xf = x.astype(jnp.float32)
    mu = jnp.mean(xf, axis=-1, keepdims=True)
    var = jnp.mean(jnp.square(xf - mu), axis=-1, keepdims=True)
    return ((xf - mu) * lax.rsqrt(var + LN_EPS)).astype(x.dtype) * g + b


def rms_norm(x, g):
    xf = x.astype(jnp.float32)
    ms = jnp.mean(jnp.square(xf), axis=-1, keepdims=True)
    return (xf * lax.rsqrt(ms + RMS_EPS)).astype(x.dtype) * g


def modulate(x, shift, scale):
    return x * (1 + scale) + shift


def heads(t, h):
    return t.reshape(t.shape[:-1] + (h, t.shape[-1] // h))


def flat_heads(t):
    return t.reshape(t.shape[:-2] + (t.shape[-2] * t.shape[-1],))


def split_cols(u):
    out, o = [], 0
    for n in IN_SIZES:
        out.append(u[..., o:o + n])
        o += n
    return out


def axial_angles(n, rot_dim):
    t = jnp.arange(n)
    rows = (t // GRID_W).astype(jnp.float32)
    cols = (t % GRID_W).astype(jnp.float32)
    nf = rot_dim // 4
    inv = ROPE_THETA ** (-jnp.arange(nf, dtype=jnp.float32) / nf)
    ang = jnp.concatenate([rows[:, None] * inv, cols[:, None] * inv], axis=-1)
    return jnp.cos(ang), jnp.sin(ang)


def apply_rope(x, cos, sin):
    half = x.shape[-1] // 2
    x1, x2 = x[..., :half], x[..., half:]
    c = cos[:, None, :].astype(x.dtype)
    sn = sin[:, None, :].astype(x.dtype)
    return jnp.concatenate([x1 * c - x2 * sn, x1 * sn + x2 * c], axis=-1)


def softmax_with_sink(s, sink):
    col = jnp.broadcast_to(sink.astype(jnp.float32), s.shape[:-1] + (1,))
    p = jax.nn.softmax(jnp.concatenate([s, col], axis=-1), axis=-1)
    return p[..., :-1]


def dense_attention(q, k, v, scale, sink=None):
    b, nq, hq, dk = q.shape
    hkv = k.shape[2]
    g = hq // hkv
    qg = q.reshape(b, nq, hkv, g, dk)
    s = jnp.einsum("bqhgd,bkhd->bhgqk", qg, k).astype(jnp.float32) * scale
    if sink is None:
        p = jax.nn.softmax(s, axis=-1)
    else:
        p = softmax_with_sink(s, sink.reshape(hkv, g)[None, :, :, None, None])
    o = jnp.einsum("bhgqk,bkhd->bqhgd", p.astype(v.dtype), v)
    return o.reshape(b, nq, hq, v.shape[-1])


def blocked_attention(q, k, v, scale):
    b, n, hq, dk = q.shape
    nb = n // Q_BLOCK
    qb = jnp.moveaxis(q.reshape(b, nb, Q_BLOCK, hq, dk), 1, 0)
    ob = lax.map(lambda qi: dense_attention(qi, k, v, scale), qb)
    return jnp.moveaxis(ob, 0, 1).reshape(b, n, hq, v.shape[-1])


def mla_qkv(cq, ckv, kr, q_norm, kv_norm, w_uq, w_ukv, rope):
    q = heads(rms_norm(cq, q_norm) @ w_uq, A_HEADS)
    kv = heads(rms_norm(ckv, kv_norm) @ w_ukv, A_HEADS)
    q_nope, q_rope = q[..., :A_NOPE], q[..., A_NOPE:]
    k_nope, v = kv[..., :A_NOPE], kv[..., A_NOPE:]
    k_rope = kr[..., None, :]
    if rope is not None:
        q_rope = apply_rope(q_rope, *rope)
        k_rope = apply_rope(k_rope, *rope)
    k_rope = jnp.broadcast_to(k_rope, k_nope.shape[:-1] + (A_ROPE,))
    return (jnp.concatenate([q_nope, q_rope], axis=-1), jnp.concatenate([k_nope, k_rope], axis=-1), v)


def band_sink_attention(q, k, v, k_ctx, v_ctx, sink):
    b, n, hq, d = q.shape
    hkv = k.shape[2]
    g = hq // hkv
    nb = n // Q_BLOCK
    nk = 3 * Q_BLOCK
    pad = ((0, 0), (Q_BLOCK, Q_BLOCK), (0, 0), (0, 0))

    def bands(t):
        tp = jnp.pad(t, pad)
        return jnp.concatenate(
            [tp[:, j * Q_BLOCK: j * Q_BLOCK + n].reshape(b, nb, Q_BLOCK, hkv, t.shape[-1]) for j in range(3)],
            axis=2)

    kb, vb = bands(k), bands(v)
    qb = q.reshape(b, nb, Q_BLOCK, hkv, g, d)
    s_win = jnp.einsum("bnqhgd,bnkhd->bnhgqk", qb, kb).astype(jnp.float32) * B_SCALE
    rel = jnp.arange(nk) - Q_BLOCK
    kabs = jnp.arange(nb)[:, None] * Q_BLOCK + rel[None, :]
    valid = (kabs >= 0) & (kabs < n)
    band = jnp.abs(rel[None, :] - jnp.arange(Q_BLOCK)[:, None]) <= B_WINDOW
    mask = band[None] & valid[:, None, :]
    s_win = jnp.where(mask[None, :, None, None], s_win, NEG_INF)
    s_ctx = jnp.einsum("bnqhgd,bchd->bnhgqc", qb, k_ctx).astype(jnp.float32) * B_SCALE
    p = softmax_with_sink(jnp.concatenate([s_win, s_ctx], axis=-1),
                          sink.reshape(hkv, g)[None, None, :, :, None, None]).astype(v.dtype)
    o = (jnp.einsum("bnhgqk,bnkhd->bnqhgd", p[..., :nk], vb)
         + jnp.einsum("bnhgqc,bchd->bnqhgd", p[..., nk:], v_ctx))
    return o.reshape(b, n, hq, v.shape[-1])


def neighbourhood_attention(q, k, v, k_ctx, v_ctx, rpb):
    b, n, h, d = q.shape
    rows = n // GRID_W
    kh = min(NA_ROWS, rows)
    kw = NA_COLS
    r = jnp.arange(rows)
    row_start = jnp.clip(r - kh // 2, 0, rows - kh)
    row_idx = row_start[:, None] + jnp.arange(kh)
    ci = jnp.arange(GRID_W)
    col_start = jnp.clip(ci - kw // 2, 0, GRID_W - kw)
    col_mask = (ci[None, :] >= col_start[:, None]) & (ci[None, :] < col_start[:, None] + kw)
    qg = q.reshape(b, rows, GRID_W, h, d)
    kg = k.reshape(b, rows, GRID_W, h, d)[:, row_idx].reshape(b, rows, kh * GRID_W, h, d)
    vg = v.reshape(b, rows, GRID_W, h, v.shape[-1])[:, row_idx].reshape(b, rows, kh * GRID_W, h, v.shape[-1])
    s_nb = jnp.einsum("brqhd,brkhd->brhqk", qg, kg).astype(jnp.float32) * C_SCALE
    row_off = row_idx - r[:, None] + (NA_ROWS - 1)
    col_off = jnp.clip(ci[None, :] - ci[:, None] + (NA_COLS - 1), 0, 2 * NA_COLS - 2)
    bias = rpb[:, row_off[:, None, :, None], col_off[None, :, None, :]]
    bias = jnp.transpose(bias, (1, 0, 2, 3, 4)).reshape(rows, h, GRID_W, kh * GRID_W)
    mask = jnp.tile(col_mask, (1, kh))
    s_nb = jnp.where(mask, s_nb + bias.astype(jnp.float32)[None], NEG_INF)
    s_ctx = jnp.einsum("brqhd,bchd->brhqc", qg, k_ctx).astype(jnp.float32) * C_SCALE
    p = jax.nn.softmax(jnp.concatenate([s_nb, s_ctx], axis=-1), axis=-1).astype(v.dtype)
    nk = kh * GRID_W
    o = (jnp.einsum("brhqk,brkhd->brqhd", p[..., :nk], vg)
         + jnp.einsum("brhqc,bchd->brqhd", p[..., nk:], v_ctx))
    return o.reshape(b, n, h, v.shape[-1])


def merge_branches(h, branches, w_branch, w_gate, b_gate, w_out):
    ys = [jax.nn.sigmoid(h @ w_gate[i] + b_gate[i]) * (o @ w_branch[i]) for i, o in enumerate(branches)]
    return (ys[0] + ys[1] + ys[2] + ys[3]) @ w_out


def token_mixers(h, hc, w_in, a_q_norm, a_kv_norm, a_w_uq, a_w_ukv, b_sink, c_rpb,
                 d_q_norm, d_k_norm, w_branch, w_gate, b_gate, w_out, rope_a, rope_b, rope_d, need_ctx):
    a_cq, a_ckv, a_kr, b_q, b_k, b_v, c_q, c_k, c_v, d_q, d_k, d_v = split_cols(h @ w_in)
    ac_cq, ac_ckv, ac_kr, bc_q, bc_k, bc_v, cc_q, cc_k, cc_v, dc_q, dc_k, dc_v = split_cols(hc @ w_in)
    cat = lambda u, w: jnp.concatenate([u, w], axis=1)

    qa, ka, va = mla_qkv(a_cq, a_ckv, a_kr, a_q_norm, a_kv_norm, a_w_uq, a_w_ukv, rope_a)
    qa_c, ka_c, va_c = mla_qkv(ac_cq, ac_ckv, ac_kr, a_q_norm, a_kv_norm, a_w_uq, a_w_ukv, None)
    oa = blocked_attention(qa, cat(ka, ka_c), cat(va, va_c), A_SCALE)

    qb = apply_rope(heads(b_q, B_HEADS), *rope_b)
    kb = apply_rope(heads(b_k, B_KV_HEADS), *rope_b)
    vb = heads(b_v, B_KV_HEADS)
    kb_c, vb_c = heads(bc_k, B_KV_HEADS), heads(bc_v, B_KV_HEADS)
    ob = band_sink_attention(qb, kb, vb, kb_c, vb_c, b_sink)

    qc_, kc_, vc_ = heads(c_q, C_HEADS), heads(c_k, C_HEADS), heads(c_v, C_HEADS)
    kc_c, vc_c = heads(cc_k, C_HEADS), heads(cc_v, C_HEADS)
    oc = neighbourhood_attention(qc_, kc_, vc_, kc_c, vc_c, c_rpb)

    qd = apply_rope(rms_norm(heads(d_q, D_HEADS), d_q_norm), *rope_d)
    kd = apply_rope(rms_norm(heads(d_k, D_KV_HEADS), d_k_norm), *rope_d)
    vd = heads(d_v, D_KV_HEADS)
    kd_c = rms_norm(heads(dc_k, D_KV_HEADS), d_k_norm)
    vd_c = heads(dc_v, D_KV_HEADS)
    od = blocked_attention(qd, cat(kd, kd_c), cat(vd, vd_c), D_SCALE)

    out = merge_branches(h, [flat_heads(oa), flat_heads(ob), flat_heads(oc), flat_heads(od)],
                         w_branch, w_gate, b_gate, w_out)
    if not need_ctx:
        return out, None
    oa_c = dense_attention(qa_c, ka_c, va_c, A_SCALE)
    ob_c = dense_attention(heads(bc_q, B_HEADS), kb_c, vb_c, B_SCALE, sink=b_sink)
    oc_c = dense_attention(heads(cc_q, C_HEADS), kc_c, vc_c, C_SCALE)
    od_c = dense_attention(rms_norm(heads(dc_q, D_HEADS), d_q_norm), kd_c, vd_c, D_SCALE)
    out_c = merge_branches(hc, [flat_heads(oa_c), flat_heads(ob_c), flat_heads(oc_c), flat_heads(od_c)],
                           w_branch, w_gate, b_gate, w_out)
    return out, out_c


def peer_ffn(h, w_q, sub_keys, u, v):
    b, n, d = h.shape
    t = b * n
    hf = h.reshape(t, d)
    q = (hf @ w_q).reshape(t, PEER_HEADS, 2, PEER_QDIM // 2)
    s = jnp.einsum("thpd,hpkd->thpk", q, sub_keys).astype(jnp.float32)
    s_top, i_top = lax.top_k(s, PEER_TOPK)
    cand = s_top[:, :, 0, :, None] + s_top[:, :, 1, None, :]
    cand_idx = i_top[:, :, 0, :, None] * PEER_KEYS + i_top[:, :, 1, None, :]
    best, pos = lax.top_k(cand.reshape(t, PEER_HEADS, PEER_TOPK * PEER_TOPK), PEER_TOPK)
    experts = jnp.take_along_axis(cand_idx.reshape(t, PEER_HEADS, PEER_TOPK * PEER_TOPK), pos, axis=-1)
    gates = jax.nn.softmax(best, axis=-1)
    nb = t // PEER_TOK_BLOCK

    def block(args):
        xb, eb, gb = args
        act = jax.nn.gelu(jnp.einsum("td,thkd->thk", xb, u[eb]), approximate=False)
        return jnp.einsum("thk,thkd->td", gb.astype(xb.dtype) * act, v[eb])

    out = lax.map(block, (hf.reshape(nb, PEER_TOK_BLOCK, d),
                          experts.reshape(nb, PEER_TOK_BLOCK, PEER_HEADS, PEER_TOPK),
                          gates.reshape(nb, PEER_TOK_BLOCK, PEER_HEADS, PEER_TOPK)))
    return out.reshape(b, n, d)


def setup_inputs(seed: int = 0) -> dict:
    key = jax.random.key(seed)
    ks = jax.random.split(key, 32)
    D, L = D_MODEL, DEPTH

    def nrm(k, shape, std):
        return std * jax.random.normal(k, shape, jnp.float32)

    def gain(k, shape):
        return 1.0 + nrm(k, shape, 0.02)

    return {
        "x": nrm(ks[0], (BATCH, SEQ, D), 1.0),
        "c": nrm(ks[1], (BATCH, D), 1.0),
        "ctx": nrm(ks[2], (BATCH, CTX_LEN, D), 1.0),
        "c_ctx": nrm(ks[3], (D,), 1.0),
        "w_ada": nrm(ks[4], (L, D, 6 * D), 0.5 * D ** -0.5),
        "b_ada": nrm(ks[5], (L, 6 * D), 0.01),
        "w_in": nrm(ks[6], (L, D, IN_WIDTH), D ** -0.5),
        "a_q_norm": gain(ks[7], (L, A_Q_LORA)),
        "a_kv_norm": gain(ks[8], (L, A_KV_LORA)),
        "a_w_uq": nrm(ks[9], (L, A_Q_LORA, A_HEADS * (A_NOPE + A_ROPE)), A_Q_LORA ** -0.5),
        "a_w_ukv": nrm(ks[10], (L, A_KV_LORA, A_HEADS * (A_NOPE + A_V)), A_KV_LORA ** -0.5),
        "b_sink": nrm(ks[11], (L, B_HEADS), 0.5),
        "c_rpb": nrm(ks[12], (L, C_HEADS, 2 * NA_ROWS - 1, 2 * NA_COLS - 1), 0.1),
        "d_q_norm": gain(ks[13], (L, D_DIM)),
        "d_k_norm": gain(ks[14], (L, D_DIM)),
        "w_branch": nrm(ks[15], (L, N_BRANCH, BRANCH_W, D), BETA * BRANCH_W ** -0.5),
        "w_gate": nrm(ks[16], (L, N_BRANCH, D, D), D ** -0.5),
        "b_gate": nrm(ks[17], (L, N_BRANCH, D), 0.01),
        "w_out": nrm(ks[18], (L, D, D), BETA * D ** -0.5),
        "ln1_g": gain(ks[19], (L, D)),
        "ln1_b": nrm(ks[20], (L, D), 0.01),
        "ln2_g": gain(ks[21], (L, D)),
        "ln2_b": nrm(ks[22], (L, D), 0.01),
        "peer_w_q": nrm(ks[23], (L, D, PEER_HEADS * PEER_QDIM), D ** -0.5),
        "peer_sub_keys": nrm(ks[24], (L, PEER_HEADS, 2, PEER_KEYS, PEER_QDIM // 2), (PEER_QDIM // 2) ** -0.5),
        "peer_u": nrm(ks[25], (L, PEER_EXPERTS, D), D ** -0.5),
        "peer_v": nrm(ks[26], (L, PEER_EXPERTS, D), BETA * PEER_HEADS ** -0.5),
    }


def reference(x, c, ctx, c_ctx, w_ada, b_ada, w_in, a_q_norm, a_kv_norm, a_w_uq, a_w_ukv, b_sink, c_rpb,
              d_q_norm, d_k_norm, w_branch, w_gate, b_gate, w_out, ln1_g, ln1_b, ln2_g, ln2_b,
              peer_w_q, peer_sub_keys, peer_u, peer_v):
    n = x.shape[1]
    rope_a = axial_angles(n, A_ROPE)
    rope_b = axial_angles(n, B_DIM)
    rope_d = axial_angles(n, D_DIM)
    xc = ctx
    for l in range(DEPTH):
        need_ctx = l < DEPTH - 1
        mod = jax.nn.silu(c) @ w_ada[l] + b_ada[l]
        mod_c = jax.nn.silu(c_ctx) @ w_ada[l] + b_ada[l]
        sh_a, sc_a, g_a, sh_f, sc_f, g_f = [m[:, None, :] for m in jnp.split(mod, 6, axis=-1)]
        shc_a, scc_a, gc_a, shc_f, scc_f, gc_f = jnp.split(mod_c, 6, axis=-1)
        h = modulate(x, sh_a, sc_a)
        hc = modulate(xc, shc_a, scc_a)
        mix, mix_c = token_mixers(h, hc, w_in[l], a_q_norm[l], a_kv_norm[l], a_w_uq[l], a_w_ukv[l], b_sink[l],
                                  c_rpb[l], d_q_norm[l], d_k_norm[l], w_branch[l], w_gate[l], b_gate[l], w_out[l],
                                  rope_a, rope_b, rope_d, need_ctx)
        x = layer_norm(ALPHA * x + g_a * mix, ln1_g[l], ln1_b[l])
        ffn = peer_ffn(modulate(x, sh_f, sc_f), peer_w_q[l], peer_sub_keys[l], peer_u[l], peer_v[l])
        x = layer_norm(ALPHA * x + g_f * ffn, ln2_g[l], ln2_b[l])
        if need_ctx:
            xc = layer_norm(ALPHA * xc + gc_a * mix_c, ln1_g[l], ln1_b[l])
            ffn_c = peer_ffn(modulate(xc, shc_f, scc_f), peer_w_q[l], peer_sub_keys[l], peer_u[l], peer_v[l])
            xc = layer_norm(ALPHA * xc + gc_f * ffn_c, ln2_g[l], ln2_b[l])
    return x
```

```python
import functools
import math

import jax
import jax.numpy as jnp
import numpy as np
from jax import lax
from jax.experimental import pallas as pl
from jax.experimental.pallas import tpu as pltpu

F32 = jnp.float32
BF16 = jnp.bfloat16

GRID_W = 64
ROPE_THETA = 10000.0
LN_EPS = 1e-5
RMS_EPS = 1e-6
NEG_INF = -1e30

A_HEADS, A_NOPE, A_ROPE, A_V, A_Q_LORA, A_KV_LORA = 4, 128, 64, 128, 384, 128
B_HEADS, B_KV_HEADS, B_DIM, B_WINDOW = 8, 2, 64, 128
C_HEADS, C_DIM, NA_ROWS, NA_COLS = 4, 128, 8, 16
D_HEADS, D_KV_HEADS, D_DIM = 4, 2, 128
N_BRANCH, BRANCH_W = 4, 512
PEER_HEADS, PEER_KEYS, PEER_QDIM, PEER_TOPK = 8, 128, 256, 16

A_SCALE = (A_NOPE + A_ROPE) ** -0.5
B_SCALE = B_DIM ** -0.5
C_SCALE = C_DIM ** -0.5
D_SCALE = D_DIM ** -0.5

IN_SIZES = (A_Q_LORA, A_KV_LORA, A_ROPE,
            B_HEADS * B_DIM, B_KV_HEADS * B_DIM, B_KV_HEADS * B_DIM,
            C_HEADS * C_DIM, C_HEADS * C_DIM, C_HEADS * C_DIM,
            D_HEADS * D_DIM, D_KV_HEADS * D_DIM, D_KV_HEADS * D_DIM)
LANE = 128
IN_PAD_WIDTH = 4096

VMEM_LIMIT = 56 * 1024 * 1024

B_TQ, B_BACK, B_WK = 512, 128, 768
C_TQ, C_BACK, C_WK = 4 * GRID_W, 4 * GRID_W, 12 * GRID_W


def _cparams(sem):
    return pltpu.CompilerParams(dimension_semantics=sem, vmem_limit_bytes=VMEM_LIMIT)


def _pick(n, cands):
    for c in cands:
        if n % c == 0:
            return c
    raise ValueError(f"no tile in {cands} divides {n}")


def _dot(a, b):
    return jnp.dot(a, b, preferred_element_type=F32)


def _dot_nt(a, b):
    return lax.dot_general(a, b, (((1,), (1,)), ((), ())), preferred_element_type=F32)


def _ada_kernel(c_ref, w_ref, b_ref, o_ref):
    c = c_ref[...]
    s = c * (1.0 / (1.0 + jnp.exp(-c)))
    w = w_ref[...]
    s_hi = s.astype(BF16)
    s_lo = (s - s_hi.astype(F32)).astype(BF16)
    w_hi = w.astype(BF16)
    w_lo = (w - w_hi.astype(F32)).astype(BF16)
    acc = _dot(s_hi, w_hi) + _dot(s_lo, w_hi) + _dot(s_hi, w_lo)
    o_ref[...] = acc + b_ref[...]


def ada_modulation(cc, w, b):
    r, k = cc.shape
    n = w.shape[1]
    tn = _pick(n, (512, 256, 128))
    return pl.pallas_call(
        _ada_kernel,
        grid=(n // tn,),
        in_specs=[pl.BlockSpec((r, k), lambda j: (0, 0)),
                  pl.BlockSpec((k, tn), lambda j: (0, j)),
                  pl.BlockSpec((1, tn), lambda j: (0, j))],
        out_specs=pl.BlockSpec((r, tn), lambda j: (0, j)),
        out_shape=jax.ShapeDtypeStruct((r, n), F32),
        compiler_params=_cparams(("arbitrary",)),
        name="ada_modulation",
    )(cc, w, b.reshape(1, n))


def _mm_kernel(*refs, prologue, has_bias, act):
    it = iter(refs)
    x_ref = next(it)
    if prologue == "modulate":
        sh_ref, sc_ref = next(it), next(it)
    elif prologue == "rmsnorm":
        g_ref = next(it)
    w_ref = next(it)
    b_ref = next(it) if has_bias else None
    o_ref = next(it)
    a_scr = next(it)

    @pl.when(pl.program_id(1) == 0)
    def _():
        x = x_ref[...].astype(F32)
        if prologue == "modulate":
            a = x * (1.0 + sc_ref[0]) + sh_ref[0]
        elif prologue == "rmsnorm":
            ms = jnp.mean(x * x, axis=-1, keepdims=True)
            a = (x * lax.rsqrt(ms + RMS_EPS)) * g_ref[...]
        else:
            a = x
        a_scr[...] = a.astype(BF16)

    acc = _dot(a_scr[...], w_ref[...])
    if has_bias:
        acc = acc + b_ref[...]
    if act == "sigmoid":
        acc = 1.0 / (1.0 + jnp.exp(-acc))
    o_ref[...] = acc.astype(o_ref.dtype)


def token_matmul(x, w, *, rows=None, x_cols=None, prologue=None, mod=None, mod_idx=None, group_of=None,
                 norm_g=None, bias=None, act=None, out_dtype=F32, name="token_matmul"):
    rows = x.shape[0] if rows is None else rows
    kdim, n = w.shape
    kw, kblk = (x.shape[1], 0) if x_cols is None else x_cols
    assert kw == kdim
    tm = _pick(rows, (512, 256))
    tn = _pick(n, (1024, 512, 256, 128))
    in_specs = [pl.BlockSpec((tm, kdim), lambda i, j: (i, kblk))]
    args = [x]
    if prologue == "modulate":
        tm_group = group_of(tm)
        for slot in mod_idx:
            in_specs.append(pl.BlockSpec((1, 1, kdim), lambda i, j, s=slot: (tm_group(i) * 6 + s, 0, 0)))
            args.append(mod)
    elif prologue == "rmsnorm":
        in_specs.append(pl.BlockSpec((1, kdim), lambda i, j: (0, 0)))
        args.append(norm_g.reshape(1, kdim))
    in_specs.append(pl.BlockSpec((kdim, tn), lambda i, j: (0, j)))
    args.append(w)
    if bias is not None:
        in_specs.append(pl.BlockSpec((1, tn), lambda i, j: (0, j)))
        args.append(bias.reshape(1, n))
    return pl.pallas_call(
        functools.partial(_mm_kernel, prologue=prologue, has_bias=bias is not None, act=act),
        grid=(rows // tm, n // tn),
        in_specs=in_specs,
        out_specs=pl.BlockSpec((tm, tn), lambda i, j: (i, j)),
        out_shape=jax.ShapeDtypeStruct((rows, n), out_dtype),
        scratch_shapes=[pltpu.VMEM((tm, kdim), BF16)],
        compiler_params=_cparams(("parallel", "arbitrary")),
        name=name,
    )(*args)


def _res_ln_kernel(x_ref, y_ref, gate_ref, g_ref, b_ref, o_ref, *, alpha):
    z = alpha * x_ref[...] + gate_ref[0] * y_ref[...].astype(F32)
    mu = jnp.mean(z, axis=-1, keepdims=True)
    zc = z - mu
    var = jnp.mean(zc * zc, axis=-1, keepdims=True)
    o_ref[...] = (zc * lax.rsqrt(var + LN_EPS)) * g_ref[...] + b_ref[...]


def residual_ln(x, y, mod, gate_slot, group_of, ln_g, ln_b, alpha):
    rows, d = y.shape
    tm = _pick(rows, (256,))
    tm_group = group_of(tm)
    return pl.pallas_call(
        functools.partial(_res_ln_kernel, alpha=alpha),
        grid=(rows // tm,),
        in_specs=[pl.BlockSpec((tm, d), lambda i: (i, 0)),
                  pl.BlockSpec((tm, d), lambda i: (i, 0)),
                  pl.BlockSpec((1, 1, d), lambda i: (tm_group(i) * 6 + gate_slot, 0, 0)),
                  pl.BlockSpec((1, d), lambda i: (0, 0)),
                  pl.BlockSpec((1, d), lambda i: (0, 0))],
        out_specs=pl.BlockSpec((tm, d), lambda i: (i, 0)),
        out_shape=jax.ShapeDtypeStruct((rows, d), F32),
        compiler_params=_cparams(("parallel",)),
        name="residual_ln",
    )(x, y, mod, ln_g.reshape(1, d), ln_b.reshape(1, d))


def _dense_attn_kernel(*refs, n_parts, scale, has_sink):
    q_refs = refs[:n_parts]
    k_refs = refs[n_parts:2 * n_parts]
    v_ref = refs[2 * n_parts]
    sink_ref = refs[2 * n_parts + 1] if has_sink else None
    o_ref = refs[-1]
    s = _dot_nt(q_refs[0][0, 0], k_refs[0][0, 0])
    for qr, kr in zip(q_refs[1:], k_refs[1:]):
        s = s + _dot_nt(qr[0, 0], kr[0, 0])
    s = s * scale
    m = jnp.max(s, axis=-1, keepdims=True)
    if has_sink:
        sink = sink_ref[pl.program_id(1)]
        m = jnp.maximum(m, sink)
    p = jnp.exp(s - m)
    l = jnp.sum(p, axis=-1, keepdims=True)
    if has_sink:
        l = l + jnp.exp(sink - m)
    o = _dot(p.astype(BF16), v_ref[0, 0])
    o_ref[0, 0] = (o / l).astype(o_ref.dtype)


def dense_attention(q_parts, k_parts, v, scale, sink=None, name="dense_attention"):
    b, hq, nq, _ = q_parts[0].shape
    hkv, nk, dv = v.shape[1], v.shape[2], v.shape[3]
    g = hq // hkv
    tq = _pick(nq, (256, 128))
    in_specs, args = [], []
    for q in q_parts:
        in_specs.append(pl.BlockSpec((1, 1, tq, q.shape[3]), lambda bi, h, i: (bi, h, i, 0)))
        args.append(q)
    for k in k_parts:
        if k.shape[1] == 1:
            in_specs.append(pl.BlockSpec((1, 1, nk, k.shape[3]), lambda bi, h, i: (bi, 0, 0, 0)))
        else:
            in_specs.append(pl.BlockSpec((1, 1, nk, k.shape[3]), lambda bi, h, i: (bi, h // g, 0, 0)))
        args.append(k)
    in_specs.append(pl.BlockSpec((1, 1, nk, dv), lambda bi, h, i: (bi, h // g, 0, 0)))
    args.append(v)
    if sink is not None:
        in_specs.append(pl.BlockSpec(memory_space=pltpu.SMEM))
        args.append(sink.astype(F32))
    return pl.pallas_call(
        functools.partial(_dense_attn_kernel, n_parts=len(q_parts), scale=scale, has_sink=sink is not None),
        grid=(b, hq, nq // tq),
        in_specs=in_specs,
        out_specs=pl.BlockSpec((1, 1, tq, dv), lambda bi, h, i: (bi, h, i, 0)),
        out_shape=jax.ShapeDtypeStruct((b, hq, nq, dv), BF16),
        compiler_params=_cparams(("parallel", "parallel", "arbitrary")),
        name=name,
    )(*args)


def _window_attn_kernel(*refs, scale, has_sink, tq, back, wk, n_lat):
    q_ref, k_ref, v_ref, kc_ref, vc_ref, bias_ref = refs[:6]
    sink_ref = refs[6] if has_sink else None
    o_ref = refs[-1]
    i = pl.program_id(2)
    start = pl.multiple_of(jnp.clip(i * tq - back, 0, n_lat - wk), LANE)
    q = q_ref[0, 0]
    kw = k_ref[0, 0, pl.ds(start, wk), :]
    vw = v_ref[0, 0, pl.ds(start, wk), :]
    s1 = _dot_nt(q, kw) * scale + bias_ref[0, 0]
    s2 = _dot_nt(q, kc_ref[0, 0]) * scale
    m = jnp.maximum(jnp.max(s1, axis=-1, keepdims=True), jnp.max(s2, axis=-1, keepdims=True))
    if has_sink:
        sink = sink_ref[pl.program_id(1)]
        m = jnp.maximum(m, sink)
    p1 = jnp.exp(s1 - m)
    p2 = jnp.exp(s2 - m)
    l = jnp.sum(p1, axis=-1, keepdims=True) + jnp.sum(p2, axis=-1, keepdims=True)
    if has_sink:
        l = l + jnp.exp(sink - m)
    o = _dot(p1.astype(BF16), vw) + _dot(p2.astype(BF16), vc_ref[0, 0])
    o_ref[0, 0] = (o / l).astype(o_ref.dtype)


def window_attention(q, k, v, kc, vc, bias, scale, geom, sink=None, name="window_attention"):
    tq, back, wk = geom
    b, hq, n, d = q.shape
    hkv, nc = k.shape[1], kc.shape[2]
    g = hq // hkv
    nqb = n // tq
    assert n % tq == 0 and nqb >= 2 and n >= wk
    per_head_bias = bias.shape[1] != 1

    def bias_map(bi, h, i):
        cls = jnp.where(i == 0, 0, jnp.where(i == nqb - 1, 2, 1))
        return (cls, h if per_head_bias else 0, 0, 0)

    in_specs = [pl.BlockSpec((1, 1, tq, d), lambda bi, h, i: (bi, h, i, 0)),
                pl.BlockSpec((1, 1, n, d), lambda bi, h, i: (bi, h // g, 0, 0)),
                pl.BlockSpec((1, 1, n, d), lambda bi, h, i: (bi, h // g, 0, 0)),
                pl.BlockSpec((1, 1, nc, d), lambda bi, h, i: (bi, h // g, 0, 0)),
                pl.BlockSpec((1, 1, nc, d), lambda bi, h, i: (bi, h // g, 0, 0)),
                pl.BlockSpec((1, 1, tq, wk), bias_map)]
    args = [q, k, v, kc, vc, bias]
    if sink is not None:
        in_specs.append(pl.BlockSpec(memory_space=pltpu.SMEM))
        args.append(sink.astype(F32))
    return pl.pallas_call(
        functools.partial(_window_attn_kernel, scale=scale, has_sink=sink is not None, tq=tq, back=back, wk=wk,
                          n_lat=n),
        grid=(b, hq, nqb),
        in_specs=in_specs,
        out_specs=pl.BlockSpec((1, 1, tq, d), lambda bi, h, i: (bi, h, i, 0)),
        out_shape=jax.ShapeDtypeStruct((b, hq, n, d), BF16),
        compiler_params=_cparams(("parallel", "parallel", "arbitrary")),
        name=name,
    )(*args)


def _window_q_offsets(geom):
    tq, back, wk = geom
    return (0, back, wk - tq)


def band_bias_table():
    tq, _, wk = (B_TQ, B_BACK, B_WK)
    kpos = np.arange(wk)[None, :]
    tabs = []
    for off in _window_q_offsets((B_TQ, B_BACK, B_WK)):
        qpos = off + np.arange(tq)[:, None]
        tabs.append(np.where(np.abs(qpos - kpos) <= B_WINDOW, 0.0, NEG_INF))
    return jnp.asarray(np.stack(tabs)[:, None], F32)


def neighbourhood_bias_table(rpb, rows):
    tq, back, wk = (C_TQ, C_BACK, C_WK)
    assert rows >= wk // GRID_W and rows >= NA_ROWS
    nqb = rows * GRID_W // tq
    tabs = []
    for cls, off in enumerate(_window_q_offsets((C_TQ, C_BACK, C_WK))):
        blk = (0, 1, nqb - 1)[cls]
        w0 = int(np.clip(blk * tq - back, 0, rows * GRID_W - wk)) // GRID_W
        assert w0 * GRID_W + off == blk * tq
        qi = np.arange(tq)
        kj = np.arange(wk)
        q_row = w0 + (off + qi) // GRID_W
        q_col = (off + qi) % GRID_W
        k_row = w0 + kj // GRID_W
        k_col = kj % GRID_W
        row_start = np.clip(q_row - NA_ROWS // 2, 0, rows - NA_ROWS)
        col_start = np.clip(q_col - NA_COLS // 2, 0, GRID_W - NA_COLS)
        row_ok = (k_row[None, :] >= row_start[:, None]) & (k_row[None, :] < row_start[:, None] + NA_ROWS)
        col_ok = (k_col[None, :] >= col_start[:, None]) & (k_col[None, :] < col_start[:, None] + NA_COLS)
        row_off = np.clip(k_row[None, :] - q_row[:, None] + (NA_ROWS - 1), 0, 2 * NA_ROWS - 2)
        col_off = np.clip(k_col[None, :] - q_col[:, None] + (NA_COLS - 1), 0, 2 * NA_COLS - 2)
        vals = rpb[:, row_off, col_off].astype(F32)
        tabs.append(jnp.where(jnp.asarray(row_ok & col_ok)[None], vals, NEG_INF))
    return jnp.stack(tabs)


def _merge_kernel(o_ref, w_ref, g_ref, y_ref, acc_ref):
    br = pl.program_id(2)
    t = g_ref[...].astype(F32) * _dot(o_ref[0], w_ref[0])

    @pl.when(br == 0)
    def _():
        acc_ref[...] = t

    @pl.when(br > 0)
    def _():
        acc_ref[...] += t

    @pl.when(br == pl.num_programs(2) - 1)
    def _():
        y_ref[...] = acc_ref[...].astype(y_ref.dtype)


def merge_branches(o, w_branch, gates):
    nbr, rows, bw = o.shape
    d = w_branch.shape[2]
    tm = _pick(rows, (512, 256))
    tn = _pick(d, (512, 256, 128))
    nj = d // tn
    return pl.pallas_call(
        _merge_kernel,
        grid=(rows // tm, nj, nbr),
        in_specs=[pl.BlockSpec((1, tm, bw), lambda i, j, r: (r, i, 0)),
                  pl.BlockSpec((1, bw, tn), lambda i, j, r: (r, 0, j)),
                  pl.BlockSpec((tm, tn), lambda i, j, r: (i, r * nj + j))],
        out_specs=pl.BlockSpec((tm, tn), lambda i, j, r: (i, j)),
        out_shape=jax.ShapeDtypeStruct((rows, d), BF16),
        scratch_shapes=[pltpu.VMEM((tm, tn), F32)],
        compiler_params=_cparams(("parallel", "parallel", "arbitrary")),
        name="merge_branches",
    )(o, w_branch, gates)


def _peer_pairs():
    return [(a, b) for a in range(PEER_TOPK) for b in range(PEER_TOPK) if (a + 1) * (b + 1) <= PEER_TOPK]


def _peer_topk_kernel(q_ref, keys_ref, s0_ref, s1_ref, e0_ref, e1_ref, tau_ref):
    tops = [[[None] * PEER_HEADS for _ in range(PEER_TOPK)] for _ in range(2)]
    for h in range(PEER_HEADS):
        for p in range(2):
            hp = 2 * h + p
            qs = q_ref[:, hp * PEER_KEYS:(hp + 1) * PEER_KEYS]
            s = _dot_nt(keys_ref[hp], qs)
            (s0_ref if p == 0 else s1_ref)[h] = s
            work = s
            for r in range(PEER_TOPK):
                m = jnp.max(work, axis=0, keepdims=True)
                tops[p][r][h] = m
                if r + 1 < PEER_TOPK:
                    work = jnp.where(work == m, -jnp.inf, work)
    top = [[jnp.concatenate(tops[p][r], axis=0) for r in range(PEER_TOPK)] for p in range(2)]
    cands = [top[0][a] + top[1][b] for a, b in _peer_pairs()]
    work = list(cands)
    for r in range(PEER_TOPK):
        m = functools.reduce(jnp.maximum, work)
        if r + 1 < PEER_TOPK:
            work = [jnp.where(w == m, -jnp.inf, w) for w in work]
    tau = m
    cmax = cands[0]
    z = functools.reduce(jnp.add, [jnp.where(c >= tau, jnp.exp(c - cmax), 0.0) for c in cands])
    rz = 1.0 / z
    tau_ref[...] = tau
    for h in range(PEER_HEADS):
        e0_ref[h] = jnp.exp(s0_ref[h] - top[0][0][h:h + 1]) * rz[h:h + 1]
        e1_ref[h] = jnp.exp(s1_ref[h] - top[1][0][h:h + 1])


def peer_topk(q, keys):
    rows = q.shape[0]
    tt = _pick(rows, (256,))
    big = pl.BlockSpec((PEER_HEADS, PEER_KEYS, tt), lambda i: (0, 0, i))
    big_shape = jax.ShapeDtypeStruct((PEER_HEADS, PEER_KEYS, rows), F32)
    return pl.pallas_call(
        _peer_topk_kernel,
        grid=(rows // tt,),
        in_specs=[pl.BlockSpec((tt, q.shape[1]), lambda i: (i, 0)),
                  pl.BlockSpec(keys.shape, lambda i: (0, 0, 0))],
        out_specs=[big, big, big, big, pl.BlockSpec((PEER_HEADS, tt), lambda i: (0, i))],
        out_shape=[big_shape, big_shape, big_shape, big_shape, jax.ShapeDtypeStruct((PEER_HEADS, rows), F32)],
        compiler_params=_cparams(("parallel",)),
        name="peer_topk",
    )(q, keys)


def _peer_dense_kernel(x_ref, sh_ref, sc_ref, u_ref, vt_ref, s0_ref, s1_ref, e0_ref, e1_ref, tau_ref, o_ref, h_scr,
                       *, i_per_blk):
    eb = pl.program_id(1)

    @pl.when(eb == 0)
    def _():
        h_scr[...] = (x_ref[...] * (1.0 + sc_ref[0]) + sh_ref[0]).astype(BF16)
        o_ref[...] = jnp.zeros_like(o_ref)

    a = _dot_nt(u_ref[...], h_scr[...])
    act = 0.5 * a * (1.0 + lax.erf(a * (1.0 / math.sqrt(2.0))))
    ws = []
    for ii in range(i_per_blk):
        i = eb * i_per_blk + ii
        w = None
        for h in range(PEER_HEADS):
            s_sum = s0_ref[h, pl.ds(i, 1), :] + s1_ref[h]
            gate = e0_ref[h, pl.ds(i, 1), :] * e1_ref[h]
            t = jnp.where(s_sum >= tau_ref[h:h + 1, :], gate, 0.0)
            w = t if w is None else w + t
        ws.append(w)
    wgt = ws[0] if len(ws) == 1 else jnp.concatenate(ws, axis=0)
    p = (wgt * act).astype(BF16)
    o_ref[...] += _dot(vt_ref[...], p)


def peer_dense(x, mod, mod_idx, group_of, u, vt, s0, s1, e0, e1, tau):
    rows, d = x.shape
    n_exp = u.shape[0]
    tt = _pick(rows, (512, 256))
    te = 512
    i_per_blk = te // PEER_KEYS
    tt_group = group_of(tt)
    sel = pl.BlockSpec((PEER_HEADS, PEER_KEYS, tt), lambda t, e: (0, 0, t))
    mod_specs = [pl.BlockSpec((1, 1, d), lambda t, e, s=slot: (tt_group(t) * 6 + s, 0, 0)) for slot in mod_idx]
    return pl.pallas_call(
        functools.partial(_peer_dense_kernel, i_per_blk=i_per_blk),
        grid=(rows // tt, n_exp // te),
        in_specs=[pl.BlockSpec((tt, d), lambda t, e: (t, 0))] + mod_specs + [
                  pl.BlockSpec((te, d), lambda t, e: (e, 0)),
                  pl.BlockSpec((d, te), lambda t, e: (0, e)),
                  sel, sel, sel, sel,
                  pl.BlockSpec((PEER_HEADS, tt), lambda t, e: (0, t))],
        out_specs=pl.BlockSpec((d, tt), lambda t, e: (0, t)),
        out_shape=jax.ShapeDtypeStruct((d, rows), F32),
        scratch_shapes=[pltpu.VMEM((tt, d), BF16)],
        compiler_params=_cparams(("parallel", "arbitrary")),
        name="peer_dense",
    )(x, mod, mod, u, vt, s0, s1, e0, e1, tau)


def _axial_angles(n, rot_dim):
    t = jnp.arange(n)
    rows = (t // GRID_W).astype(F32)
    cols = (t % GRID_W).astype(F32)
    nf = rot_dim // 4
    inv = ROPE_THETA ** (-jnp.arange(nf, dtype=F32) / nf)
    ang = jnp.concatenate([rows[:, None] * inv, cols[:, None] * inv], axis=-1)
    return jnp.cos(ang), jnp.sin(ang)


def _rope(x, cos, sin):
    half = x.shape[-1] // 2
    x1, x2 = x[..., :half], x[..., half:]
    c, s = cos[:, None, :], sin[:, None, :]
    return jnp.concatenate([x1 * c - x2 * s, x1 * s + x2 * c], axis=-1)


def _rms(x, g):
    ms = jnp.mean(jnp.square(x), axis=-1, keepdims=True)
    return (x * lax.rsqrt(ms + RMS_EPS)) * g


def _bhnd(t):
    return jnp.transpose(t, (0, 2, 1, 3)).astype(BF16)


def _flat(o):
    b, h, n, d = o.shape
    return jnp.transpose(o, (0, 2, 1, 3)).reshape(b * n, h * d)


def _padded_in_proj(w_in):
    d = w_in.shape[0]
    cols, o = [], 0
    for idx, n in enumerate(IN_SIZES):
        cols.append(w_in[:, o:o + n])
        o += n
        if idx == 2:
            cols.append(jnp.zeros((d, LANE - A_ROPE), w_in.dtype))
    used = sum(IN_SIZES) + LANE - A_ROPE
    cols.append(jnp.zeros((d, IN_PAD_WIDTH - used), w_in.dtype))
    return jnp.concatenate(cols, axis=1).astype(BF16)


def _in_offsets():
    offs, o = [], 0
    for idx, n in enumerate(IN_SIZES):
        offs.append(o)
        o += n
        if idx == 2:
            o += LANE - A_ROPE
    return offs


def kernel(x, c, ctx, c_ctx, w_ada, b_ada, w_in, a_q_norm, a_kv_norm, a_w_uq, a_w_ukv, b_sink, c_rpb, d_q_norm, d_k_norm, w_branch, w_gate, b_gate, w_out, ln1_g, ln1_b, ln2_g, ln2_b, peer_w_q, peer_sub_keys, peer_u, peer_v):
    bsz, seq, d = x.shape
    nctx = ctx.shape[1]
    depth = w_in.shape[0]
    t_lat, t_ctx = bsz * seq, bsz * nctx
    t_all = t_lat + t_ctx
    rows_grid = seq // GRID_W
    alpha = (2 * depth) ** 0.25

    rope_a = _axial_angles(seq, A_ROPE)
    rope_b = _axial_angles(seq, B_DIM)
    rope_d = _axial_angles(seq, D_DIM)
    band_bias = band_bias_table()
    offs = _in_offsets()

    n_groups = -(-(bsz + 1) // 8) * 8
    cc = jnp.zeros((n_groups, d), F32).at[:bsz].set(c).at[bsz].set(c_ctx)

    def group_of(tm):
        lat_blocks, per_batch = t_lat // tm, seq // tm
        assert t_lat % tm == 0 and seq % tm == 0
        return lambda i: jnp.where(i < lat_blocks, i // per_batch, bsz)

    xs = jnp.concatenate([x.reshape(t_lat, d), ctx.reshape(t_ctx, d)], axis=0)

    for l in range(depth):
        need_ctx = l < depth - 1
        rows_out = t_all if need_ctx else t_lat
        mod = ada_modulation(cc, w_ada[l], b_ada[l]).reshape(n_groups * 6, 1, d)

        u = token_matmul(xs, _padded_in_proj(w_in[l]), prologue="modulate", mod=mod, mod_idx=(0, 1),
                         group_of=group_of, out_dtype=F32, name="in_proj")
        w_gate_l = jnp.transpose(w_gate[l], (1, 0, 2)).reshape(d, N_BRANCH * d).astype(BF16)
        gates = token_matmul(xs, w_gate_l, rows=rows_out, prologue="modulate", mod=mod, mod_idx=(0, 1),
                             group_of=group_of, bias=b_gate[l].reshape(-1), act="sigmoid", out_dtype=BF16,
                             name="branch_gates")

        qa = token_matmul(u, a_w_uq[l].astype(BF16), x_cols=(A_Q_LORA, offs[0] // A_Q_LORA), prologue="rmsnorm",
                          norm_g=a_q_norm[l], name="mla_q_up")
        kva = token_matmul(u, a_w_ukv[l].astype(BF16), x_cols=(A_KV_LORA, offs[1] // A_KV_LORA), prologue="rmsnorm",
                           norm_g=a_kv_norm[l], name="mla_kv_up")
        qa = qa.reshape(t_all, A_HEADS, A_NOPE + A_ROPE)
        kva = kva.reshape(t_all, A_HEADS, A_NOPE + A_V)
        kr = u[:, offs[2]:offs[2] + A_ROPE]

        def lat(t):
            return t[:t_lat].reshape((bsz, seq) + t.shape[1:])

        def cx(t):
            return t[t_lat:].reshape((bsz, nctx) + t.shape[1:])

        qa_nope, qa_rope = qa[..., :A_NOPE], qa[..., A_NOPE:]
        ka_nope, va = kva[..., :A_NOPE], kva[..., A_NOPE:]
        qa_rope_l = _rope(lat(qa_rope), *rope_a)
        kr_l = _rope(lat(kr)[:, :, None, :], *rope_a)
        kr_c = cx(kr)[:, :, None, :]
        cat = lambda a, b_: jnp.concatenate([a, b_], axis=2)
        ka_nope_all = cat(_bhnd(lat(ka_nope)), _bhnd(cx(ka_nope)))
        kr_all = cat(_bhnd(kr_l), _bhnd(kr_c))
        va_all = cat(_bhnd(lat(va)), _bhnd(cx(va)))
        oa = dense_attention([_bhnd(lat(qa_nope)), _bhnd(qa_rope_l)], [ka_nope_all, kr_all], va_all, A_SCALE,
                             name="mla_attention")

        def cols(idx, heads):
            t = u[:, offs[idx]:offs[idx] + IN_SIZES[idx]]
            return t.reshape(t_all, heads, IN_SIZES[idx] // heads)

        bq, bk, bv = cols(3, B_HEADS), cols(4, B_KV_HEADS), cols(5, B_KV_HEADS)
        ob = window_attention(_bhnd(_rope(lat(bq), *rope_b)), _bhnd(_rope(lat(bk), *rope_b)), _bhnd(lat(bv)),
                              _bhnd(cx(bk)), _bhnd(cx(bv)), band_bias, B_SCALE, (B_TQ, B_BACK, B_WK),
                              sink=b_sink[l], name="swa_attention")

        cq, ck, cv = cols(6, C_HEADS), cols(7, C_HEADS), cols(8, C_HEADS)
        oc = window_attention(_bhnd(lat(cq)), _bhnd(lat(ck)), _bhnd(lat(cv)), _bhnd(cx(ck)), _bhnd(cx(cv)),
                              neighbourhood_bias_table(c_rpb[l], rows_grid), C_SCALE, (C_TQ, C_BACK, C_WK),
                              name="na_attention")

        dq, dk, dv = cols(9, D_HEADS), cols(10, D_KV_HEADS), cols(11, D_KV_HEADS)
        dq_n, dk_n = _rms(dq, d_q_norm[l]), _rms(dk, d_k_norm[l])
        kd_all = cat(_bhnd(_rope(lat(dk_n), *rope_d)), _bhnd(cx(dk_n)))
        vd_all = cat(_bhnd(lat(dv)), _bhnd(cx(dv)))
        od = dense_attention([_bhnd(_rope(lat(dq_n), *rope_d))], [kd_all], vd_all, D_SCALE, name="gqa_attention")

        branches = [_flat(oa), _flat(ob), _flat(oc), _flat(od)]
        if need_ctx:
            oa_c = dense_attention([_bhnd(cx(qa_nope)), _bhnd(cx(qa_rope))], [_bhnd(cx(ka_nope)), _bhnd(kr_c)],
                                   _bhnd(cx(va)), A_SCALE, name="mla_attention_ctx")
            ob_c = dense_attention([_bhnd(cx(bq))], [_bhnd(cx(bk))], _bhnd(cx(bv)), B_SCALE, sink=b_sink[l],
                                   name="swa_attention_ctx")
            oc_c = dense_attention([_bhnd(cx(cq))], [_bhnd(cx(ck))], _bhnd(cx(cv)), C_SCALE,
                                   name="na_attention_ctx")
            od_c = dense_attention([_bhnd(cx(dq_n))], [_bhnd(cx(dk_n))], _bhnd(cx(dv)), D_SCALE,
                                   name="gqa_attention_ctx")
            branches = [jnp.concatenate([bl, bc], axis=0)
                        for bl, bc in zip(branches, [_flat(oa_c), _flat(ob_c), _flat(oc_c), _flat(od_c)])]

        y = merge_branches(jnp.stack(branches), w_branch[l].astype(BF16), gates)
        mix = token_matmul(y, w_out[l].astype(BF16), name="out_proj")
        xs = residual_ln(xs, mix, mod, 2, group_of, ln1_g[l], ln1_b[l], alpha)

        pq = token_matmul(xs, peer_w_q[l].astype(BF16), prologue="modulate", mod=mod, mod_idx=(3, 4),
                          group_of=group_of, out_dtype=BF16, name="peer_query")
        keys = peer_sub_keys[l].reshape(2 * PEER_HEADS, PEER_KEYS, PEER_QDIM // 2).astype(BF16)
        s0, s1, e0, e1, tau = peer_topk(pq, keys)
        ffn_t = peer_dense(xs, mod, (3, 4), group_of, peer_u[l].astype(BF16), jnp.transpose(peer_v[l]).astype(BF16),
                           s0, s1, e0, e1, tau)
        xs = residual_ln(xs, jnp.transpose(ffn_t), mod, 5, group_of, ln2_g[l], ln2_b[l], alpha)

    return xs[:t_lat].reshape(bsz, seq, d)
```

```python
import functools
import math

import jax
import jax.numpy as jnp
import numpy as np
from jax import lax
from jax.experimental import pallas as pl
from jax.experimental.pallas import tpu as pltpu

F32 = jnp.float32
BF16 = jnp.bfloat16

GRID_W = 64
ROPE_THETA = 10000.0
LN_EPS = 1e-5
RMS_EPS = 1e-6
NEG_INF = -1e30

A_HEADS, A_NOPE, A_ROPE, A_V, A_Q_LORA, A_KV_LORA = 4, 128, 64, 128, 384, 128
B_HEADS, B_KV_HEADS, B_DIM, B_WINDOW = 8, 2, 64, 128
C_HEADS, C_DIM, NA_ROWS, NA_COLS = 4, 128, 8, 16
D_HEADS, D_KV_HEADS, D_DIM = 4, 2, 128
N_BRANCH, BRANCH_W = 4, 512
PEER_HEADS, PEER_KEYS, PEER_QDIM, PEER_TOPK = 8, 128, 256, 16

A_SCALE = (A_NOPE + A_ROPE) ** -0.5
B_SCALE = B_DIM ** -0.5
C_SCALE = C_DIM ** -0.5
D_SCALE = D_DIM ** -0.5

IN_SIZES = (A_Q_LORA, A_KV_LORA, A_ROPE,
            B_HEADS * B_DIM, B_KV_HEADS * B_DIM, B_KV_HEADS * B_DIM,
            C_HEADS * C_DIM, C_HEADS * C_DIM, C_HEADS * C_DIM,
            D_HEADS * D_DIM, D_KV_HEADS * D_DIM, D_KV_HEADS * D_DIM)
LANE = 128
IN_PAD_WIDTH = 4096

VMEM_LIMIT = 56 * 1024 * 1024

PEER_CHUNK = 256
PEER_EB = 512

B_TQ, B_BACK, B_WK = 512, 128, 768
C_TQ, C_BACK, C_WK = 4 * GRID_W, 4 * GRID_W, 12 * GRID_W


def _cparams(sem):
    return pltpu.CompilerParams(dimension_semantics=sem, vmem_limit_bytes=VMEM_LIMIT)


def _pick(n, cands):
    for c in cands:
        if n % c == 0:
            return c
    raise ValueError(f"no tile in {cands} divides {n}")


def _dot(a, b):
    return jnp.dot(a, b, preferred_element_type=F32)


def _dot_nt(a, b):
    return lax.dot_general(a, b, (((1,), (1,)), ((), ())), preferred_element_type=F32)


def _ada_kernel(c_ref, w_ref, b_ref, o_ref):
    c = c_ref[...]
    s = c * (1.0 / (1.0 + jnp.exp(-c)))
    w = w_ref[...]
    s_hi = s.astype(BF16)
    s_lo = (s - s_hi.astype(F32)).astype(BF16)
    w_hi = w.astype(BF16)
    w_lo = (w - w_hi.astype(F32)).astype(BF16)
    acc = _dot(s_hi, w_hi) + _dot(s_lo, w_hi) + _dot(s_hi, w_lo)
    o_ref[...] = acc + b_ref[...]


def ada_modulation(cc, w, b):
    r, k = cc.shape
    n = w.shape[1]
    tn = _pick(n, (512, 256, 128))
    return pl.pallas_call(
        _ada_kernel,
        grid=(n // tn,),
        in_specs=[pl.BlockSpec((r, k), lambda j: (0, 0)),
                  pl.BlockSpec((k, tn), lambda j: (0, j)),
                  pl.BlockSpec((1, tn), lambda j: (0, j))],
        out_specs=pl.BlockSpec((r, tn), lambda j: (0, j)),
        out_shape=jax.ShapeDtypeStruct((r, n), F32),
        compiler_params=_cparams(("arbitrary",)),
        name="ada_modulation",
    )(cc, w, b.reshape(1, n))


def _mm_kernel(*refs, prologue, has_bias, act):
    it = iter(refs)
    x_ref = next(it)
    if prologue == "modulate":
        sh_ref, sc_ref = next(it), next(it)
    elif prologue == "rmsnorm":
        g_ref = next(it)
    w_ref = next(it)
    b_ref = next(it) if has_bias else None
    o_ref = next(it)
    a_scr = next(it)

    @pl.when(pl.program_id(1) == 0)
    def _():
        x = x_ref[...].astype(F32)
        if prologue == "modulate":
            a = x * (1.0 + sc_ref[0]) + sh_ref[0]
        elif prologue == "rmsnorm":
            ms = jnp.mean(x * x, axis=-1, keepdims=True)
            a = (x * lax.rsqrt(ms + RMS_EPS)) * g_ref[...]
        else:
            a = x
        a_scr[...] = a.astype(BF16)

    acc = _dot(a_scr[...], w_ref[...])
    if has_bias:
        acc = acc + b_ref[...]
    if act == "sigmoid":
        acc = 1.0 / (1.0 + jnp.exp(-acc))
    o_ref[...] = acc.astype(o_ref.dtype)


def token_matmul(x, w, *, rows=None, x_cols=None, prologue=None, mod=None, mod_idx=None, group_of=None,
                 norm_g=None, bias=None, act=None, out_dtype=F32, name="token_matmul"):
    rows = x.shape[0] if rows is None else rows
    kdim, n = w.shape
    kw, kblk = (x.shape[1], 0) if x_cols is None else x_cols
    assert kw == kdim
    tm = _pick(rows, (512, 256))
    tn = _pick(n, (1024, 512, 256, 128))
    in_specs = [pl.BlockSpec((tm, kdim), lambda i, j: (i, kblk))]
    args = [x]
    if prologue == "modulate":
        tm_group = group_of(tm)
        for slot in mod_idx:
            in_specs.append(pl.BlockSpec((1, 1, kdim), lambda i, j, s=slot: (tm_group(i) * 6 + s, 0, 0)))
            args.append(mod)
    elif prologue == "rmsnorm":
        in_specs.append(pl.BlockSpec((1, kdim), lambda i, j: (0, 0)))
        args.append(norm_g.reshape(1, kdim))
    in_specs.append(pl.BlockSpec((kdim, tn), lambda i, j: (0, j)))
    args.append(w)
    if bias is not None:
        in_specs.append(pl.BlockSpec((1, tn), lambda i, j: (0, j)))
        args.append(bias.reshape(1, n))
    return pl.pallas_call(
        functools.partial(_mm_kernel, prologue=prologue, has_bias=bias is not None, act=act),
        grid=(rows // tm, n // tn),
        in_specs=in_specs,
        out_specs=pl.BlockSpec((tm, tn), lambda i, j: (i, j)),
        out_shape=jax.ShapeDtypeStruct((rows, n), out_dtype),
        scratch_shapes=[pltpu.VMEM((tm, kdim), BF16)],
        compiler_params=_cparams(("parallel", "arbitrary")),
        name=name,
    )(*args)


def _res_ln_kernel(x_ref, y_ref, gate_ref, g_ref, b_ref, o_ref, *, alpha):
    z = alpha * x_ref[...] + gate_ref[0] * y_ref[...].astype(F32)
    mu = jnp.mean(z, axis=-1, keepdims=True)
    zc = z - mu
    var = jnp.mean(zc * zc, axis=-1, keepdims=True)
    o_ref[...] = (zc * lax.rsqrt(var + LN_EPS)) * g_ref[...] + b_ref[...]


def residual_ln(x, y, mod, gate_slot, group_of, ln_g, ln_b, alpha):
    rows, d = y.shape
    tm = _pick(rows, (256,))
    tm_group = group_of(tm)
    return pl.pallas_call(
        functools.partial(_res_ln_kernel, alpha=alpha),
        grid=(rows // tm,),
        in_specs=[pl.BlockSpec((tm, d), lambda i: (i, 0)),
                  pl.BlockSpec((tm, d), lambda i: (i, 0)),
                  pl.BlockSpec((1, 1, d), lambda i: (tm_group(i) * 6 + gate_slot, 0, 0)),
                  pl.BlockSpec((1, d), lambda i: (0, 0)),
                  pl.BlockSpec((1, d), lambda i: (0, 0))],
        out_specs=pl.BlockSpec((tm, d), lambda i: (i, 0)),
        out_shape=jax.ShapeDtypeStruct((rows, d), F32),
        compiler_params=_cparams(("parallel",)),
        name="residual_ln",
    )(x, y, mod, ln_g.reshape(1, d), ln_b.reshape(1, d))


def _dense_attn_kernel(*refs, n_parts, scale, has_sink):
    q_refs = refs[:n_parts]
    k_refs = refs[n_parts:2 * n_parts]
    v_ref = refs[2 * n_parts]
    sink_ref = refs[2 * n_parts + 1] if has_sink else None
    o_ref = refs[-1]
    s = _dot_nt(q_refs[0][0, 0], k_refs[0][0, 0])
    for qr, kr in zip(q_refs[1:], k_refs[1:]):
        s = s + _dot_nt(qr[0, 0], kr[0, 0])
    s = s * scale
    m = jnp.max(s, axis=-1, keepdims=True)
    if has_sink:
        sink = sink_ref[pl.program_id(1)]
        m = jnp.maximum(m, sink)
    p = jnp.exp(s - m)
    l = jnp.sum(p, axis=-1, keepdims=True)
    if has_sink:
        l = l + jnp.exp(sink - m)
    o = _dot(p.astype(BF16), v_ref[0, 0])
    o_ref[0, 0] = (o / l).astype(o_ref.dtype)


def dense_attention(q_parts, k_parts, v, scale, sink=None, name="dense_attention"):
    b, hq, nq, _ = q_parts[0].shape
    hkv, nk, dv = v.shape[1], v.shape[2], v.shape[3]
    g = hq // hkv
    tq = _pick(nq, (256, 128))
    in_specs, args = [], []
    for q in q_parts:
        in_specs.append(pl.BlockSpec((1, 1, tq, q.shape[3]), lambda bi, h, i: (bi, h, i, 0)))
        args.append(q)
    for k in k_parts:
        if k.shape[1] == 1:
            in_specs.append(pl.BlockSpec((1, 1, nk, k.shape[3]), lambda bi, h, i: (bi, 0, 0, 0)))
        else:
            in_specs.append(pl.BlockSpec((1, 1, nk, k.shape[3]), lambda bi, h, i: (bi, h // g, 0, 0)))
        args.append(k)
    in_specs.append(pl.BlockSpec((1, 1, nk, dv), lambda bi, h, i: (bi, h // g, 0, 0)))
    args.append(v)
    if sink is not None:
        in_specs.append(pl.BlockSpec(memory_space=pltpu.SMEM))
        args.append(sink.astype(F32))
    return pl.pallas_call(
        functools.partial(_dense_attn_kernel, n_parts=len(q_parts), scale=scale, has_sink=sink is not None),
        grid=(b, hq, nq // tq),
        in_specs=in_specs,
        out_specs=pl.BlockSpec((1, 1, tq, dv), lambda bi, h, i: (bi, h, i, 0)),
        out_shape=jax.ShapeDtypeStruct((b, hq, nq, dv), BF16),
        compiler_params=_cparams(("parallel", "parallel", "arbitrary")),
        name=name,
    )(*args)


def _window_attn_kernel(*refs, scale, has_sink, tq, back, wk, n_lat):
    q_ref, k_ref, v_ref, kc_ref, vc_ref, bias_ref = refs[:6]
    sink_ref = refs[6] if has_sink else None
    o_ref = refs[-1]
    i = pl.program_id(2)
    start = pl.multiple_of(jnp.clip(i * tq - back, 0, n_lat - wk), LANE)
    q = q_ref[0, 0]
    kw = k_ref[0, 0, pl.ds(start, wk), :]
    vw = v_ref[0, 0, pl.ds(start, wk), :]
    s1 = _dot_nt(q, kw) * scale + bias_ref[0, 0]
    s2 = _dot_nt(q, kc_ref[0, 0]) * scale
    m = jnp.maximum(jnp.max(s1, axis=-1, keepdims=True), jnp.max(s2, axis=-1, keepdims=True))
    if has_sink:
        sink = sink_ref[pl.program_id(1)]
        m = jnp.maximum(m, sink)
    p1 = jnp.exp(s1 - m)
    p2 = jnp.exp(s2 - m)
    l = jnp.sum(p1, axis=-1, keepdims=True) + jnp.sum(p2, axis=-1, keepdims=True)
    if has_sink:
        l = l + jnp.exp(sink - m)
    o = _dot(p1.astype(BF16), vw) + _dot(p2.astype(BF16), vc_ref[0, 0])
    o_ref[0, 0] = (o / l).astype(o_ref.dtype)


def window_attention(q, k, v, kc, vc, bias, scale, geom, sink=None, name="window_attention"):
    tq, back, wk = geom
    b, hq, n, d = q.shape
    hkv, nc = k.shape[1], kc.shape[2]
    g = hq // hkv
    nqb = n // tq
    assert n % tq == 0 and nqb >= 2 and n >= wk
    per_head_bias = bias.shape[1] != 1

    def bias_map(bi, h, i):
        cls = jnp.where(i == 0, 0, jnp.where(i == nqb - 1, 2, 1))
        return (cls, h if per_head_bias else 0, 0, 0)

    in_specs = [pl.BlockSpec((1, 1, tq, d), lambda bi, h, i: (bi, h, i, 0)),
                pl.BlockSpec((1, 1, n, d), lambda bi, h, i: (bi, h // g, 0, 0)),
                pl.BlockSpec((1, 1, n, d), lambda bi, h, i: (bi, h // g, 0, 0)),
                pl.BlockSpec((1, 1, nc, d), lambda bi, h, i: (bi, h // g, 0, 0)),
                pl.BlockSpec((1, 1, nc, d), lambda bi, h, i: (bi, h // g, 0, 0)),
                pl.BlockSpec((1, 1, tq, wk), bias_map)]
    args = [q, k, v, kc, vc, bias]
    if sink is not None:
        in_specs.append(pl.BlockSpec(memory_space=pltpu.SMEM))
        args.append(sink.astype(F32))
    return pl.pallas_call(
        functools.partial(_window_attn_kernel, scale=scale, has_sink=sink is not None, tq=tq, back=back, wk=wk,
                          n_lat=n),
        grid=(b, hq, nqb),
        in_specs=in_specs,
        out_specs=pl.BlockSpec((1, 1, tq, d), lambda bi, h, i: (bi, h, i, 0)),
        out_shape=jax.ShapeDtypeStruct((b, hq, n, d), BF16),
        compiler_params=_cparams(("parallel", "parallel", "arbitrary")),
        name=name,
    )(*args)


def _window_q_offsets(geom):
    tq, back, wk = geom
    return (0, back, wk - tq)


def band_bias_table():
    tq, _, wk = (B_TQ, B_BACK, B_WK)
    kpos = np.arange(wk)[None, :]
    tabs = []
    for off in _window_q_offsets((B_TQ, B_BACK, B_WK)):
        qpos = off + np.arange(tq)[:, None]
        tabs.append(np.where(np.abs(qpos - kpos) <= B_WINDOW, 0.0, NEG_INF))
    return jnp.asarray(np.stack(tabs)[:, None], F32)


def neighbourhood_bias_table(rpb, rows):
    tq, back, wk = (C_TQ, C_BACK, C_WK)
    assert rows >= wk // GRID_W and rows >= NA_ROWS
    nqb = rows * GRID_W // tq
    nq_rows, nk_rows = tq // GRID_W, wk // GRID_W
    cols = np.arange(GRID_W)
    col_start = np.clip(cols - NA_COLS // 2, 0, GRID_W - NA_COLS)
    col_ok = (cols[None, :] >= col_start[:, None]) & (cols[None, :] < col_start[:, None] + NA_COLS)
    col_off = np.clip(cols[None, :] - cols[:, None] + (NA_COLS - 1), 0, 2 * NA_COLS - 2)
    onehot_c = jnp.asarray(col_off[..., None] == np.arange(2 * NA_COLS - 1), F32)
    tabs = []
    for cls, off in enumerate(_window_q_offsets((C_TQ, C_BACK, C_WK))):
        blk = (0, 1, nqb - 1)[cls]
        w0 = int(np.clip(blk * tq - back, 0, rows * GRID_W - wk)) // GRID_W
        assert w0 * GRID_W + off == blk * tq and off % GRID_W == 0
        q_row = w0 + off // GRID_W + np.arange(nq_rows)
        k_row = w0 + np.arange(nk_rows)
        row_start = np.clip(q_row - NA_ROWS // 2, 0, rows - NA_ROWS)
        row_ok = (k_row[None, :] >= row_start[:, None]) & (k_row[None, :] < row_start[:, None] + NA_ROWS)
        row_off = np.clip(k_row[None, :] - q_row[:, None] + (NA_ROWS - 1), 0, 2 * NA_ROWS - 2)
        by_row = rpb[:, row_off, :].astype(F32)
        vals = jnp.einsum("hrsc,qkc->hrqsk", by_row, onehot_c, precision=lax.Precision.HIGHEST)
        ok = row_ok[:, None, :, None] & col_ok[None, :, None, :]
        tabs.append(jnp.where(jnp.asarray(ok)[None], vals, NEG_INF).reshape(rpb.shape[0], tq, wk))
    return jnp.stack(tabs)


def _merge_kernel(o_ref, w_ref, g_ref, y_ref, acc_ref):
    br = pl.program_id(2)
    t = g_ref[...].astype(F32) * _dot(o_ref[0], w_ref[0])

    @pl.when(br == 0)
    def _():
        acc_ref[...] = t

    @pl.when(br > 0)
    def _():
        acc_ref[...] += t

    @pl.when(br == pl.num_programs(2) - 1)
    def _():
        y_ref[...] = acc_ref[...].astype(y_ref.dtype)


def merge_branches(o, w_branch, gates):
    nbr, rows, bw = o.shape
    d = w_branch.shape[2]
    tm = _pick(rows, (512, 256))
    tn = _pick(d, (512, 256, 128))
    nj = d // tn
    return pl.pallas_call(
        _merge_kernel,
        grid=(rows // tm, nj, nbr),
        in_specs=[pl.BlockSpec((1, tm, bw), lambda i, j, r: (r, i, 0)),
                  pl.BlockSpec((1, bw, tn), lambda i, j, r: (r, 0, j)),
                  pl.BlockSpec((tm, tn), lambda i, j, r: (i, r * nj + j))],
        out_specs=pl.BlockSpec((tm, tn), lambda i, j, r: (i, j)),
        out_shape=jax.ShapeDtypeStruct((rows, d), BF16),
        scratch_shapes=[pltpu.VMEM((tm, tn), F32)],
        compiler_params=_cparams(("parallel", "parallel", "arbitrary")),
        name="merge_branches",
    )(o, w_branch, gates)


def _peer_pairs():
    return [(a, b) for a in range(PEER_TOPK) for b in range(PEER_TOPK) if (a + 1) * (b + 1) <= PEER_TOPK]


def _peer_topk_kernel(q_ref, keys_ref, s0_ref, s1_ref, e0_ref, e1_ref, tau_ref):
    tops = [[[None] * PEER_HEADS for _ in range(PEER_TOPK)] for _ in range(2)]
    for h in range(PEER_HEADS):
        for p in range(2):
            hp = 2 * h + p
            qs = q_ref[:, hp * PEER_KEYS:(hp + 1) * PEER_KEYS]
            s = _dot_nt(keys_ref[hp], qs)
            (s0_ref if p == 0 else s1_ref)[h, 0] = s
            work = s
            for r in range(PEER_TOPK):
                m = jnp.max(work, axis=0, keepdims=True)
                tops[p][r][h] = m
                if r + 1 < PEER_TOPK:
                    work = jnp.where(work == m, -jnp.inf, work)
    top = [[jnp.concatenate(tops[p][r], axis=0) for r in range(PEER_TOPK)] for p in range(2)]
    cands = [top[0][a] + top[1][b] for a, b in _peer_pairs()]
    work = list(cands)
    for r in range(PEER_TOPK):
        m = functools.reduce(jnp.maximum, work)
        if r + 1 < PEER_TOPK:
            work = [jnp.where(w == m, -jnp.inf, w) for w in work]
    tau = m
    cmax = cands[0]
    z = functools.reduce(jnp.add, [jnp.where(c >= tau, jnp.exp(c - cmax), 0.0) for c in cands])
    rz = 1.0 / z
    for h in range(PEER_HEADS):
        tau_ref[h, 0] = tau[h:h + 1]
        e0_ref[h, 0] = jnp.exp(s0_ref[h, 0] - top[0][0][h:h + 1]) * rz[h:h + 1]
        e1_ref[h, 0] = jnp.exp(s1_ref[h, 0] - top[1][0][h:h + 1])


def peer_topk(q, keys):
    rows = q.shape[0]
    tt = PEER_CHUNK
    nch = rows // tt
    big = pl.BlockSpec((PEER_HEADS, 1, PEER_KEYS, tt), lambda i: (0, i, 0, 0))
    big_shape = jax.ShapeDtypeStruct((PEER_HEADS, nch, PEER_KEYS, tt), F32)
    return pl.pallas_call(
        _peer_topk_kernel,
        grid=(nch,),
        in_specs=[pl.BlockSpec((tt, q.shape[1]), lambda i: (i, 0)),
                  pl.BlockSpec(keys.shape, lambda i: (0, 0, 0))],
        out_specs=[big, big, big, big, pl.BlockSpec((PEER_HEADS, 1, 1, tt), lambda i: (0, i, 0, 0))],
        out_shape=[big_shape, big_shape, big_shape, big_shape,
                   jax.ShapeDtypeStruct((PEER_HEADS, nch, 1, tt), F32)],
        compiler_params=_cparams(("parallel",)),
        name="peer_topk",
    )(q, keys)


def _modulate_t_kernel(x_ref, sh_ref, sc_ref, o_ref):
    h = x_ref[...] * (1.0 + sc_ref[0]) + sh_ref[0]
    o_ref[0] = h.T.astype(BF16)


def modulate_transposed(x, mod, mod_idx, group_of):
    rows, d = x.shape
    tt = PEER_CHUNK
    tt_group = group_of(tt)
    mod_specs = [pl.BlockSpec((1, 1, d), lambda i, s=slot: (tt_group(i) * 6 + s, 0, 0)) for slot in mod_idx]
    return pl.pallas_call(
        _modulate_t_kernel,
        grid=(rows // tt,),
        in_specs=[pl.BlockSpec((tt, d), lambda i: (i, 0))] + mod_specs,
        out_specs=pl.BlockSpec((1, d, tt), lambda i: (i, 0, 0)),
        out_shape=jax.ShapeDtypeStruct((rows // tt, d, tt), BF16),
        compiler_params=_cparams(("parallel",)),
        name="modulate_transposed",
    )(x, mod, mod)


def _peer_dense_kernel(ht_ref, u_ref, vt_ref, s0_ref, s1_ref, e0_ref, e1_ref, tau_ref, o_ref, a_scr, w_scr, p_scr,
                       *, n_lane, n_eb):
    e = pl.program_id(1)
    n = n_lane * n_eb
    i_per_eb = PEER_EB // PEER_KEYS

    @pl.when(e == 0)
    def _():
        o_ref[...] = jnp.zeros_like(o_ref)

    def up_and_weights(c, slot):
        eb, lc = c // n_lane, c % n_lane
        a_scr[slot] = _dot(u_ref[eb * PEER_EB:(eb + 1) * PEER_EB, :], ht_ref[lc])
        for ii in range(i_per_eb):
            i = (e * n_eb + eb) * i_per_eb + ii
            w = None
            for h in range(PEER_HEADS):
                s_sum = s0_ref[h, lc, pl.ds(i, 1), :] + s1_ref[h, lc]
                gate = e0_ref[h, lc, pl.ds(i, 1), :] * e1_ref[h, lc]
                t = jnp.where(s_sum >= tau_ref[h, lc], gate, 0.0)
                w = t if w is None else w + t
            w_scr[slot, ii * PEER_KEYS:(ii + 1) * PEER_KEYS, :] = w

    def gate_and_down(c, slot):
        eb, lc = c // n_lane, c % n_lane
        for ii in range(i_per_eb):
            sl = slice(ii * PEER_KEYS, (ii + 1) * PEER_KEYS)
            a = a_scr[slot, sl, :]
            act = 0.5 * a * (1.0 + lax.erf(a * (1.0 / math.sqrt(2.0))))
            p_scr[slot, sl, :] = (w_scr[slot, sl, :] * act).astype(BF16)
        o_ref[lc] += _dot(vt_ref[eb], p_scr[slot])

    up_and_weights(0, 0)
    for c in range(n):
        if c + 1 < n:
            up_and_weights(c + 1, (c + 1) % 2)
        gate_and_down(c, c % 2)


def peer_dense(ht, u, vt, s0, s1, e0, e1, tau):
    nch, d, tc = ht.shape
    n_exp = u.shape[0]
    n_lane = _pick(nch, (2, 1))
    n_eb = 2
    te = n_eb * PEER_EB
    sel = pl.BlockSpec((PEER_HEADS, n_lane, PEER_KEYS, tc), lambda t, e: (0, t, 0, 0))
    return pl.pallas_call(
        functools.partial(_peer_dense_kernel, n_lane=n_lane, n_eb=n_eb),
        grid=(nch // n_lane, n_exp // te),
        in_specs=[pl.BlockSpec((n_lane, d, tc), lambda t, e: (t, 0, 0)),
                  pl.BlockSpec((te, d), lambda t, e: (e, 0)),
                  pl.BlockSpec((n_eb, d, PEER_EB), lambda t, e: (e, 0, 0)),
                  sel, sel, sel, sel,
                  pl.BlockSpec((PEER_HEADS, n_lane, 1, tc), lambda t, e: (0, t, 0, 0))],
        out_specs=pl.BlockSpec((n_lane, d, tc), lambda t, e: (t, 0, 0)),
        out_shape=jax.ShapeDtypeStruct((nch, d, tc), F32),
        scratch_shapes=[pltpu.VMEM((2, PEER_EB, tc), F32), pltpu.VMEM((2, PEER_EB, tc), F32),
                        pltpu.VMEM((2, PEER_EB, tc), BF16)],
        compiler_params=_cparams(("parallel", "arbitrary")),
        name="peer_dense",
    )(ht, u, vt, s0, s1, e0, e1, tau)


def _axial_angles(n, rot_dim):
    t = jnp.arange(n)
    rows = (t // GRID_W).astype(F32)
    cols = (t % GRID_W).astype(F32)
    nf = rot_dim // 4
    inv = ROPE_THETA ** (-jnp.arange(nf, dtype=F32) / nf)
    ang = jnp.concatenate([rows[:, None] * inv, cols[:, None] * inv], axis=-1)
    return jnp.cos(ang), jnp.sin(ang)


def _rope(x, cos, sin):
    half = x.shape[-1] // 2
    x1, x2 = x[..., :half], x[..., half:]
    c, s = cos[:, None, :], sin[:, None, :]
    return jnp.concatenate([x1 * c - x2 * s, x1 * s + x2 * c], axis=-1)


def _rms(x, g):
    ms = jnp.mean(jnp.square(x), axis=-1, keepdims=True)
    return (x * lax.rsqrt(ms + RMS_EPS)) * g


def _bhnd(t):
    return jnp.transpose(t, (0, 2, 1, 3)).astype(BF16)


def _flat(o):
    b, h, n, d = o.shape
    return jnp.transpose(o, (0, 2, 1, 3)).reshape(b * n, h * d)


def _padded_in_proj(w_in):
    d = w_in.shape[0]
    cols, o = [], 0
    for idx, n in enumerate(IN_SIZES):
        cols.append(w_in[:, o:o + n])
        o += n
        if idx == 2:
            cols.append(jnp.zeros((d, LANE - A_ROPE), w_in.dtype))
    used = sum(IN_SIZES) + LANE - A_ROPE
    cols.append(jnp.zeros((d, IN_PAD_WIDTH - used), w_in.dtype))
    return jnp.concatenate(cols, axis=1).astype(BF16)


def _in_offsets():
    offs, o = [], 0
    for idx, n in enumerate(IN_SIZES):
        offs.append(o)
        o += n
        if idx == 2:
            o += LANE - A_ROPE
    return offs


def kernel(x, c, ctx, c_ctx, w_ada, b_ada, w_in, a_q_norm, a_kv_norm, a_w_uq, a_w_ukv, b_sink, c_rpb, d_q_norm, d_k_norm, w_branch, w_gate, b_gate, w_out, ln1_g, ln1_b, ln2_g, ln2_b, peer_w_q, peer_sub_keys, peer_u, peer_v):
    bsz, seq, d = x.shape
    nctx = ctx.shape[1]
    depth = w_in.shape[0]
    t_lat, t_ctx = bsz * seq, bsz * nctx
    t_all = t_lat + t_ctx
    rows_grid = seq // GRID_W
    alpha = (2 * depth) ** 0.25

    rope_a = _axial_angles(seq, A_ROPE)
    rope_b = _axial_angles(seq, B_DIM)
    rope_d = _axial_angles(seq, D_DIM)
    band_bias = band_bias_table()
    offs = _in_offsets()

    n_groups = -(-(bsz + 1) // 8) * 8
    cc = jnp.zeros((n_groups, d), F32).at[:bsz].set(c).at[bsz].set(c_ctx)

    def group_of(tm):
        lat_blocks, per_batch = t_lat // tm, seq // tm
        assert t_lat % tm == 0 and seq % tm == 0
        return lambda i: jnp.where(i < lat_blocks, i // per_batch, bsz)

    xs = jnp.concatenate([x.reshape(t_lat, d), ctx.reshape(t_ctx, d)], axis=0)

    for l in range(depth):
        need_ctx = l < depth - 1
        rows_out = t_all if need_ctx else t_lat
        mod = ada_modulation(cc, w_ada[l], b_ada[l]).reshape(n_groups * 6, 1, d)

        u = token_matmul(xs, _padded_in_proj(w_in[l]), prologue="modulate", mod=mod, mod_idx=(0, 1),
                         group_of=group_of, out_dtype=F32, name="in_proj")
        w_gate_l = jnp.transpose(w_gate[l], (1, 0, 2)).reshape(d, N_BRANCH * d).astype(BF16)
        gates = token_matmul(xs, w_gate_l, rows=rows_out, prologue="modulate", mod=mod, mod_idx=(0, 1),
                             group_of=group_of, bias=b_gate[l].reshape(-1), act="sigmoid", out_dtype=BF16,
                             name="branch_gates")

        qa = token_matmul(u, a_w_uq[l].astype(BF16), x_cols=(A_Q_LORA, offs[0] // A_Q_LORA), prologue="rmsnorm",
                          norm_g=a_q_norm[l], name="mla_q_up")
        kva = token_matmul(u, a_w_ukv[l].astype(BF16), x_cols=(A_KV_LORA, offs[1] // A_KV_LORA), prologue="rmsnorm",
                           norm_g=a_kv_norm[l], name="mla_kv_up")
        qa = qa.reshape(t_all, A_HEADS, A_NOPE + A_ROPE)
        kva = kva.reshape(t_all, A_HEADS, A_NOPE + A_V)
        kr = u[:, offs[2]:offs[2] + A_ROPE]

        def lat(t):
            return t[:t_lat].reshape((bsz, seq) + t.shape[1:])

        def cx(t):
            return t[t_lat:].reshape((bsz, nctx) + t.shape[1:])

        qa_nope, qa_rope = qa[..., :A_NOPE], qa[..., A_NOPE:]
        ka_nope, va = kva[..., :A_NOPE], kva[..., A_NOPE:]
        qa_rope_l = _rope(lat(qa_rope), *rope_a)
        kr_l = _rope(lat(kr)[:, :, None, :], *rope_a)
        kr_c = cx(kr)[:, :, None, :]
        cat = lambda a, b_: jnp.concatenate([a, b_], axis=2)
        ka_nope_all = cat(_bhnd(lat(ka_nope)), _bhnd(cx(ka_nope)))
        kr_all = cat(_bhnd(kr_l), _bhnd(kr_c))
        va_all = cat(_bhnd(lat(va)), _bhnd(cx(va)))
        oa = dense_attention([_bhnd(lat(qa_nope)), _bhnd(qa_rope_l)], [ka_nope_all, kr_all], va_all, A_SCALE,
                             name="mla_attention")

        def cols(idx, heads):
            t = u[:, offs[idx]:offs[idx] + IN_SIZES[idx]]
            return t.reshape(t_all, heads, IN_SIZES[idx] // heads)

        bq, bk, bv = cols(3, B_HEADS), cols(4, B_KV_HEADS), cols(5, B_KV_HEADS)
        ob = window_attention(_bhnd(_rope(lat(bq), *rope_b)), _bhnd(_rope(lat(bk), *rope_b)), _bhnd(lat(bv)),
                              _bhnd(cx(bk)), _bhnd(cx(bv)), band_bias, B_SCALE, (B_TQ, B_BACK, B_WK),
                              sink=b_sink[l], name="swa_attention")

        cq, ck, cv = cols(6, C_HEADS), cols(7, C_HEADS), cols(8, C_HEADS)
        oc = window_attention(_bhnd(lat(cq)), _bhnd(lat(ck)), _bhnd(lat(cv)), _bhnd(cx(ck)), _bhnd(cx(cv)),
                              neighbourhood_bias_table(c_rpb[l], rows_grid), C_SCALE, (C_TQ, C_BACK, C_WK),
                              name="na_attention")

        dq, dk, dv = cols(9, D_HEADS), cols(10, D_KV_HEADS), cols(11, D_KV_HEADS)
        dq_n, dk_n = _rms(dq, d_q_norm[l]), _rms(dk, d_k_norm[l])
        kd_all = cat(_bhnd(_rope(lat(dk_n), *rope_d)), _bhnd(cx(dk_n)))
        vd_all = cat(_bhnd(lat(dv)), _bhnd(cx(dv)))
        od = dense_attention([_bhnd(_rope(lat(dq_n), *rope_d))], [kd_all], vd_all, D_SCALE, name="gqa_attention")

        branches = [_flat(oa), _flat(ob), _flat(oc), _flat(od)]
        if need_ctx:
            oa_c = dense_attention([_bhnd(cx(qa_nope)), _bhnd(cx(qa_rope))], [_bhnd(cx(ka_nope)), _bhnd(kr_c)],
                                   _bhnd(cx(va)), A_SCALE, name="mla_attention_ctx")
            ob_c = dense_attention([_bhnd(cx(bq))], [_bhnd(cx(bk))], _bhnd(cx(bv)), B_SCALE, sink=b_sink[l],
                                   name="swa_attention_ctx")
            oc_c = dense_attention([_bhnd(cx(cq))], [_bhnd(cx(ck))], _bhnd(cx(cv)), C_SCALE,
                                   name="na_attention_ctx")
            od_c = dense_attention([_bhnd(cx(dq_n))], [_bhnd(cx(dk_n))], _bhnd(cx(dv)), D_SCALE,
                                   name="gqa_attention_ctx")
            branches = [jnp.concatenate([bl, bc], axis=0)
                        for bl, bc in zip(branches, [_flat(oa_c), _flat(ob_c), _flat(oc_c), _flat(od_c)])]

        y = merge_branches(jnp.stack(branches), w_branch[l].astype(BF16), gates)
        mix = token_matmul(y, w_out[l].astype(BF16), name="out_proj")
        xs = residual_ln(xs, mix, mod, 2, group_of, ln1_g[l], ln1_b[l], alpha)

        pq = token_matmul(xs, peer_w_q[l].astype(BF16), prologue="modulate", mod=mod, mod_idx=(3, 4),
                          group_of=group_of, out_dtype=BF16, name="peer_query")
        keys = peer_sub_keys[l].reshape(2 * PEER_HEADS, PEER_KEYS, PEER_QDIM // 2).astype(BF16)
        s0, s1, e0, e1, tau = peer_topk(pq, keys)
        ht = modulate_transposed(xs, mod, (3, 4), group_of)
        n_exp = peer_u.shape[1]
        vt = jnp.transpose(peer_v[l].astype(BF16).reshape(n_exp // PEER_EB, PEER_EB, d), (0, 2, 1))
        ffn_t = peer_dense(ht, peer_u[l].astype(BF16), vt, s0, s1, e0, e1, tau)
        ffn = jnp.transpose(ffn_t, (0, 2, 1)).reshape(xs.shape[0], d)
        xs = residual_ln(xs, ffn, mod, 5, group_of, ln2_g[l], ln2_b[l], alpha)

    return xs[:t_lat].reshape(bsz, seq, d)
```

```python
import functools
import math

import jax
import jax.numpy as jnp
import numpy as np
from jax import lax
from jax.experimental import pallas as pl
from jax.experimental.pallas import tpu as pltpu

F32 = jnp.float32
BF16 = jnp.bfloat16

GRID_W = 64
ROPE_THETA = 10000.0
LN_EPS = 1e-5
RMS_EPS = 1e-6
NEG_INF = -1e30

A_HEADS, A_NOPE, A_ROPE, A_V, A_Q_LORA, A_KV_LORA = 4, 128, 64, 128, 384, 128
B_HEADS, B_KV_HEADS, B_DIM, B_WINDOW = 8, 2, 64, 128
C_HEADS, C_DIM, NA_ROWS, NA_COLS = 4, 128, 8, 16
D_HEADS, D_KV_HEADS, D_DIM = 4, 2, 128
N_BRANCH, BRANCH_W = 4, 512
PEER_HEADS, PEER_KEYS, PEER_QDIM, PEER_TOPK = 8, 128, 256, 16

A_SCALE = (A_NOPE + A_ROPE) ** -0.5
B_SCALE = B_DIM ** -0.5
C_SCALE = C_DIM ** -0.5
D_SCALE = D_DIM ** -0.5

IN_SIZES = (A_Q_LORA, A_KV_LORA, A_ROPE,
            B_HEADS * B_DIM, B_KV_HEADS * B_DIM, B_KV_HEADS * B_DIM,
            C_HEADS * C_DIM, C_HEADS * C_DIM, C_HEADS * C_DIM,
            D_HEADS * D_DIM, D_KV_HEADS * D_DIM, D_KV_HEADS * D_DIM)
LANE = 128
BF16_ROWS = 16
IN_PAD_WIDTH = 4096

VMEM_LIMIT = 56 * 1024 * 1024

PEER_CHUNK = 512
PEER_EB = 512
PEER_TOPK_TILE = 256

B_TQ, B_BACK, B_WK = 512, 128, 768
C_TQ, C_BACK, C_WK = 4 * GRID_W, 4 * GRID_W, 12 * GRID_W


def _cparams(sem):
    return pltpu.CompilerParams(dimension_semantics=sem, vmem_limit_bytes=VMEM_LIMIT)


def _pick(n, cands):
    for c in cands:
        if n % c == 0:
            return c
    raise ValueError(f"no tile in {cands} divides {n}")


def _dot(a, b):
    return jnp.dot(a, b, preferred_element_type=F32)


def _dot_nt(a, b):
    return lax.dot_general(a, b, (((1,), (1,)), ((), ())), preferred_element_type=F32)


def _ada_kernel(c_ref, w_ref, b_ref, o_ref):
    c = c_ref[...]
    s = c * (1.0 / (1.0 + jnp.exp(-c)))
    w = w_ref[...]
    s_hi = s.astype(BF16)
    s_lo = (s - s_hi.astype(F32)).astype(BF16)
    w_hi = w.astype(BF16)
    w_lo = (w - w_hi.astype(F32)).astype(BF16)
    acc = _dot(s_hi, w_hi) + _dot(s_lo, w_hi) + _dot(s_hi, w_lo)
    o_ref[...] = acc + b_ref[...]


def ada_modulation(cc, w, b):
    r, k = cc.shape
    n = w.shape[1]
    tn = _pick(n, (512, 256, 128))
    return pl.pallas_call(
        _ada_kernel,
        grid=(n // tn,),
        in_specs=[pl.BlockSpec((r, k), lambda j: (0, 0)),
                  pl.BlockSpec((k, tn), lambda j: (0, j)),
                  pl.BlockSpec((1, tn), lambda j: (0, j))],
        out_specs=pl.BlockSpec((r, tn), lambda j: (0, j)),
        out_shape=jax.ShapeDtypeStruct((r, n), F32),
        compiler_params=_cparams(("arbitrary",)),
        name="ada_modulation",
    )(cc, w, b.reshape(1, n))


def _mm_kernel(*refs, prologue, has_bias, act):
    it = iter(refs)
    x_ref = next(it)
    if prologue == "modulate":
        sh_ref, sc_ref = next(it), next(it)
    elif prologue == "rmsnorm":
        g_ref = next(it)
    w_ref = next(it)
    b_ref = next(it) if has_bias else None
    o_ref = next(it)
    a_scr = next(it)

    @pl.when(pl.program_id(1) == 0)
    def _():
        x = x_ref[...].astype(F32)
        if prologue == "modulate":
            a = x * (1.0 + sc_ref[0]) + sh_ref[0]
        elif prologue == "rmsnorm":
            ms = jnp.mean(x * x, axis=-1, keepdims=True)
            a = (x * lax.rsqrt(ms + RMS_EPS)) * g_ref[...]
        else:
            a = x
        a_scr[...] = a.astype(BF16)

    acc = _dot(a_scr[...], w_ref[...])
    if has_bias:
        acc = acc + b_ref[...]
    if act == "sigmoid":
        acc = 1.0 / (1.0 + jnp.exp(-acc))
    o_ref[...] = acc.astype(o_ref.dtype)


def token_matmul(x, w, *, rows=None, x_cols=None, prologue=None, mod=None, mod_idx=None, group_of=None,
                 norm_g=None, bias=None, act=None, out_dtype=F32, name="token_matmul"):
    rows = x.shape[0] if rows is None else rows
    kdim, n = w.shape
    kw, kblk = (x.shape[1], 0) if x_cols is None else x_cols
    assert kw == kdim
    tm = _pick(rows, (512, 256))
    tn = _pick(n, (1024, 512, 256, 128))
    in_specs = [pl.BlockSpec((tm, kdim), lambda i, j: (i, kblk))]
    args = [x]
    if prologue == "modulate":
        tm_group = group_of(tm)
        for slot in mod_idx:
            in_specs.append(pl.BlockSpec((1, 1, kdim), lambda i, j, s=slot: (tm_group(i) * 6 + s, 0, 0)))
            args.append(mod)
    elif prologue == "rmsnorm":
        in_specs.append(pl.BlockSpec((1, kdim), lambda i, j: (0, 0)))
        args.append(norm_g.reshape(1, kdim))
    in_specs.append(pl.BlockSpec((kdim, tn), lambda i, j: (0, j)))
    args.append(w)
    if bias is not None:
        in_specs.append(pl.BlockSpec((1, tn), lambda i, j: (0, j)))
        args.append(bias.reshape(1, n))
    return pl.pallas_call(
        functools.partial(_mm_kernel, prologue=prologue, has_bias=bias is not None, act=act),
        grid=(rows // tm, n // tn),
        in_specs=in_specs,
        out_specs=pl.BlockSpec((tm, tn), lambda i, j: (i, j)),
        out_shape=jax.ShapeDtypeStruct((rows, n), out_dtype),
        scratch_shapes=[pltpu.VMEM((tm, kdim), BF16)],
        compiler_params=_cparams(("parallel", "arbitrary")),
        name=name,
    )(*args)


def _res_ln_kernel(x_ref, yt_ref, gate_ref, g_ref, b_ref, o_ref, *, alpha):
    z = alpha * x_ref[...] + gate_ref[0] * yt_ref[0].T
    mu = jnp.mean(z, axis=-1, keepdims=True)
    zc = z - mu
    var = jnp.mean(zc * zc, axis=-1, keepdims=True)
    o_ref[...] = (zc * lax.rsqrt(var + LN_EPS)) * g_ref[...] + b_ref[...]


def residual_ln(x, yt, mod, gate_slot, group_of, ln_g, ln_b, alpha):
    nch, d, tm = yt.shape
    rows = nch * tm
    tm_group = group_of(tm)
    return pl.pallas_call(
        functools.partial(_res_ln_kernel, alpha=alpha),
        grid=(nch,),
        in_specs=[pl.BlockSpec((tm, d), lambda i: (i, 0)),
                  pl.BlockSpec((1, d, tm), lambda i: (i, 0, 0)),
                  pl.BlockSpec((1, 1, d), lambda i: (tm_group(i) * 6 + gate_slot, 0, 0)),
                  pl.BlockSpec((1, d), lambda i: (0, 0)),
                  pl.BlockSpec((1, d), lambda i: (0, 0))],
        out_specs=pl.BlockSpec((tm, d), lambda i: (i, 0)),
        out_shape=jax.ShapeDtypeStruct((rows, d), F32),
        compiler_params=_cparams(("parallel",)),
        name="residual_ln",
    )(x, yt, mod, ln_g.reshape(1, d), ln_b.reshape(1, d))


def _dense_attn_kernel(*refs, n_parts, scale, has_sink):
    q_refs = refs[:n_parts]
    k_refs = refs[n_parts:2 * n_parts]
    v_ref = refs[2 * n_parts]
    sink_ref = refs[2 * n_parts + 1] if has_sink else None
    o_ref = refs[-1]
    s = _dot_nt(q_refs[0][0, 0], k_refs[0][0, 0])
    for qr, kr in zip(q_refs[1:], k_refs[1:]):
        s = s + _dot_nt(qr[0, 0], kr[0, 0])
    s = s * scale
    m = jnp.max(s, axis=-1, keepdims=True)
    if has_sink:
        sink = sink_ref[pl.program_id(1)]
        m = jnp.maximum(m, sink)
    p = jnp.exp(s - m)
    l = jnp.sum(p, axis=-1, keepdims=True)
    if has_sink:
        l = l + jnp.exp(sink - m)
    o = _dot(p.astype(BF16), v_ref[0, 0])
    o_ref[0, 0] = (o / l).astype(o_ref.dtype)


def dense_attention(q_parts, k_parts, v, scale, sink=None, name="dense_attention"):
    b, hq, nq, _ = q_parts[0].shape
    hkv, nk, dv = v.shape[1], v.shape[2], v.shape[3]
    g = hq // hkv
    tq = _pick(nq, (256, 128))
    in_specs, args = [], []
    for q in q_parts:
        in_specs.append(pl.BlockSpec((1, 1, tq, q.shape[3]), lambda bi, h, i: (bi, h, i, 0)))
        args.append(q)
    for k in k_parts:
        if k.shape[1] == 1:
            in_specs.append(pl.BlockSpec((1, 1, nk, k.shape[3]), lambda bi, h, i: (bi, 0, 0, 0)))
        else:
            in_specs.append(pl.BlockSpec((1, 1, nk, k.shape[3]), lambda bi, h, i: (bi, h // g, 0, 0)))
        args.append(k)
    in_specs.append(pl.BlockSpec((1, 1, nk, dv), lambda bi, h, i: (bi, h // g, 0, 0)))
    args.append(v)
    if sink is not None:
        in_specs.append(pl.BlockSpec(memory_space=pltpu.SMEM))
        args.append(sink.astype(F32))
    return pl.pallas_call(
        functools.partial(_dense_attn_kernel, n_parts=len(q_parts), scale=scale, has_sink=sink is not None),
        grid=(b, hq, nq // tq),
        in_specs=in_specs,
        out_specs=pl.BlockSpec((1, 1, tq, dv), lambda bi, h, i: (bi, h, i, 0)),
        out_shape=jax.ShapeDtypeStruct((b, hq, nq, dv), BF16),
        compiler_params=_cparams(("parallel", "parallel", "arbitrary")),
        name=name,
    )(*args)


def _window_attn_kernel(*refs, scale, has_sink, tq, back, wk, n_lat):
    q_ref, k_ref, v_ref, kc_ref, vc_ref, bias_ref = refs[:6]
    sink_ref = refs[6] if has_sink else None
    o_ref = refs[-1]
    i = pl.program_id(2)
    start = pl.multiple_of(jnp.clip(i * tq - back, 0, n_lat - wk), LANE)
    q = q_ref[0, 0]
    kw = k_ref[0, 0, pl.ds(start, wk), :]
    vw = v_ref[0, 0, pl.ds(start, wk), :]
    s1 = _dot_nt(q, kw) * scale + bias_ref[0, 0]
    s2 = _dot_nt(q, kc_ref[0, 0]) * scale
    m = jnp.maximum(jnp.max(s1, axis=-1, keepdims=True), jnp.max(s2, axis=-1, keepdims=True))
    if has_sink:
        sink = sink_ref[pl.program_id(1)]
        m = jnp.maximum(m, sink)
    p1 = jnp.exp(s1 - m)
    p2 = jnp.exp(s2 - m)
    l = jnp.sum(p1, axis=-1, keepdims=True) + jnp.sum(p2, axis=-1, keepdims=True)
    if has_sink:
        l = l + jnp.exp(sink - m)
    o = _dot(p1.astype(BF16), vw) + _dot(p2.astype(BF16), vc_ref[0, 0])
    o_ref[0, 0] = (o / l).astype(o_ref.dtype)


def window_attention(q, k, v, kc, vc, bias, scale, geom, sink=None, name="window_attention"):
    tq, back, wk = geom
    b, hq, n, d = q.shape
    hkv, nc = k.shape[1], kc.shape[2]
    g = hq // hkv
    nqb = n // tq
    assert n % tq == 0 and nqb >= 2 and n >= wk
    per_head_bias = bias.shape[1] != 1

    def bias_map(bi, h, i):
        cls = jnp.where(i == 0, 0, jnp.where(i == nqb - 1, 2, 1))
        return (cls, h if per_head_bias else 0, 0, 0)

    in_specs = [pl.BlockSpec((1, 1, tq, d), lambda bi, h, i: (bi, h, i, 0)),
                pl.BlockSpec((1, 1, n, d), lambda bi, h, i: (bi, h // g, 0, 0)),
                pl.BlockSpec((1, 1, n, d), lambda bi, h, i: (bi, h // g, 0, 0)),
                pl.BlockSpec((1, 1, nc, d), lambda bi, h, i: (bi, h // g, 0, 0)),
                pl.BlockSpec((1, 1, nc, d), lambda bi, h, i: (bi, h // g, 0, 0)),
                pl.BlockSpec((1, 1, tq, wk), bias_map)]
    args = [q, k, v, kc, vc, bias]
    if sink is not None:
        in_specs.append(pl.BlockSpec(memory_space=pltpu.SMEM))
        args.append(sink.astype(F32))
    return pl.pallas_call(
        functools.partial(_window_attn_kernel, scale=scale, has_sink=sink is not None, tq=tq, back=back, wk=wk,
                          n_lat=n),
        grid=(b, hq, nqb),
        in_specs=in_specs,
        out_specs=pl.BlockSpec((1, 1, tq, d), lambda bi, h, i: (bi, h, i, 0)),
        out_shape=jax.ShapeDtypeStruct((b, hq, n, d), BF16),
        compiler_params=_cparams(("parallel", "parallel", "arbitrary")),
        name=name,
    )(*args)


def _window_q_offsets(geom):
    tq, back, wk = geom
    return (0, back, wk - tq)


def band_bias_table():
    tq, _, wk = (B_TQ, B_BACK, B_WK)
    kpos = np.arange(wk)[None, :]
    tabs = []
    for off in _window_q_offsets((B_TQ, B_BACK, B_WK)):
        qpos = off + np.arange(tq)[:, None]
        tabs.append(np.where(np.abs(qpos - kpos) <= B_WINDOW, 0.0, NEG_INF))
    return jnp.asarray(np.stack(tabs)[:, None], F32)


def neighbourhood_bias_table(rpb, rows):
    tq, back, wk = (C_TQ, C_BACK, C_WK)
    assert rows >= wk // GRID_W and rows >= NA_ROWS
    nqb = rows * GRID_W // tq
    nq_rows, nk_rows = tq // GRID_W, wk // GRID_W
    cols = np.arange(GRID_W)
    col_start = np.clip(cols - NA_COLS // 2, 0, GRID_W - NA_COLS)
    col_ok = (cols[None, :] >= col_start[:, None]) & (cols[None, :] < col_start[:, None] + NA_COLS)
    col_off = np.clip(cols[None, :] - cols[:, None] + (NA_COLS - 1), 0, 2 * NA_COLS - 2)
    onehot_c = jnp.asarray(col_off[..., None] == np.arange(2 * NA_COLS - 1), F32)
    tabs = []
    for cls, off in enumerate(_window_q_offsets((C_TQ, C_BACK, C_WK))):
        blk = (0, 1, nqb - 1)[cls]
        w0 = int(np.clip(blk * tq - back, 0, rows * GRID_W - wk)) // GRID_W
        assert w0 * GRID_W + off == blk * tq and off % GRID_W == 0
        q_row = w0 + off // GRID_W + np.arange(nq_rows)
        k_row = w0 + np.arange(nk_rows)
        row_start = np.clip(q_row - NA_ROWS // 2, 0, rows - NA_ROWS)
        row_ok = (k_row[None, :] >= row_start[:, None]) & (k_row[None, :] < row_start[:, None] + NA_ROWS)
        row_off = np.clip(k_row[None, :] - q_row[:, None] + (NA_ROWS - 1), 0, 2 * NA_ROWS - 2)
        by_row = rpb[:, row_off, :].astype(F32)
        vals = jnp.einsum("hrsc,qkc->hrqsk", by_row, onehot_c, precision=lax.Precision.HIGHEST)
        ok = row_ok[:, None, :, None] & col_ok[None, :, None, :]
        tabs.append(jnp.where(jnp.asarray(ok)[None], vals, NEG_INF).reshape(rpb.shape[0], tq, wk))
    return jnp.stack(tabs)


def _merge_out_ln_kernel(*refs, alpha, n_branch):
    o_refs = refs[:n_branch]
    wbr_ref, g_ref, wout_ref, x_ref, gate_ref, lng_ref, lnb_ref, out_ref = refs[n_branch:]
    d = x_ref.shape[1]
    y = None
    for r, o_ref in enumerate(o_refs):
        t = g_ref[:, r * d:(r + 1) * d].astype(F32) * _dot(o_ref[...], wbr_ref[r])
        y = t if y is None else y + t
    mix = _dot(y.astype(BF16), wout_ref[...])
    z = alpha * x_ref[...] + gate_ref[0] * mix
    mu = jnp.mean(z, axis=-1, keepdims=True)
    zc = z - mu
    var = jnp.mean(zc * zc, axis=-1, keepdims=True)
    out_ref[...] = (zc * lax.rsqrt(var + LN_EPS)) * lng_ref[...] + lnb_ref[...]


def merge_out_ln(branches, w_branch, gates, w_out, x, mod, gate_slot, group_of, ln_g, ln_b, alpha):
    rows = gates.shape[0]
    nbr, bw, d = w_branch.shape
    tm = _pick(rows, (256,))
    tm_group = group_of(tm)
    resident = pl.Buffered(1)
    return pl.pallas_call(
        functools.partial(_merge_out_ln_kernel, alpha=alpha, n_branch=nbr),
        grid=(rows // tm,),
        in_specs=[pl.BlockSpec((tm, bw), lambda i: (i, 0)) for _ in range(nbr)] + [
                  pl.BlockSpec((nbr, bw, d), lambda i: (0, 0, 0), pipeline_mode=resident),
                  pl.BlockSpec((tm, nbr * d), lambda i: (i, 0)),
                  pl.BlockSpec((d, d), lambda i: (0, 0), pipeline_mode=resident),
                  pl.BlockSpec((tm, d), lambda i: (i, 0)),
                  pl.BlockSpec((1, 1, d), lambda i: (tm_group(i) * 6 + gate_slot, 0, 0)),
                  pl.BlockSpec((1, d), lambda i: (0, 0)),
                  pl.BlockSpec((1, d), lambda i: (0, 0))],
        out_specs=pl.BlockSpec((tm, d), lambda i: (i, 0)),
        out_shape=jax.ShapeDtypeStruct((rows, d), F32),
        compiler_params=_cparams(("parallel",)),
        name="merge_out_ln",
    )(*branches, w_branch, gates, w_out, x, mod, ln_g.reshape(1, d), ln_b.reshape(1, d))


def _peer_pairs():
    return [(a, b) for a in range(PEER_TOPK) for b in range(PEER_TOPK) if (a + 1) * (b + 1) <= PEER_TOPK]


def _peer_topk_kernel(q_ref, keys_ref, cnt_ref, e0_ref, rank_ref, e1_ref, s0_scr):
    tops = [[[None] * PEER_HEADS for _ in range(PEER_TOPK)] for _ in range(2)]
    for h in range(PEER_HEADS):
        for p in range(2):
            hp = 2 * h + p
            qs = q_ref[:, hp * PEER_KEYS:(hp + 1) * PEER_KEYS]
            s = _dot_nt(keys_ref[hp], qs)
            work = s
            rank = jnp.full(s.shape, float(PEER_TOPK), F32)
            for r in range(PEER_TOPK):
                m = jnp.max(work, axis=0, keepdims=True)
                tops[p][r][h] = m
                if p == 1:
                    hit = work == m
                    rank = jnp.where(hit, float(r), rank)
                    work = jnp.where(hit, -jnp.inf, work)
                elif r + 1 < PEER_TOPK:
                    work = jnp.where(work == m, -jnp.inf, work)
            if p == 0:
                s0_scr[h] = s
            else:
                rank_ref[h, 0] = rank.astype(BF16)
                e1_ref[h, 0] = jnp.exp(s - tops[1][0][h]).astype(BF16)
    top = [[jnp.concatenate(tops[p][r], axis=0) for r in range(PEER_TOPK)] for p in range(2)]
    cands = [top[0][a] + top[1][b] for a, b in _peer_pairs()]
    work = list(cands)
    for r in range(PEER_TOPK):
        m = functools.reduce(jnp.maximum, work)
        if r + 1 < PEER_TOPK:
            work = [jnp.where(w == m, -jnp.inf, w) for w in work]
    tau = m
    cmax = cands[0]
    z = functools.reduce(jnp.add, [jnp.where(c >= tau, jnp.exp(c - cmax), 0.0) for c in cands])
    rz = 1.0 / z
    for h in range(PEER_HEADS):
        s0 = s0_scr[h]
        tau_h = tau[h:h + 1]
        cnt = None
        for b in range(PEER_TOPK):
            inc = jnp.where(s0 + top[1][b][h:h + 1] >= tau_h, 1.0, 0.0)
            cnt = inc if cnt is None else cnt + inc
        cnt_ref[h, 0] = cnt
        e0_ref[h, 0] = jnp.exp(s0 - top[0][0][h:h + 1]) * rz[h:h + 1]


def peer_topk(q, keys):
    rows = q.shape[0]
    tt = PEER_TOPK_TILE
    per_chunk = PEER_CHUNK // tt
    nch = rows // PEER_CHUNK
    big = pl.BlockSpec((PEER_HEADS, 1, PEER_KEYS, tt), lambda i: (0, i // per_chunk, 0, i % per_chunk))
    shape_f32 = jax.ShapeDtypeStruct((PEER_HEADS, nch, PEER_KEYS, PEER_CHUNK), F32)
    shape_bf16 = jax.ShapeDtypeStruct((PEER_HEADS, nch, PEER_KEYS, PEER_CHUNK), BF16)
    return pl.pallas_call(
        _peer_topk_kernel,
        grid=(rows // tt,),
        in_specs=[pl.BlockSpec((tt, q.shape[1]), lambda i: (i, 0)),
                  pl.BlockSpec(keys.shape, lambda i: (0, 0, 0))],
        out_specs=[big, big, big, big],
        out_shape=[shape_f32, shape_f32, shape_bf16, shape_bf16],
        scratch_shapes=[pltpu.VMEM((PEER_HEADS, PEER_KEYS, tt), F32)],
        compiler_params=_cparams(("parallel",)),
        name="peer_topk",
    )(q, keys)


def _modulate_t_kernel(x_ref, sh_ref, sc_ref, o_ref):
    h = x_ref[...] * (1.0 + sc_ref[0]) + sh_ref[0]
    o_ref[0] = h.T.astype(BF16)


def modulate_transposed(x, mod, mod_idx, group_of):
    rows, d = x.shape
    tt = PEER_CHUNK
    tt_group = group_of(tt)
    mod_specs = [pl.BlockSpec((1, 1, d), lambda i, s=slot: (tt_group(i) * 6 + s, 0, 0)) for slot in mod_idx]
    return pl.pallas_call(
        _modulate_t_kernel,
        grid=(rows // tt,),
        in_specs=[pl.BlockSpec((tt, d), lambda i: (i, 0))] + mod_specs,
        out_specs=pl.BlockSpec((1, d, tt), lambda i: (i, 0, 0)),
        out_shape=jax.ShapeDtypeStruct((rows // tt, d, tt), BF16),
        compiler_params=_cparams(("parallel",)),
        name="modulate_transposed",
    )(x, mod, mod)


def _peer_dense_kernel(ht_ref, u_ref, vt_ref, cnt_ref, e0_ref, rank_ref, e1_ref, o_ref, a_scr, w_scr, p_scr, *, n_e):
    e = pl.program_id(1)
    cur = e % 2
    prev = 1 - cur
    i_per_eb = PEER_EB // PEER_KEYS
    tc = w_scr.shape[-1]

    def gate_and_down():
        for ii in range(i_per_eb):
            sl = slice(ii * PEER_KEYS, (ii + 1) * PEER_KEYS)
            a = a_scr[prev, sl, :]
            act = 0.5 * a * (1.0 + lax.erf(a * (1.0 / math.sqrt(2.0))))
            p_scr[sl, :] = act.astype(BF16) * w_scr[prev, sl, :]
        o_ref[0] += _dot(vt_ref[0], p_scr[...])

    def up_and_weights():
        a_scr[cur] = _dot(u_ref[...], ht_ref[0])
        for ii in range(i_per_eb):
            i = e * i_per_eb + ii
            cnt = [jnp.broadcast_to(cnt_ref[h, 0, pl.ds(i, 1), :], (BF16_ROWS, tc)).astype(BF16)
                   for h in range(PEER_HEADS)]
            e0 = [jnp.broadcast_to(e0_ref[h, 0, pl.ds(i, 1), :], (BF16_ROWS, tc)).astype(BF16)
                  for h in range(PEER_HEADS)]
            for g in range(PEER_KEYS // BF16_ROWS):
                rows = slice(g * BF16_ROWS, (g + 1) * BF16_ROWS)
                w = None
                for h in range(PEER_HEADS):
                    e1 = e1_ref[h, 0, rows, :]
                    t = jnp.where(rank_ref[h, 0, rows, :] < cnt[h], e1, jnp.zeros_like(e1)) * e0[h]
                    w = t if w is None else w + t
                w_scr[cur, ii * PEER_KEYS + g * BF16_ROWS:ii * PEER_KEYS + (g + 1) * BF16_ROWS, :] = w

    @pl.when(e == 0)
    def _():
        o_ref[...] = jnp.zeros_like(o_ref)
        up_and_weights()

    @pl.when(jnp.logical_and(e > 0, e < n_e))
    def _():
        gate_and_down()
        up_and_weights()

    @pl.when(e == n_e)
    def _():
        gate_and_down()


def peer_dense(ht, u, vt, cnt, e0, rank, e1):
    nch, d, tc = ht.shape
    n_e = u.shape[0] // PEER_EB
    sel = pl.BlockSpec((PEER_HEADS, 1, PEER_KEYS, tc), lambda t, e: (0, t, 0, 0))
    return pl.pallas_call(
        functools.partial(_peer_dense_kernel, n_e=n_e),
        grid=(nch, n_e + 1),
        in_specs=[pl.BlockSpec((1, d, tc), lambda t, e: (t, 0, 0)),
                  pl.BlockSpec((PEER_EB, d), lambda t, e: (jnp.minimum(e, n_e - 1), 0)),
                  pl.BlockSpec((1, d, PEER_EB), lambda t, e: (jnp.maximum(e - 1, 0), 0, 0)),
                  sel, sel, sel, sel],
        out_specs=pl.BlockSpec((1, d, tc), lambda t, e: (t, 0, 0)),
        out_shape=jax.ShapeDtypeStruct((nch, d, tc), F32),
        scratch_shapes=[pltpu.VMEM((2, PEER_EB, tc), F32), pltpu.VMEM((2, PEER_EB, tc), BF16),
                        pltpu.VMEM((PEER_EB, tc), BF16)],
        compiler_params=_cparams(("parallel", "arbitrary")),
        name="peer_dense",
    )(ht, u, vt, cnt, e0, rank, e1)


def _axial_angles(n, rot_dim):
    t = jnp.arange(n)
    rows = (t // GRID_W).astype(F32)
    cols = (t % GRID_W).astype(F32)
    nf = rot_dim // 4
    inv = ROPE_THETA ** (-jnp.arange(nf, dtype=F32) / nf)
    ang = jnp.concatenate([rows[:, None] * inv, cols[:, None] * inv], axis=-1)
    return jnp.cos(ang), jnp.sin(ang)


def _rope(x, cos, sin):
    half = x.shape[-1] // 2
    x1, x2 = x[..., :half], x[..., half:]
    c, s = cos[:, None, :], sin[:, None, :]
    return jnp.concatenate([x1 * c - x2 * s, x1 * s + x2 * c], axis=-1)


def _rms(x, g):
    ms = jnp.mean(jnp.square(x), axis=-1, keepdims=True)
    return (x * lax.rsqrt(ms + RMS_EPS)) * g


def _bhnd(t):
    return jnp.transpose(t, (0, 2, 1, 3)).astype(BF16)


def _flat(o):
    b, h, n, d = o.shape
    return jnp.transpose(o, (0, 2, 1, 3)).reshape(b * n, h * d)


def _padded_in_proj(w_in):
    d = w_in.shape[0]
    cols, o = [], 0
    for idx, n in enumerate(IN_SIZES):
        cols.append(w_in[:, o:o + n])
        o += n
        if idx == 2:
            cols.append(jnp.zeros((d, LANE - A_ROPE), w_in.dtype))
    used = sum(IN_SIZES) + LANE - A_ROPE
    cols.append(jnp.zeros((d, IN_PAD_WIDTH - used), w_in.dtype))
    return jnp.concatenate(cols, axis=1).astype(BF16)


def _in_offsets():
    offs, o = [], 0
    for idx, n in enumerate(IN_SIZES):
        offs.append(o)
        o += n
        if idx == 2:
            o += LANE - A_ROPE
    return offs


def kernel(x, c, ctx, c_ctx, w_ada, b_ada, w_in, a_q_norm, a_kv_norm, a_w_uq, a_w_ukv, b_sink, c_rpb, d_q_norm, d_k_norm, w_branch, w_gate, b_gate, w_out, ln1_g, ln1_b, ln2_g, ln2_b, peer_w_q, peer_sub_keys, peer_u, peer_v):
    bsz, seq, d = x.shape
    nctx = ctx.shape[1]
    depth = w_in.shape[0]
    t_lat, t_ctx = bsz * seq, bsz * nctx
    t_all = t_lat + t_ctx
    rows_grid = seq // GRID_W
    alpha = (2 * depth) ** 0.25

    rope_a = _axial_angles(seq, A_ROPE)
    rope_b = _axial_angles(seq, B_DIM)
    rope_d = _axial_angles(seq, D_DIM)
    band_bias = band_bias_table()
    offs = _in_offsets()

    n_groups = -(-(bsz + 1) // 8) * 8
    cc = jnp.zeros((n_groups, d), F32).at[:bsz].set(c).at[bsz].set(c_ctx)

    def group_of(tm):
        lat_blocks, per_batch = t_lat // tm, seq // tm
        assert t_lat % tm == 0 and seq % tm == 0
        return lambda i: jnp.where(i < lat_blocks, i // per_batch, bsz)

    xs = jnp.concatenate([x.reshape(t_lat, d), ctx.reshape(t_ctx, d)], axis=0)

    for l in range(depth):
        need_ctx = l < depth - 1
        rows_out = t_all if need_ctx else t_lat
        mod = ada_modulation(cc, w_ada[l], b_ada[l]).reshape(n_groups * 6, 1, d)

        u = token_matmul(xs, _padded_in_proj(w_in[l]), prologue="modulate", mod=mod, mod_idx=(0, 1),
                         group_of=group_of, out_dtype=BF16, name="in_proj")
        w_gate_l = jnp.transpose(w_gate[l], (1, 0, 2)).reshape(d, N_BRANCH * d).astype(BF16)
        gates = token_matmul(xs, w_gate_l, rows=rows_out, prologue="modulate", mod=mod, mod_idx=(0, 1),
                             group_of=group_of, bias=b_gate[l].reshape(-1), act="sigmoid", out_dtype=BF16,
                             name="branch_gates")

        qa = token_matmul(u, a_w_uq[l].astype(BF16), x_cols=(A_Q_LORA, offs[0] // A_Q_LORA), prologue="rmsnorm",
                          norm_g=a_q_norm[l], name="mla_q_up")
        kva = token_matmul(u, a_w_ukv[l].astype(BF16), x_cols=(A_KV_LORA, offs[1] // A_KV_LORA), prologue="rmsnorm",
                           norm_g=a_kv_norm[l], name="mla_kv_up")
        qa = qa.reshape(t_all, A_HEADS, A_NOPE + A_ROPE)
        kva = kva.reshape(t_all, A_HEADS, A_NOPE + A_V)
        kr = u[:, offs[2]:offs[2] + A_ROPE].astype(F32)

        def lat(t):
            return t[:t_lat].reshape((bsz, seq) + t.shape[1:])

        def cx(t):
            return t[t_lat:].reshape((bsz, nctx) + t.shape[1:])

        qa_nope, qa_rope = qa[..., :A_NOPE], qa[..., A_NOPE:]
        ka_nope, va = kva[..., :A_NOPE], kva[..., A_NOPE:]
        qa_rope_l = _rope(lat(qa_rope), *rope_a)
        kr_l = _rope(lat(kr)[:, :, None, :], *rope_a)
        kr_c = cx(kr)[:, :, None, :]
        cat = lambda a, b_: jnp.concatenate([a, b_], axis=2)
        ka_nope_all = cat(_bhnd(lat(ka_nope)), _bhnd(cx(ka_nope)))
        kr_all = cat(_bhnd(kr_l), _bhnd(kr_c))
        va_all = cat(_bhnd(lat(va)), _bhnd(cx(va)))
        oa = dense_attention([_bhnd(lat(qa_nope)), _bhnd(qa_rope_l)], [ka_nope_all, kr_all], va_all, A_SCALE,
                             name="mla_attention")

        def cols(idx, heads):
            t = u[:, offs[idx]:offs[idx] + IN_SIZES[idx]].astype(F32)
            return t.reshape(t_all, heads, IN_SIZES[idx] // heads)

        bq, bk, bv = cols(3, B_HEADS), cols(4, B_KV_HEADS), cols(5, B_KV_HEADS)
        ob = window_attention(_bhnd(_rope(lat(bq), *rope_b)), _bhnd(_rope(lat(bk), *rope_b)), _bhnd(lat(bv)),
                              _bhnd(cx(bk)), _bhnd(cx(bv)), band_bias, B_SCALE, (B_TQ, B_BACK, B_WK),
                              sink=b_sink[l], name="swa_attention")

        cq, ck, cv = cols(6, C_HEADS), cols(7, C_HEADS), cols(8, C_HEADS)
        oc = window_attention(_bhnd(lat(cq)), _bhnd(lat(ck)), _bhnd(lat(cv)), _bhnd(cx(ck)), _bhnd(cx(cv)),
                              neighbourhood_bias_table(c_rpb[l], rows_grid), C_SCALE, (C_TQ, C_BACK, C_WK),
                              name="na_attention")

        dq, dk, dv = cols(9, D_HEADS), cols(10, D_KV_HEADS), cols(11, D_KV_HEADS)
        dq_n, dk_n = _rms(dq, d_q_norm[l]), _rms(dk, d_k_norm[l])
        kd_all = cat(_bhnd(_rope(lat(dk_n), *rope_d)), _bhnd(cx(dk_n)))
        vd_all = cat(_bhnd(lat(dv)), _bhnd(cx(dv)))
        od = dense_attention([_bhnd(_rope(lat(dq_n), *rope_d))], [kd_all], vd_all, D_SCALE, name="gqa_attention")

        branches = [_flat(oa), _flat(ob), _flat(oc), _flat(od)]
        if need_ctx:
            oa_c = dense_attention([_bhnd(cx(qa_nope)), _bhnd(cx(qa_rope))], [_bhnd(cx(ka_nope)), _bhnd(kr_c)],
                                   _bhnd(cx(va)), A_SCALE, name="mla_attention_ctx")
            ob_c = dense_attention([_bhnd(cx(bq))], [_bhnd(cx(bk))], _bhnd(cx(bv)), B_SCALE, sink=b_sink[l],
                                   name="swa_attention_ctx")
            oc_c = dense_attention([_bhnd(cx(cq))], [_bhnd(cx(ck))], _bhnd(cx(cv)), C_SCALE,
                                   name="na_attention_ctx")
            od_c = dense_attention([_bhnd(cx(dq_n))], [_bhnd(cx(dk_n))], _bhnd(cx(dv)), D_SCALE,
                                   name="gqa_attention_ctx")
            branches = [jnp.concatenate([bl, bc], axis=0)
                        for bl, bc in zip(branches, [_flat(oa_c), _flat(ob_c), _flat(oc_c), _flat(od_c)])]

        xs = merge_out_ln(branches, w_branch[l].astype(BF16), gates, w_out[l].astype(BF16), xs, mod, 2, group_of,
                          ln1_g[l], ln1_b[l], alpha)

        pq = token_matmul(xs, peer_w_q[l].astype(BF16), prologue="modulate", mod=mod, mod_idx=(3, 4),
                          group_of=group_of, out_dtype=BF16, name="peer_query")
        keys = peer_sub_keys[l].reshape(2 * PEER_HEADS, PEER_KEYS, PEER_QDIM // 2).astype(BF16)
        cnt, e0, rank, e1 = peer_topk(pq, keys)
        ht = modulate_transposed(xs, mod, (3, 4), group_of)
        n_exp = peer_u.shape[1]
        vt = jnp.transpose(peer_v[l].astype(BF16).reshape(n_exp // PEER_EB, PEER_EB, d), (0, 2, 1))
        ffn_t = peer_dense(ht, peer_u[l].astype(BF16), vt, cnt, e0, rank, e1)
        xs = residual_ln(xs, ffn_t, mod, 5, group_of, ln2_g[l], ln2_b[l], alpha)

    return xs[:t_lat].reshape(bsz, seq, d)
```

```python
import functools
import math

import jax
import jax.numpy as jnp
import numpy as np
from jax import lax
from jax.experimental import pallas as pl
from jax.experimental.pallas import tpu as pltpu

F32 = jnp.float32
BF16 = jnp.bfloat16

GRID_W = 64
ROPE_THETA = 10000.0
LN_EPS = 1e-5
RMS_EPS = 1e-6
NEG_INF = -1e30

A_HEADS, A_NOPE, A_ROPE, A_V, A_Q_LORA, A_KV_LORA = 4, 128, 64, 128, 384, 128
B_HEADS, B_KV_HEADS, B_DIM, B_WINDOW = 8, 2, 64, 128
C_HEADS, C_DIM, NA_ROWS, NA_COLS = 4, 128, 8, 16
D_HEADS, D_KV_HEADS, D_DIM = 4, 2, 128
N_BRANCH, BRANCH_W = 4, 512
PEER_HEADS, PEER_KEYS, PEER_QDIM, PEER_TOPK = 8, 128, 256, 16

A_SCALE = (A_NOPE + A_ROPE) ** -0.5
B_SCALE = B_DIM ** -0.5
C_SCALE = C_DIM ** -0.5
D_SCALE = D_DIM ** -0.5

IN_SIZES = (A_Q_LORA, A_KV_LORA, A_ROPE,
            B_HEADS * B_DIM, B_KV_HEADS * B_DIM, B_KV_HEADS * B_DIM,
            C_HEADS * C_DIM, C_HEADS * C_DIM, C_HEADS * C_DIM,
            D_HEADS * D_DIM, D_KV_HEADS * D_DIM, D_KV_HEADS * D_DIM)
LANE = 128
BF16_ROWS = 16

VMEM_LIMIT = 56 * 1024 * 1024

PEER_CHUNK = 512
PEER_EB = 512
PEER_TOPK_TILE = 256

B_TQ, B_BACK, B_WK = 512, 128, 768
C_TQ, C_BACK, C_WK = 4 * GRID_W, 4 * GRID_W, 12 * GRID_W
DENSE_TQ = 256
ROPE_TM = 256

U_CQ, U_CKV, U_KR, U_BQ, U_BK, U_DQ, U_DK, U_BV, U_CQ_, U_CK, U_CV, U_DV, U_BLOCKS = 0, 3, 4, 5, 13, 14, 18, 20, 21, 25, 29, 33, 36
U_PREP0, U_PREP_N, U_PREP_R32, U_PREP_RMSQ = U_KR, 16, 10, 4
R_KR, R_BQ, R_BK, R_DQ, R_DK = 0, 1, 9, 10, 14


def _cparams(sem):
    return pltpu.CompilerParams(dimension_semantics=sem, vmem_limit_bytes=VMEM_LIMIT)


def _pick(n, cands):
    for c in cands:
        if n % c == 0:
            return c
    raise ValueError(f"no tile in {cands} divides {n}")


def _dot(a, b):
    return jnp.dot(a, b, preferred_element_type=F32)


def _dot_nt(a, b):
    return lax.dot_general(a, b, (((1,), (1,)), ((), ())), preferred_element_type=F32)


def _ada_kernel(c_ref, w_ref, b_ref, o_ref):
    c = c_ref[...]
    s = c * (1.0 / (1.0 + jnp.exp(-c)))
    w = w_ref[...]
    s_hi = s.astype(BF16)
    s_lo = (s - s_hi.astype(F32)).astype(BF16)
    w_hi = w.astype(BF16)
    w_lo = (w - w_hi.astype(F32)).astype(BF16)
    acc = _dot(s_hi, w_hi) + _dot(s_lo, w_hi) + _dot(s_hi, w_lo)
    o_ref[...] = acc + b_ref[...]


def ada_modulation(cc, w, b):
    r, k = cc.shape
    n = w.shape[1]
    tn = _pick(n, (512, 256, 128))
    return pl.pallas_call(
        _ada_kernel,
        grid=(n // tn,),
        in_specs=[pl.BlockSpec((r, k), lambda j: (0, 0)),
                  pl.BlockSpec((k, tn), lambda j: (0, j)),
                  pl.BlockSpec((1, tn), lambda j: (0, j))],
        out_specs=pl.BlockSpec((r, tn), lambda j: (0, j)),
        out_shape=jax.ShapeDtypeStruct((r, n), F32),
        compiler_params=_cparams(("arbitrary",)),
        name="ada_modulation",
    )(cc, w, b.reshape(1, n))


def _mm_kernel(*refs, prologue, has_bias, act):
    it = iter(refs)
    x_ref = next(it)
    if prologue == "modulate":
        sh_ref, sc_ref = next(it), next(it)
    elif prologue == "rmsnorm":
        g_ref = next(it)
    w_ref = next(it)
    b_ref = next(it) if has_bias else None
    o_ref = next(it)
    a_scr = next(it)

    @pl.when(pl.program_id(1) == 0)
    def _():
        x = x_ref[...].astype(F32)
        if prologue == "modulate":
            a = x * (1.0 + sc_ref[0]) + sh_ref[0]
        elif prologue == "rmsnorm":
            ms = jnp.mean(x * x, axis=-1, keepdims=True)
            a = (x * lax.rsqrt(ms + RMS_EPS)) * g_ref[...]
        else:
            a = x
        a_scr[...] = a.astype(BF16)

    acc = _dot(a_scr[...], w_ref[...])
    if has_bias:
        acc = acc + b_ref[...]
    if act == "sigmoid":
        acc = 1.0 / (1.0 + jnp.exp(-acc))
    o_ref[...] = acc.astype(o_ref.dtype)


def token_matmul(x, w, *, rows=None, x_cols=None, prologue=None, mod=None, mod_idx=None, group_of=None,
                 norm_g=None, bias=None, act=None, out_dtype=F32, name="token_matmul"):
    rows = x.shape[0] if rows is None else rows
    kdim, n = w.shape
    kw, kblk = (x.shape[1], 0) if x_cols is None else x_cols
    assert kw == kdim
    tm = _pick(rows, (512, 256))
    tn = _pick(n, (1024, 512, 256, 128))
    in_specs = [pl.BlockSpec((tm, kdim), lambda i, j: (i, kblk))]
    args = [x]
    if prologue == "modulate":
        tm_group = group_of(tm)
        for slot in mod_idx:
            in_specs.append(pl.BlockSpec((1, 1, kdim), lambda i, j, s=slot: (tm_group(i) * 6 + s, 0, 0)))
            args.append(mod)
    elif prologue == "rmsnorm":
        in_specs.append(pl.BlockSpec((1, kdim), lambda i, j: (0, 0)))
        args.append(norm_g.reshape(1, kdim))
    in_specs.append(pl.BlockSpec((kdim, tn), lambda i, j: (0, j)))
    args.append(w)
    if bias is not None:
        in_specs.append(pl.BlockSpec((1, tn), lambda i, j: (0, j)))
        args.append(bias.reshape(1, n))
    return pl.pallas_call(
        functools.partial(_mm_kernel, prologue=prologue, has_bias=bias is not None, act=act),
        grid=(rows // tm, n // tn),
        in_specs=in_specs,
        out_specs=pl.BlockSpec((tm, tn), lambda i, j: (i, j)),
        out_shape=jax.ShapeDtypeStruct((rows, n), out_dtype),
        scratch_shapes=[pltpu.VMEM((tm, kdim), BF16)],
        compiler_params=_cparams(("parallel", "arbitrary")),
        name=name,
    )(*args)


def _res_ln_kernel(x_ref, yt_ref, gate_ref, g_ref, b_ref, o_ref, *, alpha):
    z = alpha * x_ref[...] + gate_ref[0] * yt_ref[0].T
    mu = jnp.mean(z, axis=-1, keepdims=True)
    zc = z - mu
    var = jnp.mean(zc * zc, axis=-1, keepdims=True)
    o_ref[...] = (zc * lax.rsqrt(var + LN_EPS)) * g_ref[...] + b_ref[...]


def residual_ln(x, yt, mod, gate_slot, group_of, ln_g, ln_b, alpha):
    nch, d, tm = yt.shape
    rows = nch * tm
    tm_group = group_of(tm)
    return pl.pallas_call(
        functools.partial(_res_ln_kernel, alpha=alpha),
        grid=(nch,),
        in_specs=[pl.BlockSpec((tm, d), lambda i: (i, 0)),
                  pl.BlockSpec((1, d, tm), lambda i: (i, 0, 0)),
                  pl.BlockSpec((1, 1, d), lambda i: (tm_group(i) * 6 + gate_slot, 0, 0)),
                  pl.BlockSpec((1, d), lambda i: (0, 0)),
                  pl.BlockSpec((1, d), lambda i: (0, 0))],
        out_specs=pl.BlockSpec((tm, d), lambda i: (i, 0)),
        out_shape=jax.ShapeDtypeStruct((rows, d), F32),
        compiler_params=_cparams(("parallel",)),
        name="residual_ln",
    )(x, yt, mod, ln_g.reshape(1, d), ln_b.reshape(1, d))


def _qk_prep_kernel(x_ref, c32_ref, sa_ref, sb_ref, c64_ref, s64_ref, gq_ref, gk_ref, o_ref, *, n_r32, n_rms_q):
    j = pl.program_id(1)

    @pl.when(j < n_r32)
    def _():
        x = x_ref[...].astype(F32)
        y = x * c32_ref[...] + pltpu.roll(x, 32, 1) * sa_ref[...] + pltpu.roll(x, 96, 1) * sb_ref[...]
        o_ref[...] = y.astype(o_ref.dtype)

    @pl.when(j >= n_r32)
    def _():
        x = x_ref[...].astype(F32)
        g = jnp.where(j < n_r32 + n_rms_q, gq_ref[...], gk_ref[...])
        ms = jnp.mean(x * x, axis=-1, keepdims=True)
        y = (x * lax.rsqrt(ms + RMS_EPS)) * g
        y = y * c64_ref[...] + pltpu.roll(y, 64, 1) * s64_ref[...]
        o_ref[...] = y.astype(o_ref.dtype)


def qk_prep(x, col0, n_blocks, n_r32, n_rms_q, tables, gq, gk, lat_rows, seq, name):
    rows = x.shape[0]
    tm = ROPE_TM
    lat_blocks, per_seq = lat_rows // tm, seq // tm
    tab_spec = pl.BlockSpec((tm, LANE), lambda i, j: (jnp.where(i < lat_blocks, i % per_seq, per_seq), 0))
    vec_spec = pl.BlockSpec((1, LANE), lambda i, j: (0, 0))
    return pl.pallas_call(
        functools.partial(_qk_prep_kernel, n_r32=n_r32, n_rms_q=n_rms_q),
        grid=(rows // tm, n_blocks),
        in_specs=[pl.BlockSpec((tm, LANE), lambda i, j: (i, col0 + j))] + [tab_spec] * 5 + [vec_spec] * 2,
        out_specs=pl.BlockSpec((tm, LANE), lambda i, j: (i, j)),
        out_shape=jax.ShapeDtypeStruct((rows, n_blocks * LANE), BF16),
        compiler_params=_cparams(("parallel", "arbitrary")),
        name=name,
    )(x, *tables, gq.reshape(1, LANE), gk.reshape(1, LANE))


def _attn_kernel(*refs, n_parts, has_lat, window, has_bias, has_sink, scale, n_lat):
    it = iter(refs)
    q_refs = [next(it) for _ in range(n_parts)]
    if has_lat:
        kl_refs = [next(it) for _ in range(n_parts)]
        vl_ref = next(it)
    kc_refs = [next(it) for _ in range(n_parts)]
    vc_ref = next(it)
    bias_ref = next(it) if has_bias else None
    sink_ref = next(it) if has_sink else None
    refs = list(it)
    o_ref = refs[-1]

    qs = [q[...] for q in q_refs]
    s2 = functools.reduce(jnp.add, [_dot_nt(q, k[...]) for q, k in zip(qs, kc_refs)]) * scale
    m = jnp.max(s2, axis=-1, keepdims=True)
    if has_lat:
        if window is None:
            kls = [k[...] for k in kl_refs]
            vl = vl_ref[...]
        else:
            tq, back, wk = window
            start = pl.multiple_of(jnp.clip(pl.program_id(2) * tq - back, 0, n_lat - wk), LANE)
            kls = [k[pl.ds(start, wk), :] for k in kl_refs]
            vl = vl_ref[pl.ds(start, wk), :]
        s1 = functools.reduce(jnp.add, [_dot_nt(q, k) for q, k in zip(qs, kls)]) * scale
        if has_bias:
            s1 = s1 + bias_ref[0, 0]
        m = jnp.maximum(m, jnp.max(s1, axis=-1, keepdims=True))
    if has_sink:
        sink = sink_ref[pl.program_id(1)]
        m = jnp.maximum(m, sink)
    p2 = jnp.exp(s2 - m)
    l = jnp.sum(p2, axis=-1, keepdims=True)
    o = _dot(p2.astype(BF16), vc_ref[...])
    if has_lat:
        p1 = jnp.exp(s1 - m)
        l = l + jnp.sum(p1, axis=-1, keepdims=True)
        o = o + _dot(p1.astype(BF16), vl)
    if has_sink:
        l = l + jnp.exp(sink - m)
    o_ref[...] = (o / l).astype(o_ref.dtype)


def attention(q_srcs, k_srcs, v_src, *, heads, scale, bsz, seq, nctx, t_lat, mode, out_rows=None,
              out_prev=None, window=None, bias=None, sink=None, name="attention"):
    lat = mode == "lat"
    n_parts = len(q_srcs)
    tq = (window[0] if window is not None else DENSE_TQ) if lat else nctx
    nqb = (seq if lat else nctx) // tq
    ctx0 = t_lat // nctx

    def kcol(first, grp, h):
        return first + (h // grp if grp else 0)

    def q_row(bi, i):
        return bi * nqb + i if lat else t_lat // tq + bi

    in_specs, args = [], []
    for arr, first in q_srcs:
        in_specs.append(pl.BlockSpec((tq, LANE), lambda bi, h, i, f=first: (q_row(bi, i), f + h)))
        args.append(arr)
    if lat:
        for arr, first, grp in list(k_srcs) + [v_src]:
            in_specs.append(pl.BlockSpec((seq, LANE), lambda bi, h, i, f=first, g=grp: (bi, kcol(f, g, h))))
            args.append(arr)
    for arr, first, grp in list(k_srcs) + [v_src]:
        in_specs.append(pl.BlockSpec((nctx, LANE), lambda bi, h, i, f=first, g=grp: (ctx0 + bi, kcol(f, g, h))))
        args.append(arr)
    if bias is not None:
        per_head = bias.shape[1] != 1
        in_specs.append(pl.BlockSpec(
            (1, 1, tq, window[2]),
            lambda bi, h, i: (jnp.where(i == 0, 0, jnp.where(i == nqb - 1, 2, 1)), h if per_head else 0, 0, 0)))
        args.append(bias)
    if sink is not None:
        in_specs.append(pl.BlockSpec(memory_space=pltpu.SMEM))
        args.append(sink.astype(F32))
    aliases = {}
    if out_prev is not None:
        aliases = {len(args): 0}
        in_specs.append(pl.BlockSpec(memory_space=pl.ANY))
        args.append(out_prev)
        out_rows = out_prev.shape[0]
    if window is not None:
        assert seq % tq == 0 and nqb >= 2 and seq >= window[2]
    return pl.pallas_call(
        functools.partial(_attn_kernel, n_parts=n_parts, has_lat=lat, window=window, has_bias=bias is not None,
                          has_sink=sink is not None, scale=scale, n_lat=seq),
        grid=(bsz, heads, nqb),
        in_specs=in_specs,
        out_specs=pl.BlockSpec((tq, LANE), lambda bi, h, i: (q_row(bi, i), h)),
        out_shape=jax.ShapeDtypeStruct((out_rows, heads * LANE), BF16),
        input_output_aliases=aliases,
        compiler_params=_cparams(("parallel", "parallel", "arbitrary")),
        name=name,
    )(*args)


def _window_q_offsets(geom):
    tq, back, wk = geom
    return (0, back, wk - tq)


def band_bias_table():
    tq, _, wk = (B_TQ, B_BACK, B_WK)
    kpos = np.arange(wk)[None, :]
    tabs = []
    for off in _window_q_offsets((B_TQ, B_BACK, B_WK)):
        qpos = off + np.arange(tq)[:, None]
        tabs.append(np.where(np.abs(qpos - kpos) <= B_WINDOW, 0.0, NEG_INF))
    return jnp.asarray(np.stack(tabs)[:, None], F32)


def neighbourhood_bias_table(rpb, rows):
    tq, back, wk = (C_TQ, C_BACK, C_WK)
    assert rows >= wk // GRID_W and rows >= NA_ROWS
    nqb = rows * GRID_W // tq
    nq_rows, nk_rows = tq // GRID_W, wk // GRID_W
    cols = np.arange(GRID_W)
    col_start = np.clip(cols - NA_COLS // 2, 0, GRID_W - NA_COLS)
    col_ok = (cols[None, :] >= col_start[:, None]) & (cols[None, :] < col_start[:, None] + NA_COLS)
    col_off = np.clip(cols[None, :] - cols[:, None] + (NA_COLS - 1), 0, 2 * NA_COLS - 2)
    onehot_c = jnp.asarray(col_off[..., None] == np.arange(2 * NA_COLS - 1), F32)
    tabs = []
    for cls, off in enumerate(_window_q_offsets((C_TQ, C_BACK, C_WK))):
        blk = (0, 1, nqb - 1)[cls]
        w0 = int(np.clip(blk * tq - back, 0, rows * GRID_W - wk)) // GRID_W
        assert w0 * GRID_W + off == blk * tq and off % GRID_W == 0
        q_row = w0 + off // GRID_W + np.arange(nq_rows)
        k_row = w0 + np.arange(nk_rows)
        row_start = np.clip(q_row - NA_ROWS // 2, 0, rows - NA_ROWS)
        row_ok = (k_row[None, :] >= row_start[:, None]) & (k_row[None, :] < row_start[:, None] + NA_ROWS)
        row_off = np.clip(k_row[None, :] - q_row[:, None] + (NA_ROWS - 1), 0, 2 * NA_ROWS - 2)
        by_row = rpb[:, row_off, :].astype(F32)
        vals = jnp.einsum("hrsc,qkc->hrqsk", by_row, onehot_c, precision=lax.Precision.HIGHEST)
        ok = row_ok[:, None, :, None] & col_ok[None, :, None, :]
        tabs.append(jnp.where(jnp.asarray(ok)[None], vals, NEG_INF).reshape(rpb.shape[0], tq, wk))
    return jnp.stack(tabs)


def _merge_out_ln_kernel(*refs, alpha, n_branch):
    o_refs = refs[:n_branch]
    wbr_refs = refs[n_branch:2 * n_branch]
    g_ref, wout_ref, x_ref, gate_ref, lng_ref, lnb_ref, out_ref = refs[2 * n_branch:]
    d = x_ref.shape[1]
    y = None
    for r, (o_ref, wbr_ref) in enumerate(zip(o_refs, wbr_refs)):
        t = g_ref[:, r * d:(r + 1) * d].astype(F32) * _dot(o_ref[...], wbr_ref[...])
        y = t if y is None else y + t
    mix = _dot(y.astype(BF16), wout_ref[...])
    z = alpha * x_ref[...] + gate_ref[0] * mix
    mu = jnp.mean(z, axis=-1, keepdims=True)
    zc = z - mu
    var = jnp.mean(zc * zc, axis=-1, keepdims=True)
    out_ref[...] = (zc * lax.rsqrt(var + LN_EPS)) * lng_ref[...] + lnb_ref[...]


def merge_out_ln(branches, w_branch, gates, w_out, x, mod, gate_slot, group_of, ln_g, ln_b, alpha):
    rows = gates.shape[0]
    nbr = len(branches)
    d = w_out.shape[0]
    tm = _pick(rows, (256,))
    tm_group = group_of(tm)
    resident = pl.Buffered(1)
    return pl.pallas_call(
        functools.partial(_merge_out_ln_kernel, alpha=alpha, n_branch=nbr),
        grid=(rows // tm,),
        in_specs=[pl.BlockSpec((tm, o.shape[1]), lambda i: (i, 0)) for o in branches] + [
                  pl.BlockSpec(w.shape, lambda i: (0, 0), pipeline_mode=resident) for w in w_branch] + [
                  pl.BlockSpec((tm, nbr * d), lambda i: (i, 0)),
                  pl.BlockSpec((d, d), lambda i: (0, 0), pipeline_mode=resident),
                  pl.BlockSpec((tm, d), lambda i: (i, 0)),
                  pl.BlockSpec((1, 1, d), lambda i: (tm_group(i) * 6 + gate_slot, 0, 0)),
                  pl.BlockSpec((1, d), lambda i: (0, 0)),
                  pl.BlockSpec((1, d), lambda i: (0, 0))],
        out_specs=pl.BlockSpec((tm, d), lambda i: (i, 0)),
        out_shape=jax.ShapeDtypeStruct((rows, d), F32),
        compiler_params=_cparams(("parallel",)),
        name="merge_out_ln",
    )(*branches, *w_branch, gates, w_out, x, mod, ln_g.reshape(1, d), ln_b.reshape(1, d))


def _peer_pairs():
    return [(a, b) for a in range(PEER_TOPK) for b in range(PEER_TOPK) if (a + 1) * (b + 1) <= PEER_TOPK]


def _peer_topk_kernel(q_ref, keys_ref, cnt_ref, e0_ref, rank_ref, e1_ref, s0_scr):
    tops = [[[None] * PEER_HEADS for _ in range(PEER_TOPK)] for _ in range(2)]
    for h in range(PEER_HEADS):
        for p in range(2):
            hp = 2 * h + p
            qs = q_ref[:, hp * PEER_KEYS:(hp + 1) * PEER_KEYS]
            s = _dot_nt(keys_ref[hp], qs)
            work = s
            rank = jnp.full(s.shape, float(PEER_TOPK), F32)
            for r in range(PEER_TOPK):
                m = jnp.max(work, axis=0, keepdims=True)
                tops[p][r][h] = m
                if p == 1:
                    hit = work == m
                    rank = jnp.where(hit, float(r), rank)
                    work = jnp.where(hit, -jnp.inf, work)
                elif r + 1 < PEER_TOPK:
                    work = jnp.where(work == m, -jnp.inf, work)
            if p == 0:
                s0_scr[h] = s
            else:
                rank_ref[h, 0] = rank.astype(BF16)
                e1_ref[h, 0] = jnp.exp(s - tops[1][0][h]).astype(BF16)
    top = [[jnp.concatenate(tops[p][r], axis=0) for r in range(PEER_TOPK)] for p in range(2)]
    cands = [top[0][a] + top[1][b] for a, b in _peer_pairs()]
    work = list(cands)
    for r in range(PEER_TOPK):
        m = functools.reduce(jnp.maximum, work)
        if r + 1 < PEER_TOPK:
            work = [jnp.where(w == m, -jnp.inf, w) for w in work]
    tau = m
    cmax = cands[0]
    z = functools.reduce(jnp.add, [jnp.where(c >= tau, jnp.exp(c - cmax), 0.0) for c in cands])
    rz = 1.0 / z
    for h in range(PEER_HEADS):
        s0 = s0_scr[h]
        tau_h = tau[h:h + 1]
        cnt = None
        for b in range(PEER_TOPK):
            inc = jnp.where(s0 + top[1][b][h:h + 1] >= tau_h, 1.0, 0.0)
            cnt = inc if cnt is None else cnt + inc
        cnt_ref[h, 0] = cnt
        e0_ref[h, 0] = jnp.exp(s0 - top[0][0][h:h + 1]) * rz[h:h + 1]


def peer_topk(q, keys):
    rows = q.shape[0]
    tt = PEER_TOPK_TILE
    per_chunk = PEER_CHUNK // tt
    nch = rows // PEER_CHUNK
    big = pl.BlockSpec((PEER_HEADS, 1, PEER_KEYS, tt), lambda i: (0, i // per_chunk, 0, i % per_chunk))
    shape_f32 = jax.ShapeDtypeStruct((PEER_HEADS, nch, PEER_KEYS, PEER_CHUNK), F32)
    shape_bf16 = jax.ShapeDtypeStruct((PEER_HEADS, nch, PEER_KEYS, PEER_CHUNK), BF16)
    return pl.pallas_call(
        _peer_topk_kernel,
        grid=(rows // tt,),
        in_specs=[pl.BlockSpec((tt, q.shape[1]), lambda i: (i, 0)),
                  pl.BlockSpec(keys.shape, lambda i: (0, 0, 0))],
        out_specs=[big, big, big, big],
        out_shape=[shape_f32, shape_f32, shape_bf16, shape_bf16],
        scratch_shapes=[pltpu.VMEM((PEER_HEADS, PEER_KEYS, tt), F32)],
        compiler_params=_cparams(("parallel",)),
        name="peer_topk",
    )(q, keys)


def _modulate_t_kernel(x_ref, sh_ref, sc_ref, o_ref):
    h = x_ref[...] * (1.0 + sc_ref[0]) + sh_ref[0]
    o_ref[0] = h.T.astype(BF16)


def modulate_transposed(x, mod, mod_idx, group_of):
    rows, d = x.shape
    tt = PEER_CHUNK
    tt_group = group_of(tt)
    mod_specs = [pl.BlockSpec((1, 1, d), lambda i, s=slot: (tt_group(i) * 6 + s, 0, 0)) for slot in mod_idx]
    return pl.pallas_call(
        _modulate_t_kernel,
        grid=(rows // tt,),
        in_specs=[pl.BlockSpec((tt, d), lambda i: (i, 0))] + mod_specs,
        out_specs=pl.BlockSpec((1, d, tt), lambda i: (i, 0, 0)),
        out_shape=jax.ShapeDtypeStruct((rows // tt, d, tt), BF16),
        compiler_params=_cparams(("parallel",)),
        name="modulate_transposed",
    )(x, mod, mod)


def _peer_dense_kernel(ht_ref, u_ref, vt_ref, cnt_ref, e0_ref, rank_ref, e1_ref, o_ref, a_scr, w_scr, p_scr, *, n_e):
    e = pl.program_id(1)
    cur = e % 2
    prev = 1 - cur
    i_per_eb = PEER_EB // PEER_KEYS
    tc = w_scr.shape[-1]

    def gate_and_down():
        for ii in range(i_per_eb):
            sl = slice(ii * PEER_KEYS, (ii + 1) * PEER_KEYS)
            a = a_scr[prev, sl, :]
            act = 0.5 * a * (1.0 + lax.erf(a * (1.0 / math.sqrt(2.0))))
            p_scr[sl, :] = act.astype(BF16) * w_scr[prev, sl, :]
        o_ref[0] += _dot(vt_ref[0], p_scr[...])

    def up_and_weights():
        a_scr[cur] = _dot(u_ref[...], ht_ref[0])
        for ii in range(i_per_eb):
            i = e * i_per_eb + ii
            cnt = [jnp.broadcast_to(cnt_ref[h, 0, pl.ds(i, 1), :], (BF16_ROWS, tc)).astype(BF16)
                   for h in range(PEER_HEADS)]
            e0 = [jnp.broadcast_to(e0_ref[h, 0, pl.ds(i, 1), :], (BF16_ROWS, tc)).astype(BF16)
                  for h in range(PEER_HEADS)]
            for g in range(PEER_KEYS // BF16_ROWS):
                rows = slice(g * BF16_ROWS, (g + 1) * BF16_ROWS)
                w = None
                for h in range(PEER_HEADS):
                    e1 = e1_ref[h, 0, rows, :]
                    t = jnp.where(rank_ref[h, 0, rows, :] < cnt[h], e1, jnp.zeros_like(e1)) * e0[h]
                    w = t if w is None else w + t
                w_scr[cur, ii * PEER_KEYS + g * BF16_ROWS:ii * PEER_KEYS + (g + 1) * BF16_ROWS, :] = w

    @pl.when(e == 0)
    def _():
        o_ref[...] = jnp.zeros_like(o_ref)
        up_and_weights()

    @pl.when(jnp.logical_and(e > 0, e < n_e))
    def _():
        gate_and_down()
        up_and_weights()

    @pl.when(e == n_e)
    def _():
        gate_and_down()


def peer_dense(ht, u, vt, cnt, e0, rank, e1):
    nch, d, tc = ht.shape
    n_e = u.shape[0] // PEER_EB
    sel = pl.BlockSpec((PEER_HEADS, 1, PEER_KEYS, tc), lambda t, e: (0, t, 0, 0))
    return pl.pallas_call(
        functools.partial(_peer_dense_kernel, n_e=n_e),
        grid=(nch, n_e + 1),
        in_specs=[pl.BlockSpec((1, d, tc), lambda t, e: (t, 0, 0)),
                  pl.BlockSpec((PEER_EB, d), lambda t, e: (jnp.minimum(e, n_e - 1), 0)),
                  pl.BlockSpec((1, d, PEER_EB), lambda t, e: (jnp.maximum(e - 1, 0), 0, 0)),
                  sel, sel, sel, sel],
        out_specs=pl.BlockSpec((1, d, tc), lambda t, e: (t, 0, 0)),
        out_shape=jax.ShapeDtypeStruct((nch, d, tc), F32),
        scratch_shapes=[pltpu.VMEM((2, PEER_EB, tc), F32), pltpu.VMEM((2, PEER_EB, tc), BF16),
                        pltpu.VMEM((PEER_EB, tc), BF16)],
        compiler_params=_cparams(("parallel", "arbitrary")),
        name="peer_dense",
    )(ht, u, vt, cnt, e0, rank, e1)


def _axial_angles(n, rot_dim):
    t = jnp.arange(n)
    rows = (t // GRID_W).astype(F32)
    cols = (t % GRID_W).astype(F32)
    nf = rot_dim // 4
    inv = ROPE_THETA ** (-jnp.arange(nf, dtype=F32) / nf)
    ang = jnp.concatenate([rows[:, None] * inv, cols[:, None] * inv], axis=-1)
    return jnp.cos(ang), jnp.sin(ang)


def _rope_tables(seq):
    assert A_ROPE == B_DIM == LANE // 2 and D_DIM == LANE

    def pad(t, fill):
        return jnp.concatenate([t, jnp.full((ROPE_TM, LANE), fill, F32)], axis=0)

    cos, sin = _axial_angles(seq, B_DIM)
    z = jnp.zeros_like(sin)
    c32 = jnp.concatenate([cos, cos, cos, cos], axis=1)
    sa = jnp.concatenate([z, sin, z, sin], axis=1)
    sb = jnp.concatenate([-sin, z, -sin, z], axis=1)
    cos, sin = _axial_angles(seq, D_DIM)
    c64 = jnp.concatenate([cos, cos], axis=1)
    s64 = jnp.concatenate([-sin, sin], axis=1)
    return [pad(c32, 1.0), pad(sa, 0.0), pad(sb, 0.0), pad(c64, 1.0), pad(s64, 0.0)]


def _half_block(t, first_half, axis):
    z = jnp.zeros_like(t)
    return jnp.concatenate([t, z] if first_half else [z, t], axis=axis)


def _in_proj_weight(w_in):
    d = w_in.shape[0]
    o = np.cumsum((0,) + IN_SIZES)
    part = [w_in[:, o[i]:o[i + 1]] for i in range(len(IN_SIZES))]
    per_kv = B_HEADS // B_KV_HEADS
    bq = [_half_block(part[3][:, h * B_DIM:(h + 1) * B_DIM], h // per_kv == 0, 1) for h in range(B_HEADS)]
    cols = [part[0], part[1], _half_block(part[2], True, 1)] + bq + [
        part[4], part[9], part[10], part[5], part[6], part[7], part[8], part[11], jnp.zeros((d, LANE), w_in.dtype)]
    out = jnp.concatenate(cols, axis=1).astype(BF16)
    assert out.shape[1] == U_BLOCKS * LANE
    return out


def _mla_up_weights(w_uq, w_ukv):
    dq, dkv = A_NOPE + A_ROPE, A_NOPE + A_V
    q_nope = [w_uq[:, h * dq:h * dq + A_NOPE] for h in range(A_HEADS)]
    q_rope = [_half_block(w_uq[:, h * dq + A_NOPE:(h + 1) * dq], True, 1) for h in range(A_HEADS)]
    k_nope = [w_ukv[:, h * dkv:h * dkv + A_NOPE] for h in range(A_HEADS)]
    v = [w_ukv[:, h * dkv + A_NOPE:(h + 1) * dkv] for h in range(A_HEADS)]
    return (jnp.concatenate(q_nope + q_rope, axis=1).astype(BF16), jnp.concatenate(k_nope + v, axis=1).astype(BF16))


def _branch_weights(w_branch):
    per_kv = B_HEADS // B_KV_HEADS
    wb = jnp.concatenate([_half_block(w_branch[1][h * B_DIM:(h + 1) * B_DIM], h // per_kv == 0, 0)
                          for h in range(B_HEADS)], axis=0)
    return [w_branch[0].astype(BF16), wb.astype(BF16), w_branch[2].astype(BF16), w_branch[3].astype(BF16)]


def kernel(x, c, ctx, c_ctx, w_ada, b_ada, w_in, a_q_norm, a_kv_norm, a_w_uq, a_w_ukv, b_sink, c_rpb, d_q_norm, d_k_norm, w_branch, w_gate, b_gate, w_out, ln1_g, ln1_b, ln2_g, ln2_b, peer_w_q, peer_sub_keys, peer_u, peer_v):
    bsz, seq, d = x.shape
    nctx = ctx.shape[1]
    depth = w_in.shape[0]
    t_lat, t_ctx = bsz * seq, bsz * nctx
    t_all = t_lat + t_ctx
    rows_grid = seq // GRID_W
    alpha = (2 * depth) ** 0.25

    tables = _rope_tables(seq)
    band_bias = band_bias_table()
    dims = dict(bsz=bsz, seq=seq, nctx=nctx, t_lat=t_lat)

    n_groups = -(-(bsz + 1) // 8) * 8
    cc = jnp.zeros((n_groups, d), F32).at[:bsz].set(c).at[bsz].set(c_ctx)

    def group_of(tm):
        lat_blocks, per_batch = t_lat // tm, seq // tm
        assert t_lat % tm == 0 and seq % tm == 0
        return lambda i: jnp.where(i < lat_blocks, i // per_batch, bsz)

    xs = jnp.concatenate([x.reshape(t_lat, d), ctx.reshape(t_ctx, d)], axis=0)

    for l in range(depth):
        need_ctx = l < depth - 1
        rows_out = t_all if need_ctx else t_lat
        mod = ada_modulation(cc, w_ada[l], b_ada[l]).reshape(n_groups * 6, 1, d)

        u = token_matmul(xs, _in_proj_weight(w_in[l]), prologue="modulate", mod=mod, mod_idx=(0, 1),
                         group_of=group_of, out_dtype=BF16, name="in_proj")
        w_gate_l = jnp.transpose(w_gate[l], (1, 0, 2)).reshape(d, N_BRANCH * d).astype(BF16)
        gates = token_matmul(xs, w_gate_l, rows=rows_out, prologue="modulate", mod=mod, mod_idx=(0, 1),
                             group_of=group_of, bias=b_gate[l].reshape(-1), act="sigmoid", out_dtype=BF16,
                             name="branch_gates")

        w_uq, w_ukv = _mla_up_weights(a_w_uq[l], a_w_ukv[l])
        qa = token_matmul(u, w_uq, x_cols=(A_Q_LORA, U_CQ), prologue="rmsnorm", norm_g=a_q_norm[l], out_dtype=BF16,
                          name="mla_q_up")
        kva = token_matmul(u, w_ukv, x_cols=(A_KV_LORA, U_CKV), prologue="rmsnorm", norm_g=a_kv_norm[l],
                           out_dtype=BF16, name="mla_kv_up")
        ur = qk_prep(u, U_PREP0, U_PREP_N, U_PREP_R32, U_PREP_RMSQ, tables, d_q_norm[l], d_k_norm[l], t_lat, seq,
                     "qk_prep")
        qr = qk_prep(qa, A_HEADS, A_HEADS, A_HEADS, 0, tables, d_q_norm[l], d_k_norm[l], t_lat, seq, "mla_q_rope")
        mixers = {
            "mla": dict(q_srcs=[(qa, 0), (qr, 0)], k_srcs=[(kva, 0, 1), (ur, R_KR, 0)], v_src=(kva, A_HEADS, 1),
                        heads=A_HEADS, scale=A_SCALE),
            "swa": dict(q_srcs=[(ur, R_BQ)], k_srcs=[(ur, R_BK, 0)], v_src=(u, U_BV, 0), heads=B_HEADS,
                        scale=B_SCALE, sink=b_sink[l]),
            "na": dict(q_srcs=[(u, U_CQ_)], k_srcs=[(u, U_CK, 1)], v_src=(u, U_CV, 1), heads=C_HEADS,
                       scale=C_SCALE),
            "gqa": dict(q_srcs=[(ur, R_DQ)], k_srcs=[(ur, R_DK, D_HEADS // D_KV_HEADS)],
                        v_src=(u, U_DV, D_HEADS // D_KV_HEADS), heads=D_HEADS, scale=D_SCALE),
        }
        windows = {"swa": dict(window=(B_TQ, B_BACK, B_WK), bias=band_bias),
                   "na": dict(window=(C_TQ, C_BACK, C_WK), bias=neighbourhood_bias_table(c_rpb[l], rows_grid))}
        branches = []
        for mixer, cfg in mixers.items():
            o = attention(mode="lat", out_rows=rows_out, name=mixer + "_attention", **cfg, **windows.get(mixer, {}),
                          **dims)
            if need_ctx:
                o = attention(mode="ctx", out_prev=o, name=mixer + "_attention_ctx", **cfg, **dims)
            branches.append(o)

        xs = merge_out_ln(branches, _branch_weights(w_branch[l]), gates, w_out[l].astype(BF16), xs, mod, 2, group_of,
                          ln1_g[l], ln1_b[l], alpha)

        pq = token_matmul(xs, peer_w_q[l].astype(BF16), prologue="modulate", mod=mod, mod_idx=(3, 4),
                          group_of=group_of, out_dtype=BF16, name="peer_query")
        keys = peer_sub_keys[l].reshape(2 * PEER_HEADS, PEER_KEYS, PEER_QDIM // 2).astype(BF16)
        cnt, e0, rank, e1 = peer_topk(pq, keys)
        ht = modulate_transposed(xs, mod, (3, 4), group_of)
        n_exp = peer_u.shape[1]
        vt = jnp.transpose(peer_v[l].astype(BF16).reshape(n_exp // PEER_EB, PEER_EB, d), (0, 2, 1))
        ffn_t = peer_dense(ht, peer_u[l].astype(BF16), vt, cnt, e0, rank, e1)
        xs = residual_ln(xs, ffn_t, mod, 5, group_of, ln2_g[l], ln2_b[l], alpha)

    return xs[:t_lat].reshape(bsz, seq, d)
```

```python
import functools
import math

import jax
import jax.numpy as jnp
import numpy as np
from jax import lax
from jax.experimental import pallas as pl
from jax.experimental.pallas import tpu as pltpu

F32 = jnp.float32
BF16 = jnp.bfloat16

GRID_W = 64
ROPE_THETA = 10000.0
LN_EPS = 1e-5
RMS_EPS = 1e-6
NEG_INF = -1e30

A_HEADS, A_NOPE, A_ROPE, A_V, A_Q_LORA, A_KV_LORA = 4, 128, 64, 128, 384, 128
B_HEADS, B_KV_HEADS, B_DIM, B_WINDOW = 8, 2, 64, 128
C_HEADS, C_DIM, NA_ROWS, NA_COLS = 4, 128, 8, 16
D_HEADS, D_KV_HEADS, D_DIM = 4, 2, 128
N_BRANCH, BRANCH_W = 4, 512
PEER_HEADS, PEER_KEYS, PEER_QDIM, PEER_TOPK = 8, 128, 256, 16

A_SCALE = (A_NOPE + A_ROPE) ** -0.5
B_SCALE = B_DIM ** -0.5
C_SCALE = C_DIM ** -0.5
D_SCALE = D_DIM ** -0.5

IN_SIZES = (A_Q_LORA, A_KV_LORA, A_ROPE,
            B_HEADS * B_DIM, B_KV_HEADS * B_DIM, B_KV_HEADS * B_DIM,
            C_HEADS * C_DIM, C_HEADS * C_DIM, C_HEADS * C_DIM,
            D_HEADS * D_DIM, D_KV_HEADS * D_DIM, D_KV_HEADS * D_DIM)
LANE = 128
BF16_ROWS = 16

VMEM_LIMIT = 56 * 1024 * 1024

PEER_CHUNK = 512
PEER_EB = 1024
PEER_TOPK_TILE = 256

B_TQ, B_BACK, B_WK = 512, 128, 768
C_TQ, C_BACK, C_WK = 4 * GRID_W, 4 * GRID_W, 12 * GRID_W
DENSE_TQ = 256
ROPE_TM = 512

R_KR, R_BQ, R_BK, R_DQ, R_DK = 0, 1, 9, 10, 14
U_PREP_N, U_PREP_R32, U_PREP_RMSQ = 16, 10, 4
U_CKV, U_BV, U_CQ, U_CQ_, U_CK, U_CV, U_DV, U_BLOCKS = 16, 17, 18, 21, 25, 29, 33, 36


def _cparams(sem):
    return pltpu.CompilerParams(dimension_semantics=sem, vmem_limit_bytes=VMEM_LIMIT)


def _pick(n, cands):
    for c in cands:
        if n % c == 0:
            return c
    raise ValueError(f"no tile in {cands} divides {n}")


def _dot(a, b):
    return jnp.dot(a, b, preferred_element_type=F32)


def _dot_nt(a, b):
    return lax.dot_general(a, b, (((1,), (1,)), ((), ())), preferred_element_type=F32)


def _ada_kernel(c_ref, w_ref, b_ref, o_ref):
    c = c_ref[...]
    s = c * (1.0 / (1.0 + jnp.exp(-c)))
    w = w_ref[...]
    s_hi = s.astype(BF16)
    s_lo = (s - s_hi.astype(F32)).astype(BF16)
    w_hi = w.astype(BF16)
    w_lo = (w - w_hi.astype(F32)).astype(BF16)
    acc = _dot(s_hi, w_hi) + _dot(s_lo, w_hi) + _dot(s_hi, w_lo)
    o_ref[...] = acc + b_ref[...]


def ada_modulation(cc, w, b):
    r, k = cc.shape
    n = w.shape[1]
    tn = _pick(n, (512, 256, 128))
    return pl.pallas_call(
        _ada_kernel,
        grid=(n // tn,),
        in_specs=[pl.BlockSpec((r, k), lambda j: (0, 0)),
                  pl.BlockSpec((k, tn), lambda j: (0, j)),
                  pl.BlockSpec((1, tn), lambda j: (0, j))],
        out_specs=pl.BlockSpec((r, tn), lambda j: (0, j)),
        out_shape=jax.ShapeDtypeStruct((r, n), F32),
        compiler_params=_cparams(("arbitrary",)),
        name="ada_modulation",
    )(cc, w, b.reshape(1, n))


def _mm_kernel(*refs, prologue, has_bias, act):
    it = iter(refs)
    x_ref = next(it)
    if prologue == "modulate":
        sh_ref, sc_ref = next(it), next(it)
    elif prologue == "rmsnorm":
        g_ref = next(it)
    w_ref = next(it)
    b_ref = next(it) if has_bias else None
    o_ref = next(it)
    a_scr = next(it)

    @pl.when(pl.program_id(1) == 0)
    def _():
        x = x_ref[...].astype(F32)
        if prologue == "modulate":
            a = x * (1.0 + sc_ref[0]) + sh_ref[0]
        elif prologue == "rmsnorm":
            ms = jnp.mean(x * x, axis=-1, keepdims=True)
            a = (x * lax.rsqrt(ms + RMS_EPS)) * g_ref[...]
        else:
            a = x
        a_scr[...] = a.astype(BF16)

    acc = _dot(a_scr[...], w_ref[...])
    if has_bias:
        acc = acc + b_ref[...]
    if act == "sigmoid":
        acc = 1.0 / (1.0 + jnp.exp(-acc))
    o_ref[...] = acc.astype(o_ref.dtype)


def token_matmul(x, w, *, rows=None, x_cols=None, prologue=None, mod=None, mod_idx=None, group_of=None,
                 norm_g=None, bias=None, act=None, out_dtype=F32, name="token_matmul"):
    rows = x.shape[0] if rows is None else rows
    kdim, n = w.shape
    kw, kblk = (x.shape[1], 0) if x_cols is None else x_cols
    assert kw == kdim
    tm = _pick(rows, (512, 256))
    tn = _pick(n, (2048, 1536, 1024, 512, 256, 128))
    in_specs = [pl.BlockSpec((tm, kdim), lambda i, j: (i, kblk))]
    args = [x]
    if prologue == "modulate":
        tm_group = group_of(tm)
        for slot in mod_idx:
            in_specs.append(pl.BlockSpec((1, 1, kdim), lambda i, j, s=slot: (tm_group(i) * 6 + s, 0, 0)))
            args.append(mod)
    elif prologue == "rmsnorm":
        in_specs.append(pl.BlockSpec((1, kdim), lambda i, j: (0, 0)))
        args.append(norm_g.reshape(1, kdim))
    in_specs.append(pl.BlockSpec((kdim, tn), lambda i, j: (0, j)))
    args.append(w)
    if bias is not None:
        in_specs.append(pl.BlockSpec((1, tn), lambda i, j: (0, j)))
        args.append(bias.reshape(1, n))
    return pl.pallas_call(
        functools.partial(_mm_kernel, prologue=prologue, has_bias=bias is not None, act=act),
        grid=(rows // tm, n // tn),
        in_specs=in_specs,
        out_specs=pl.BlockSpec((tm, tn), lambda i, j: (i, j)),
        out_shape=jax.ShapeDtypeStruct((rows, n), out_dtype),
        scratch_shapes=[pltpu.VMEM((tm, kdim), BF16)],
        compiler_params=_cparams(("parallel", "arbitrary")),
        name=name,
    )(*args)


def _res_ln_kernel(x_ref, yt_ref, gate_ref, g_ref, b_ref, o_ref, *, alpha):
    z = alpha * x_ref[...] + gate_ref[0] * yt_ref[0].T
    mu = jnp.mean(z, axis=-1, keepdims=True)
    zc = z - mu
    var = jnp.mean(zc * zc, axis=-1, keepdims=True)
    o_ref[...] = (zc * lax.rsqrt(var + LN_EPS)) * g_ref[...] + b_ref[...]


def residual_ln(x, yt, mod, gate_slot, group_of, ln_g, ln_b, alpha):
    nch, d, tm = yt.shape
    rows = nch * tm
    tm_group = group_of(tm)
    return pl.pallas_call(
        functools.partial(_res_ln_kernel, alpha=alpha),
        grid=(nch,),
        in_specs=[pl.BlockSpec((tm, d), lambda i: (i, 0)),
                  pl.BlockSpec((1, d, tm), lambda i: (i, 0, 0)),
                  pl.BlockSpec((1, 1, d), lambda i: (tm_group(i) * 6 + gate_slot, 0, 0)),
                  pl.BlockSpec((1, d), lambda i: (0, 0)),
                  pl.BlockSpec((1, d), lambda i: (0, 0))],
        out_specs=pl.BlockSpec((tm, d), lambda i: (i, 0)),
        out_shape=jax.ShapeDtypeStruct((rows, d), F32),
        compiler_params=_cparams(("parallel",)),
        name="residual_ln",
    )(x, yt, mod, ln_g.reshape(1, d), ln_b.reshape(1, d))


def _qk_prep_kernel(x_ref, c32_ref, sa_ref, sb_ref, c64_ref, s64_ref, gq_ref, gk_ref, o_ref, *, n_r32, n_rms_q):
    for j in range(x_ref.shape[1] // LANE):
        cols = slice(j * LANE, (j + 1) * LANE)
        x = x_ref[:, cols].astype(F32)
        if j < n_r32:
            y = x * c32_ref[...] + pltpu.roll(x, 32, 1) * sa_ref[...] + pltpu.roll(x, 96, 1) * sb_ref[...]
        else:
            g = gq_ref[...] if j < n_r32 + n_rms_q else gk_ref[...]
            ms = jnp.mean(x * x, axis=-1, keepdims=True)
            y = (x * lax.rsqrt(ms + RMS_EPS)) * g
            y = y * c64_ref[...] + pltpu.roll(y, 64, 1) * s64_ref[...]
        o_ref[:, cols] = y.astype(o_ref.dtype)


def qk_prep(x, group_idx, n_blocks, n_r32, n_rms_q, tables, gq, gk, lat_rows, seq, name):
    rows = x.shape[0]
    tm = ROPE_TM
    width = n_blocks * LANE
    lat_blocks, per_seq = lat_rows // tm, seq // tm
    tab_spec = pl.BlockSpec((tm, LANE), lambda i: (jnp.where(i < lat_blocks, i % per_seq, per_seq), 0))
    vec_spec = pl.BlockSpec((1, LANE), lambda i: (0, 0))
    return pl.pallas_call(
        functools.partial(_qk_prep_kernel, n_r32=n_r32, n_rms_q=n_rms_q),
        grid=(rows // tm,),
        in_specs=[pl.BlockSpec((tm, width), lambda i: (i, group_idx))] + [tab_spec] * 5 + [vec_spec] * 2,
        out_specs=pl.BlockSpec((tm, width), lambda i: (i, 0)),
        out_shape=jax.ShapeDtypeStruct((rows, width), BF16),
        compiler_params=_cparams(("parallel",)),
        name=name,
    )(x, *tables, gq.reshape(1, LANE), gk.reshape(1, LANE))


def _attn_kernel(*refs, n_parts, has_lat, window, has_bias, has_sink, scale, n_lat):
    it = iter(refs)
    q_refs = [next(it) for _ in range(n_parts)]
    if has_lat:
        kl_refs = [next(it) for _ in range(n_parts)]
        vl_ref = next(it)
    kc_refs = [next(it) for _ in range(n_parts)]
    vc_ref = next(it)
    bias_ref = next(it) if has_bias else None
    sink_ref = next(it) if has_sink else None
    refs = list(it)
    o_ref = refs[-1]

    qs = [q[...] for q in q_refs]
    s2 = functools.reduce(jnp.add, [_dot_nt(q, k[...]) for q, k in zip(qs, kc_refs)]) * scale
    m = jnp.max(s2, axis=-1, keepdims=True)
    if has_lat:
        if window is None:
            kls = [k[...] for k in kl_refs]
            vl = vl_ref[...]
        else:
            tq, back, wk = window
            start = pl.multiple_of(jnp.clip(pl.program_id(2) * tq - back, 0, n_lat - wk), LANE)
            kls = [k[pl.ds(start, wk), :] for k in kl_refs]
            vl = vl_ref[pl.ds(start, wk), :]
        s1 = functools.reduce(jnp.add, [_dot_nt(q, k) for q, k in zip(qs, kls)]) * scale
        if has_bias:
            s1 = s1 + bias_ref[0, 0]
        m = jnp.maximum(m, jnp.max(s1, axis=-1, keepdims=True))
    if has_sink:
        sink = sink_ref[pl.program_id(1)]
        m = jnp.maximum(m, sink)
    p2 = jnp.exp(s2 - m)
    l = jnp.sum(p2, axis=-1, keepdims=True)
    o = _dot(p2.astype(BF16), vc_ref[...])
    if has_lat:
        p1 = jnp.exp(s1 - m)
        l = l + jnp.sum(p1, axis=-1, keepdims=True)
        o = o + _dot(p1.astype(BF16), vl)
    if has_sink:
        l = l + jnp.exp(sink - m)
    o_ref[...] = (o / l).astype(o_ref.dtype)


def attention(q_srcs, k_srcs, v_src, *, heads, scale, bsz, seq, nctx, t_lat, mode, out_rows=None,
              out_prev=None, window=None, bias=None, sink=None, name="attention"):
    lat = mode == "lat"
    n_parts = len(q_srcs)
    tq = (window[0] if window is not None else DENSE_TQ) if lat else nctx
    nqb = (seq if lat else nctx) // tq
    ctx0 = t_lat // nctx

    def kcol(first, grp, h):
        return first + (h // grp if grp else 0)

    def q_row(bi, i):
        return bi * nqb + i if lat else t_lat // tq + bi

    in_specs, args = [], []
    for arr, first in q_srcs:
        in_specs.append(pl.BlockSpec((tq, LANE), lambda bi, h, i, f=first: (q_row(bi, i), f + h)))
        args.append(arr)
    if lat:
        for arr, first, grp in list(k_srcs) + [v_src]:
            in_specs.append(pl.BlockSpec((seq, LANE), lambda bi, h, i, f=first, g=grp: (bi, kcol(f, g, h))))
            args.append(arr)
    for arr, first, grp in list(k_srcs) + [v_src]:
        in_specs.append(pl.BlockSpec((nctx, LANE), lambda bi, h, i, f=first, g=grp: (ctx0 + bi, kcol(f, g, h))))
        args.append(arr)
    if bias is not None:
        per_head = bias.shape[1] != 1
        in_specs.append(pl.BlockSpec(
            (1, 1, tq, window[2]),
            lambda bi, h, i: (jnp.where(i == 0, 0, jnp.where(i == nqb - 1, 2, 1)), h if per_head else 0, 0, 0)))
        args.append(bias)
    if sink is not None:
        in_specs.append(pl.BlockSpec(memory_space=pltpu.SMEM))
        args.append(sink.astype(F32))
    aliases = {}
    if out_prev is not None:
        aliases = {len(args): 0}
        in_specs.append(pl.BlockSpec(memory_space=pl.ANY))
        args.append(out_prev)
        out_rows = out_prev.shape[0]
    if window is not None:
        assert seq % tq == 0 and nqb >= 2 and seq >= window[2]
    return pl.pallas_call(
        functools.partial(_attn_kernel, n_parts=n_parts, has_lat=lat, window=window, has_bias=bias is not None,
                          has_sink=sink is not None, scale=scale, n_lat=seq),
        grid=(bsz, heads, nqb),
        in_specs=in_specs,
        out_specs=pl.BlockSpec((tq, LANE), lambda bi, h, i: (q_row(bi, i), h)),
        out_shape=jax.ShapeDtypeStruct((out_rows, heads * LANE), BF16),
        input_output_aliases=aliases,
        compiler_params=_cparams(("parallel", "parallel", "arbitrary")),
        name=name,
    )(*args)


def _window_q_offsets(geom):
    tq, back, wk = geom
    return (0, back, wk - tq)


def band_bias_table():
    tq, _, wk = (B_TQ, B_BACK, B_WK)
    kpos = np.arange(wk)[None, :]
    tabs = []
    for off in _window_q_offsets((B_TQ, B_BACK, B_WK)):
        qpos = off + np.arange(tq)[:, None]
        tabs.append(np.where(np.abs(qpos - kpos) <= B_WINDOW, 0.0, NEG_INF))
    return jnp.asarray(np.stack(tabs)[:, None], F32)


def neighbourhood_bias_table(rpb, rows):
    tq, back, wk = (C_TQ, C_BACK, C_WK)
    assert rows >= wk // GRID_W and rows >= NA_ROWS
    nqb = rows * GRID_W // tq
    nq_rows, nk_rows = tq // GRID_W, wk // GRID_W
    cols = np.arange(GRID_W)
    col_start = np.clip(cols - NA_COLS // 2, 0, GRID_W - NA_COLS)
    col_ok = (cols[None, :] >= col_start[:, None]) & (cols[None, :] < col_start[:, None] + NA_COLS)
    col_off = np.clip(cols[None, :] - cols[:, None] + (NA_COLS - 1), 0, 2 * NA_COLS - 2)
    onehot_c = jnp.asarray(col_off[..., None] == np.arange(2 * NA_COLS - 1), F32)
    tabs = []
    for cls, off in enumerate(_window_q_offsets((C_TQ, C_BACK, C_WK))):
        blk = (0, 1, nqb - 1)[cls]
        w0 = int(np.clip(blk * tq - back, 0, rows * GRID_W - wk)) // GRID_W
        assert w0 * GRID_W + off == blk * tq and off % GRID_W == 0
        q_row = w0 + off // GRID_W + np.arange(nq_rows)
        k_row = w0 + np.arange(nk_rows)
        row_start = np.clip(q_row - NA_ROWS // 2, 0, rows - NA_ROWS)
        row_ok = (k_row[None, :] >= row_start[:, None]) & (k_row[None, :] < row_start[:, None] + NA_ROWS)
        row_off = np.clip(k_row[None, :] - q_row[:, None] + (NA_ROWS - 1), 0, 2 * NA_ROWS - 2)
        by_row = rpb[:, row_off, :].astype(F32)
        vals = jnp.einsum("hrsc,qkc->hrqsk", by_row, onehot_c, precision=lax.Precision.HIGHEST)
        ok = row_ok[:, None, :, None] & col_ok[None, :, None, :]
        tabs.append(jnp.where(jnp.asarray(ok)[None], vals, NEG_INF).reshape(rpb.shape[0], tq, wk))
    return jnp.stack(tabs)


def _merge_out_ln_kernel(*refs, alpha, n_branch):
    o_refs = refs[:n_branch]
    wbr_refs = refs[n_branch:2 * n_branch]
    g_ref, wout_ref, x_ref, gate_ref, lng_ref, lnb_ref, out_ref = refs[2 * n_branch:]
    d = x_ref.shape[1]
    y = None
    for r, (o_ref, wbr_ref) in enumerate(zip(o_refs, wbr_refs)):
        t = g_ref[:, r * d:(r + 1) * d].astype(F32) * _dot(o_ref[...], wbr_ref[...])
        y = t if y is None else y + t
    mix = _dot(y.astype(BF16), wout_ref[...])
    z = alpha * x_ref[...] + gate_ref[0] * mix
    mu = jnp.mean(z, axis=-1, keepdims=True)
    zc = z - mu
    var = jnp.mean(zc * zc, axis=-1, keepdims=True)
    out_ref[...] = (zc * lax.rsqrt(var + LN_EPS)) * lng_ref[...] + lnb_ref[...]


def merge_out_ln(branches, w_branch, gates, w_out, x, mod, gate_slot, group_of, ln_g, ln_b, alpha):
    rows = gates.shape[0]
    nbr = len(branches)
    d = w_out.shape[0]
    tm = _pick(rows, (256,))
    tm_group = group_of(tm)
    resident = pl.Buffered(1)
    return pl.pallas_call(
        functools.partial(_merge_out_ln_kernel, alpha=alpha, n_branch=nbr),
        grid=(rows // tm,),
        in_specs=[pl.BlockSpec((tm, o.shape[1]), lambda i: (i, 0)) for o in branches] + [
                  pl.BlockSpec(w.shape, lambda i: (0, 0), pipeline_mode=resident) for w in w_branch] + [
                  pl.BlockSpec((tm, nbr * d), lambda i: (i, 0)),
                  pl.BlockSpec((d, d), lambda i: (0, 0), pipeline_mode=resident),
                  pl.BlockSpec((tm, d), lambda i: (i, 0)),
                  pl.BlockSpec((1, 1, d), lambda i: (tm_group(i) * 6 + gate_slot, 0, 0)),
                  pl.BlockSpec((1, d), lambda i: (0, 0)),
                  pl.BlockSpec((1, d), lambda i: (0, 0))],
        out_specs=pl.BlockSpec((tm, d), lambda i: (i, 0)),
        out_shape=jax.ShapeDtypeStruct((rows, d), F32),
        compiler_params=_cparams(("parallel",)),
        name="merge_out_ln",
    )(*branches, *w_branch, gates, w_out, x, mod, ln_g.reshape(1, d), ln_b.reshape(1, d))


def _peer_pairs():
    return [(a, b) for a in range(PEER_TOPK) for b in range(PEER_TOPK) if (a + 1) * (b + 1) <= PEER_TOPK]


def _peer_topk_kernel(q_ref, keys_ref, cnt_ref, e0_ref, rank_ref, e1_ref, s0_scr):
    tops = [[[None] * PEER_HEADS for _ in range(PEER_TOPK)] for _ in range(2)]
    for h in range(PEER_HEADS):
        for p in range(2):
            hp = 2 * h + p
            qs = q_ref[:, hp * PEER_KEYS:(hp + 1) * PEER_KEYS]
            s = _dot_nt(keys_ref[hp], qs)
            work = s
            rank = jnp.full(s.shape, float(PEER_TOPK), F32)
            for r in range(PEER_TOPK):
                m = jnp.max(work, axis=0, keepdims=True)
                tops[p][r][h] = m
                if p == 1:
                    hit = work == m
                    rank = jnp.where(hit, float(r), rank)
                    work = jnp.where(hit, -jnp.inf, work)
                elif r + 1 < PEER_TOPK:
                    work = jnp.where(work == m, -jnp.inf, work)
            if p == 0:
                s0_scr[h] = s
            else:
                rank_ref[h, 0] = rank.astype(BF16)
                e1_ref[h, 0] = jnp.exp(s - tops[1][0][h]).astype(BF16)
    top = [[jnp.concatenate(tops[p][r], axis=0) for r in range(PEER_TOPK)] for p in range(2)]
    cands = [top[0][a] + top[1][b] for a, b in _peer_pairs()]
    work = list(cands)
    for r in range(PEER_TOPK):
        m = functools.reduce(jnp.maximum, work)
        if r + 1 < PEER_TOPK:
            work = [jnp.where(w == m, -jnp.inf, w) for w in work]
    tau = m
    cmax = cands[0]
    z = functools.reduce(jnp.add, [jnp.where(c >= tau, jnp.exp(c - cmax), 0.0) for c in cands])
    rz = 1.0 / z
    for h in range(PEER_HEADS):
        s0 = s0_scr[h]
        tau_h = tau[h:h + 1]
        cnt = None
        for b in range(PEER_TOPK):
            inc = jnp.where(s0 + top[1][b][h:h + 1] >= tau_h, 1.0, 0.0)
            cnt = inc if cnt is None else cnt + inc
        cnt_ref[h, 0] = cnt
        e0_ref[h, 0] = jnp.exp(s0 - top[0][0][h:h + 1]) * rz[h:h + 1]


def peer_topk(q, keys):
    rows = q.shape[0]
    tt = PEER_TOPK_TILE
    per_chunk = PEER_CHUNK // tt
    nch = rows // PEER_CHUNK
    big = pl.BlockSpec((PEER_HEADS, 1, PEER_KEYS, tt), lambda i: (0, i // per_chunk, 0, i % per_chunk))
    shape_f32 = jax.ShapeDtypeStruct((PEER_HEADS, nch, PEER_KEYS, PEER_CHUNK), F32)
    shape_bf16 = jax.ShapeDtypeStruct((PEER_HEADS, nch, PEER_KEYS, PEER_CHUNK), BF16)
    return pl.pallas_call(
        _peer_topk_kernel,
        grid=(rows // tt,),
        in_specs=[pl.BlockSpec((tt, q.shape[1]), lambda i: (i, 0)),
                  pl.BlockSpec(keys.shape, lambda i: (0, 0, 0))],
        out_specs=[big, big, big, big],
        out_shape=[shape_f32, shape_f32, shape_bf16, shape_bf16],
        scratch_shapes=[pltpu.VMEM((PEER_HEADS, PEER_KEYS, tt), F32)],
        compiler_params=_cparams(("parallel",)),
        name="peer_topk",
    )(q, keys)


def _modulate_t_kernel(x_ref, sh_ref, sc_ref, o_ref):
    h = x_ref[...] * (1.0 + sc_ref[0]) + sh_ref[0]
    o_ref[0] = h.T.astype(BF16)


def modulate_transposed(x, mod, mod_idx, group_of):
    rows, d = x.shape
    tt = PEER_CHUNK
    tt_group = group_of(tt)
    mod_specs = [pl.BlockSpec((1, 1, d), lambda i, s=slot: (tt_group(i) * 6 + s, 0, 0)) for slot in mod_idx]
    return pl.pallas_call(
        _modulate_t_kernel,
        grid=(rows // tt,),
        in_specs=[pl.BlockSpec((tt, d), lambda i: (i, 0))] + mod_specs,
        out_specs=pl.BlockSpec((1, d, tt), lambda i: (i, 0, 0)),
        out_shape=jax.ShapeDtypeStruct((rows // tt, d, tt), BF16),
        compiler_params=_cparams(("parallel",)),
        name="modulate_transposed",
    )(x, mod, mod)


def _peer_dense_kernel(ht_ref, u_ref, vt_ref, cnt_ref, e0_ref, rank_ref, e1_ref, o_ref, a_scr, w_scr, p_scr, *, n_e):
    e = pl.program_id(1)
    cur = e % 2
    prev = 1 - cur
    i_per_eb = PEER_EB // PEER_KEYS
    tc = w_scr.shape[-1]

    def gate_and_down():
        for ii in range(i_per_eb):
            sl = slice(ii * PEER_KEYS, (ii + 1) * PEER_KEYS)
            a = a_scr[prev, sl, :]
            act = 0.5 * a * (1.0 + lax.erf(a * (1.0 / math.sqrt(2.0))))
            p_scr[sl, :] = act.astype(BF16) * w_scr[prev, sl, :]
        o_ref[0] += _dot(vt_ref[0], p_scr[...])

    def up_and_weights():
        a_scr[cur] = _dot(u_ref[...], ht_ref[0])
        for ii in range(i_per_eb):
            i = e * i_per_eb + ii
            cnt = [jnp.broadcast_to(cnt_ref[h, 0, pl.ds(i, 1), :], (BF16_ROWS, tc)).astype(BF16)
                   for h in range(PEER_HEADS)]
            e0 = [jnp.broadcast_to(e0_ref[h, 0, pl.ds(i, 1), :], (BF16_ROWS, tc)).astype(BF16)
                  for h in range(PEER_HEADS)]
            for g in range(PEER_KEYS // BF16_ROWS):
                rows = slice(g * BF16_ROWS, (g + 1) * BF16_ROWS)
                w = None
                for h in range(PEER_HEADS):
                    e1 = e1_ref[h, 0, rows, :]
                    t = jnp.where(rank_ref[h, 0, rows, :] < cnt[h], e1, jnp.zeros_like(e1)) * e0[h]
                    w = t if w is None else w + t
                w_scr[cur, ii * PEER_KEYS + g * BF16_ROWS:ii * PEER_KEYS + (g + 1) * BF16_ROWS, :] = w

    @pl.when(e == 0)
    def _():
        o_ref[...] = jnp.zeros_like(o_ref)
        up_and_weights()

    @pl.when(jnp.logical_and(e > 0, e < n_e))
    def _():
        gate_and_down()
        up_and_weights()

    @pl.when(e == n_e)
    def _():
        gate_and_down()


def peer_dense(ht, u, vt, cnt, e0, rank, e1):
    nch, d, tc = ht.shape
    n_e = u.shape[0] // PEER_EB
    sel = pl.BlockSpec((PEER_HEADS, 1, PEER_KEYS, tc), lambda t, e: (0, t, 0, 0))
    return pl.pallas_call(
        functools.partial(_peer_dense_kernel, n_e=n_e),
        grid=(nch, n_e + 1),
        in_specs=[pl.BlockSpec((1, d, tc), lambda t, e: (t, 0, 0)),
                  pl.BlockSpec((PEER_EB, d), lambda t, e: (jnp.minimum(e, n_e - 1), 0)),
                  pl.BlockSpec((1, d, PEER_EB), lambda t, e: (jnp.maximum(e - 1, 0), 0, 0)),
                  sel, sel, sel, sel],
        out_specs=pl.BlockSpec((1, d, tc), lambda t, e: (t, 0, 0)),
        out_shape=jax.ShapeDtypeStruct((nch, d, tc), F32),
        scratch_shapes=[pltpu.VMEM((2, PEER_EB, tc), F32), pltpu.VMEM((2, PEER_EB, tc), BF16),
                        pltpu.VMEM((PEER_EB, tc), BF16)],
        compiler_params=_cparams(("parallel", "arbitrary")),
        name="peer_dense",
    )(ht, u, vt, cnt, e0, rank, e1)


def _axial_angles(n, rot_dim):
    t = jnp.arange(n)
    rows = (t // GRID_W).astype(F32)
    cols = (t % GRID_W).astype(F32)
    nf = rot_dim // 4
    inv = ROPE_THETA ** (-jnp.arange(nf, dtype=F32) / nf)
    ang = jnp.concatenate([rows[:, None] * inv, cols[:, None] * inv], axis=-1)
    return jnp.cos(ang), jnp.sin(ang)


def _rope_tables(seq):
    assert A_ROPE == B_DIM == LANE // 2 and D_DIM == LANE

    def pad(t, fill):
        return jnp.concatenate([t, jnp.full((ROPE_TM, LANE), fill, F32)], axis=0)

    cos, sin = _axial_angles(seq, B_DIM)
    z = jnp.zeros_like(sin)
    c32 = jnp.concatenate([cos, cos, cos, cos], axis=1)
    sa = jnp.concatenate([z, sin, z, sin], axis=1)
    sb = jnp.concatenate([-sin, z, -sin, z], axis=1)
    cos, sin = _axial_angles(seq, D_DIM)
    c64 = jnp.concatenate([cos, cos], axis=1)
    s64 = jnp.concatenate([-sin, sin], axis=1)
    return [pad(c32, 1.0), pad(sa, 0.0), pad(sb, 0.0), pad(c64, 1.0), pad(s64, 0.0)]


def _half_block(t, first_half, axis):
    z = jnp.zeros_like(t)
    return jnp.concatenate([t, z] if first_half else [z, t], axis=axis)


def _in_proj_weight(w_in):
    d = w_in.shape[0]
    o = np.cumsum((0,) + IN_SIZES)
    part = [w_in[:, o[i]:o[i + 1]] for i in range(len(IN_SIZES))]
    per_kv = B_HEADS // B_KV_HEADS
    bq = [_half_block(part[3][:, h * B_DIM:(h + 1) * B_DIM], h // per_kv == 0, 1) for h in range(B_HEADS)]
    cols = [_half_block(part[2], True, 1)] + bq + [
        part[4], part[9], part[10], part[1], part[5], part[0], part[6], part[7], part[8], part[11],
        jnp.zeros((d, LANE), w_in.dtype)]
    out = jnp.concatenate(cols, axis=1).astype(BF16)
    assert out.shape[1] == U_BLOCKS * LANE
    return out


def _mla_up_weights(w_uq, w_ukv):
    dq, dkv = A_NOPE + A_ROPE, A_NOPE + A_V
    q_nope = [w_uq[:, h * dq:h * dq + A_NOPE] for h in range(A_HEADS)]
    q_rope = [_half_block(w_uq[:, h * dq + A_NOPE:(h + 1) * dq], True, 1) for h in range(A_HEADS)]
    k_nope = [w_ukv[:, h * dkv:h * dkv + A_NOPE] for h in range(A_HEADS)]
    v = [w_ukv[:, h * dkv + A_NOPE:(h + 1) * dkv] for h in range(A_HEADS)]
    return (jnp.concatenate(q_nope + q_rope, axis=1).astype(BF16), jnp.concatenate(k_nope + v, axis=1).astype(BF16))


def _branch_weights(w_branch):
    per_kv = B_HEADS // B_KV_HEADS
    wb = jnp.concatenate([_half_block(w_branch[1][h * B_DIM:(h + 1) * B_DIM], h // per_kv == 0, 0)
                          for h in range(B_HEADS)], axis=0)
    return [w_branch[0].astype(BF16), wb.astype(BF16), w_branch[2].astype(BF16), w_branch[3].astype(BF16)]


def kernel(x, c, ctx, c_ctx, w_ada, b_ada, w_in, a_q_norm, a_kv_norm, a_w_uq, a_w_ukv, b_sink, c_rpb, d_q_norm, d_k_norm, w_branch, w_gate, b_gate, w_out, ln1_g, ln1_b, ln2_g, ln2_b, peer_w_q, peer_sub_keys, peer_u, peer_v):
    bsz, seq, d = x.shape
    nctx = ctx.shape[1]
    depth = w_in.shape[0]
    t_lat, t_ctx = bsz * seq, bsz * nctx
    t_all = t_lat + t_ctx
    rows_grid = seq // GRID_W
    alpha = (2 * depth) ** 0.25

    tables = _rope_tables(seq)
    band_bias = band_bias_table()
    dims = dict(bsz=bsz, seq=seq, nctx=nctx, t_lat=t_lat)

    n_groups = -(-(bsz + 1) // 8) * 8
    cc = jnp.zeros((n_groups, d), F32).at[:bsz].set(c).at[bsz].set(c_ctx)

    def group_of(tm):
        lat_blocks, per_batch = t_lat // tm, seq // tm
        assert t_lat % tm == 0 and seq % tm == 0
        return lambda i: jnp.where(i < lat_blocks, i // per_batch, bsz)

    xs = jnp.concatenate([x.reshape(t_lat, d), ctx.reshape(t_ctx, d)], axis=0)

    for l in range(depth):
        need_ctx = l < depth - 1
        rows_out = t_all if need_ctx else t_lat
        mod = ada_modulation(cc, w_ada[l], b_ada[l]).reshape(n_groups * 6, 1, d)

        u = token_matmul(xs, _in_proj_weight(w_in[l]), prologue="modulate", mod=mod, mod_idx=(0, 1),
                         group_of=group_of, out_dtype=BF16, name="in_proj")
        w_gate_l = jnp.transpose(w_gate[l], (1, 0, 2)).reshape(d, N_BRANCH * d).astype(BF16)
        gates = token_matmul(xs, w_gate_l, rows=rows_out, prologue="modulate", mod=mod, mod_idx=(0, 1),
                             group_of=group_of, bias=b_gate[l].reshape(-1), act="sigmoid", out_dtype=BF16,
                             name="branch_gates")

        w_uq, w_ukv = _mla_up_weights(a_w_uq[l], a_w_ukv[l])
        qa = token_matmul(u, w_uq, x_cols=(A_Q_LORA, U_CQ * LANE // A_Q_LORA), prologue="rmsnorm", norm_g=a_q_norm[l], out_dtype=BF16,
                          name="mla_q_up")
        kva = token_matmul(u, w_ukv, x_cols=(A_KV_LORA, U_CKV), prologue="rmsnorm", norm_g=a_kv_norm[l],
                           out_dtype=BF16, name="mla_kv_up")
        ur = qk_prep(u, 0, U_PREP_N, U_PREP_R32, U_PREP_RMSQ, tables, d_q_norm[l], d_k_norm[l], t_lat, seq, "qk_prep")
        qr = qk_prep(qa, 1, A_HEADS, A_HEADS, 0, tables, d_q_norm[l], d_k_norm[l], t_lat, seq, "mla_q_rope")
        mixers = {
            "mla": dict(q_srcs=[(qa, 0), (qr, 0)], k_srcs=[(kva, 0, 1), (ur, R_KR, 0)], v_src=(kva, A_HEADS, 1),
                        heads=A_HEADS, scale=A_SCALE),
            "swa": dict(q_srcs=[(ur, R_BQ)], k_srcs=[(ur, R_BK, 0)], v_src=(u, U_BV, 0), heads=B_HEADS,
                        scale=B_SCALE, sink=b_sink[l]),
            "na": dict(q_srcs=[(u, U_CQ_)], k_srcs=[(u, U_CK, 1)], v_src=(u, U_CV, 1), heads=C_HEADS,
                       scale=C_SCALE),
            "gqa": dict(q_srcs=[(ur, R_DQ)], k_srcs=[(ur, R_DK, D_HEADS // D_KV_HEADS)],
                        v_src=(u, U_DV, D_HEADS // D_KV_HEADS), heads=D_HEADS, scale=D_SCALE),
        }
        windows = {"swa": dict(window=(B_TQ, B_BACK, B_WK), bias=band_bias),
                   "na": dict(window=(C_TQ, C_BACK, C_WK), bias=neighbourhood_bias_table(c_rpb[l], rows_grid))}
        branches = []
        for mixer, cfg in mixers.items():
            o = attention(mode="lat", out_rows=rows_out, name=mixer + "_attention", **cfg, **windows.get(mixer, {}),
                          **dims)
            if need_ctx:
                o = attention(mode="ctx", out_prev=o, name=mixer + "_attention_ctx", **cfg, **dims)
            branches.append(o)

        xs = merge_out_ln(branches, _branch_weights(w_branch[l]), gates, w_out[l].astype(BF16), xs, mod, 2, group_of,
                          ln1_g[l], ln1_b[l], alpha)

        pq = token_matmul(xs, peer_w_q[l].astype(BF16), prologue="modulate", mod=mod, mod_idx=(3, 4),
                          group_of=group_of, out_dtype=BF16, name="peer_query")
        keys = peer_sub_keys[l].reshape(2 * PEER_HEADS, PEER_KEYS, PEER_QDIM // 2).astype(BF16)
        cnt, e0, rank, e1 = peer_topk(pq, keys)
        ht = modulate_transposed(xs, mod, (3, 4), group_of)
        n_exp = peer_u.shape[1]
        vt = jnp.transpose(peer_v[l].astype(BF16).reshape(n_exp // PEER_EB, PEER_EB, d), (0, 2, 1))
        ffn_t = peer_dense(ht, peer_u[l].astype(BF16), vt, cnt, e0, rank, e1)
        xs = residual_ln(xs, ffn_t, mod, 5, group_of, ln2_g[l], ln2_b[l], alpha)

    return xs[:t_lat].reshape(bsz, seq, d)
```

```python
import functools
import math

import jax
import jax.numpy as jnp
import numpy as np
from jax import lax
from jax.experimental import pallas as pl
from jax.experimental.pallas import tpu as pltpu

F32 = jnp.float32
BF16 = jnp.bfloat16

GRID_W = 64
ROPE_THETA = 10000.0
LN_EPS = 1e-5
RMS_EPS = 1e-6
NEG_INF = -1e30

A_HEADS, A_NOPE, A_ROPE, A_V, A_Q_LORA, A_KV_LORA = 4, 128, 64, 128, 384, 128
B_HEADS, B_KV_HEADS, B_DIM, B_WINDOW = 8, 2, 64, 128
C_HEADS, C_DIM, NA_ROWS, NA_COLS = 4, 128, 8, 16
D_HEADS, D_KV_HEADS, D_DIM = 4, 2, 128
N_BRANCH, BRANCH_W = 4, 512
PEER_HEADS, PEER_KEYS, PEER_QDIM, PEER_TOPK = 8, 128, 256, 16

A_SCALE = (A_NOPE + A_ROPE) ** -0.5
B_SCALE = B_DIM ** -0.5
C_SCALE = C_DIM ** -0.5
D_SCALE = D_DIM ** -0.5

IN_SIZES = (A_Q_LORA, A_KV_LORA, A_ROPE,
            B_HEADS * B_DIM, B_KV_HEADS * B_DIM, B_KV_HEADS * B_DIM,
            C_HEADS * C_DIM, C_HEADS * C_DIM, C_HEADS * C_DIM,
            D_HEADS * D_DIM, D_KV_HEADS * D_DIM, D_KV_HEADS * D_DIM)
LANE = 128
BF16_ROWS = 16

VMEM_LIMIT = 56 * 1024 * 1024

PEER_CHUNK = 512
PEER_EB = 1024
PEER_TOPK_TILE = 256

B_TQ, B_BACK, B_WK = 512, 128, 768
C_TQ, C_BACK, C_WK = 4 * GRID_W, 4 * GRID_W, 12 * GRID_W
DENSE_TQ = 512
DENSE_SUB = 2
ROPE_TM = 512

R_BQ, R_KR, R_BK, R_DQ, R_DK = 0, 8, 9, 10, 14
U_PREP_N, U_PREP_R32, U_PREP_RMSQ = 16, 10, 4
U_CQ_, U_CK, U_CV, U_CKV, U_BV, U_CQ, U_DV, U_BLOCKS = 16, 20, 24, 28, 29, 30, 33, 36


def _cparams(sem):
    return pltpu.CompilerParams(dimension_semantics=sem, vmem_limit_bytes=VMEM_LIMIT)


def _pick(n, cands):
    for c in cands:
        if n % c == 0:
            return c
    raise ValueError(f"no tile in {cands} divides {n}")


def _dot(a, b):
    return jnp.dot(a, b, preferred_element_type=F32)


def _dot_nt(a, b):
    return lax.dot_general(a, b, (((1,), (1,)), ((), ())), preferred_element_type=F32)


def _ada_kernel(c_ref, w_ref, b_ref, o_ref):
    c = c_ref[...]
    s = c * (1.0 / (1.0 + jnp.exp(-c)))
    w = w_ref[...]
    s_hi = s.astype(BF16)
    s_lo = (s - s_hi.astype(F32)).astype(BF16)
    w_hi = w.astype(BF16)
    w_lo = (w - w_hi.astype(F32)).astype(BF16)
    acc = _dot(s_hi, w_hi) + _dot(s_lo, w_hi) + _dot(s_hi, w_lo)
    o_ref[...] = acc + b_ref[...]


def ada_modulation(cc, w, b):
    r, k = cc.shape
    n = w.shape[1]
    tn = _pick(n, (512, 256, 128))
    return pl.pallas_call(
        _ada_kernel,
        grid=(n // tn,),
        in_specs=[pl.BlockSpec((r, k), lambda j: (0, 0)),
                  pl.BlockSpec((k, tn), lambda j: (0, j)),
                  pl.BlockSpec((1, tn), lambda j: (0, j))],
        out_specs=pl.BlockSpec((r, tn), lambda j: (0, j)),
        out_shape=jax.ShapeDtypeStruct((r, n), F32),
        compiler_params=_cparams(("arbitrary",)),
        name="ada_modulation",
    )(cc, w, b.reshape(1, n))


def _mm_kernel(*refs, prologue, has_bias, act):
    it = iter(refs)
    x_ref = next(it)
    if prologue == "modulate":
        sh_ref, sc_ref = next(it), next(it)
    elif prologue == "rmsnorm":
        g_ref = next(it)
    w_ref = next(it)
    b_ref = next(it) if has_bias else None
    o_ref = next(it)
    a_scr = next(it)

    @pl.when(pl.program_id(1) == 0)
    def _():
        x = x_ref[...].astype(F32)
        if prologue == "modulate":
            a = x * (1.0 + sc_ref[0]) + sh_ref[0]
        elif prologue == "rmsnorm":
            ms = jnp.mean(x * x, axis=-1, keepdims=True)
            a = (x * lax.rsqrt(ms + RMS_EPS)) * g_ref[...]
        else:
            a = x
        a_scr[...] = a.astype(BF16)

    acc = _dot(a_scr[...], w_ref[...])
    if has_bias:
        acc = acc + b_ref[...]
    if act == "sigmoid":
        acc = 1.0 / (1.0 + jnp.exp(-acc))
    o_ref[...] = acc.astype(o_ref.dtype)


def token_matmul(x, w, *, rows=None, x_cols=None, prologue=None, mod=None, mod_idx=None, group_of=None,
                 norm_g=None, bias=None, act=None, out_dtype=F32, name="token_matmul"):
    rows = x.shape[0] if rows is None else rows
    kdim, n = w.shape
    kw, kblk = (x.shape[1], 0) if x_cols is None else x_cols
    assert kw == kdim
    tm = _pick(rows, (512, 256))
    tn = _pick(n, (2048, 1536, 1024, 512, 256, 128))
    in_specs = [pl.BlockSpec((tm, kdim), lambda i, j: (i, kblk))]
    args = [x]
    if prologue == "modulate":
        tm_group = group_of(tm)
        for slot in mod_idx:
            in_specs.append(pl.BlockSpec((1, 1, kdim), lambda i, j, s=slot: (tm_group(i) * 6 + s, 0, 0)))
            args.append(mod)
    elif prologue == "rmsnorm":
        in_specs.append(pl.BlockSpec((1, kdim), lambda i, j: (0, 0)))
        args.append(norm_g.reshape(1, kdim))
    in_specs.append(pl.BlockSpec((kdim, tn), lambda i, j: (0, j)))
    args.append(w)
    if bias is not None:
        in_specs.append(pl.BlockSpec((1, tn), lambda i, j: (0, j)))
        args.append(bias.reshape(1, n))
    return pl.pallas_call(
        functools.partial(_mm_kernel, prologue=prologue, has_bias=bias is not None, act=act),
        grid=(rows // tm, n // tn),
        in_specs=in_specs,
        out_specs=pl.BlockSpec((tm, tn), lambda i, j: (i, j)),
        out_shape=jax.ShapeDtypeStruct((rows, n), out_dtype),
        scratch_shapes=[pltpu.VMEM((tm, kdim), BF16)],
        compiler_params=_cparams(("parallel", "arbitrary")),
        name=name,
    )(*args)


def _res_ln_kernel(x_ref, yt_ref, gate_ref, g_ref, b_ref, o_ref, *, alpha):
    z = alpha * x_ref[...] + gate_ref[0] * yt_ref[0].T
    mu = jnp.mean(z, axis=-1, keepdims=True)
    zc = z - mu
    var = jnp.mean(zc * zc, axis=-1, keepdims=True)
    o_ref[...] = (zc * lax.rsqrt(var + LN_EPS)) * g_ref[...] + b_ref[...]


def residual_ln(x, yt, mod, gate_slot, group_of, ln_g, ln_b, alpha):
    nch, d, tm = yt.shape
    rows = nch * tm
    tm_group = group_of(tm)
    return pl.pallas_call(
        functools.partial(_res_ln_kernel, alpha=alpha),
        grid=(nch,),
        in_specs=[pl.BlockSpec((tm, d), lambda i: (i, 0)),
                  pl.BlockSpec((1, d, tm), lambda i: (i, 0, 0)),
                  pl.BlockSpec((1, 1, d), lambda i: (tm_group(i) * 6 + gate_slot, 0, 0)),
                  pl.BlockSpec((1, d), lambda i: (0, 0)),
                  pl.BlockSpec((1, d), lambda i: (0, 0))],
        out_specs=pl.BlockSpec((tm, d), lambda i: (i, 0)),
        out_shape=jax.ShapeDtypeStruct((rows, d), F32),
        compiler_params=_cparams(("parallel",)),
        name="residual_ln",
    )(x, yt, mod, ln_g.reshape(1, d), ln_b.reshape(1, d))


def _qk_prep_kernel(x_ref, c32_ref, sa_ref, sb_ref, c64_ref, s64_ref, gq_ref, gk_ref, o_ref, *, n_r32, n_rms_q):
    for j in range(x_ref.shape[1] // LANE):
        cols = slice(j * LANE, (j + 1) * LANE)
        x = x_ref[:, cols].astype(F32)
        if j < n_r32:
            y = x * c32_ref[...] + pltpu.roll(x, 32, 1) * sa_ref[...] + pltpu.roll(x, 96, 1) * sb_ref[...]
        else:
            g = gq_ref[...] if j < n_r32 + n_rms_q else gk_ref[...]
            ms = jnp.mean(x * x, axis=-1, keepdims=True)
            y = (x * lax.rsqrt(ms + RMS_EPS)) * g
            y = y * c64_ref[...] + pltpu.roll(y, 64, 1) * s64_ref[...]
        o_ref[:, cols] = y.astype(o_ref.dtype)


def qk_prep(x, group_idx, n_blocks, n_r32, n_rms_q, tables, gq, gk, lat_rows, seq, name):
    rows = x.shape[0]
    tm = ROPE_TM
    width = n_blocks * LANE
    lat_blocks, per_seq = lat_rows // tm, seq // tm
    tab_spec = pl.BlockSpec((tm, LANE), lambda i: (jnp.where(i < lat_blocks, i % per_seq, per_seq), 0))
    vec_spec = pl.BlockSpec((1, LANE), lambda i: (0, 0))
    return pl.pallas_call(
        functools.partial(_qk_prep_kernel, n_r32=n_r32, n_rms_q=n_rms_q),
        grid=(rows // tm,),
        in_specs=[pl.BlockSpec((tm, width), lambda i: (i, group_idx))] + [tab_spec] * 5 + [vec_spec] * 2,
        out_specs=pl.BlockSpec((tm, width), lambda i: (i, 0)),
        out_shape=jax.ShapeDtypeStruct((rows, width), BF16),
        compiler_params=_cparams(("parallel",)),
        name=name,
    )(x, *tables, gq.reshape(1, LANE), gk.reshape(1, LANE))


def _attn_kernel(*refs, n_parts, has_lat, window, has_bias, bias_per_head, has_sink, scale, n_lat, n_sub, hps,
                 per_head):
    it = iter(refs)
    q_refs = [next(it) for _ in range(n_parts)]
    if has_lat:
        kl_refs = [next(it) for _ in range(n_parts)]
        vl_ref = next(it)
    kc_refs = [next(it) for _ in range(n_parts)]
    vc_ref = next(it)
    bias_ref = next(it) if has_bias else None
    sink_ref = next(it) if has_sink else None
    refs = list(it)
    o_ref = refs[-1]

    if has_lat and window is not None:
        tq, back, wk = window
        start = pl.multiple_of(jnp.clip(pl.program_id(2) * tq - back, 0, n_lat - wk), LANE)
        key_rows = pl.ds(start, wk)
    else:
        key_rows = slice(None)
    sub = o_ref.shape[0] // n_sub
    for hh in range(hps):
        lanes = slice(hh * LANE, (hh + 1) * LANE)

        def cols(j):
            return lanes if per_head[j] else slice(None)

        kcs = [k[:, cols(j)] for j, k in enumerate(kc_refs)]
        vc = vc_ref[:, cols(n_parts)]
        if has_lat:
            kls = [k[key_rows, cols(j)] for j, k in enumerate(kl_refs)]
            vl = vl_ref[key_rows, cols(n_parts)]
        if has_sink:
            sink = sink_ref[pl.program_id(1) * hps + hh]
        for sb in range(n_sub):
            rows = slice(sb * sub, (sb + 1) * sub)
            qs = [q[rows, lanes] for q in q_refs]
            s2 = functools.reduce(jnp.add, [_dot_nt(q, k) for q, k in zip(qs, kcs)]) * scale
            m = jnp.max(s2, axis=-1, keepdims=True)
            if has_lat:
                s1 = functools.reduce(jnp.add, [_dot_nt(q, k) for q, k in zip(qs, kls)]) * scale
                if has_bias:
                    s1 = s1 + bias_ref[0, hh if bias_per_head else 0, rows, :]
                m = jnp.maximum(m, jnp.max(s1, axis=-1, keepdims=True))
            if has_sink:
                m = jnp.maximum(m, sink)
            p2 = jnp.exp(s2 - m)
            l = jnp.sum(p2, axis=-1, keepdims=True)
            o = _dot(p2.astype(BF16), vc)
            if has_lat:
                p1 = jnp.exp(s1 - m)
                l = l + jnp.sum(p1, axis=-1, keepdims=True)
                o = o + _dot(p1.astype(BF16), vl)
            if has_sink:
                l = l + jnp.exp(sink - m)
            o_ref[rows, lanes] = (o / l).astype(o_ref.dtype)


def attention(q_srcs, k_srcs, v_src, *, heads, scale, bsz, seq, nctx, t_lat, mode, out_rows=None, out_prev=None,
              window=None, bias=None, sink=None, heads_per_step=1, name="attention"):
    lat = mode == "lat"
    hps = heads_per_step
    n_parts = len(q_srcs)
    tq = (window[0] if window is not None else DENSE_TQ) if lat else nctx
    nqb = (seq if lat else nctx) // tq
    ctx0 = t_lat // nctx
    kv_srcs = list(k_srcs) + [v_src]
    per_head = tuple(grp == 1 for _, _, grp in kv_srcs)
    assert heads % hps == 0 and all(first % hps == 0 for _, first in q_srcs)
    assert hps == 1 or all(grp == 0 or (grp == 1 and first % hps == 0) for _, first, grp in kv_srcs)

    def kv_spec(rows, row_blk, first, grp):
        if hps > 1 and grp == 1:
            return pl.BlockSpec((rows, hps * LANE), lambda bi, hg, i: (row_blk(bi), first // hps + hg))
        return pl.BlockSpec((rows, LANE), lambda bi, hg, i: (row_blk(bi), first + (hg // grp if grp else 0)))

    def q_row(bi, i):
        return bi * nqb + i if lat else t_lat // tq + bi

    in_specs, args = [], []
    for arr, first in q_srcs:
        in_specs.append(pl.BlockSpec((tq, hps * LANE), lambda bi, hg, i, f=first: (q_row(bi, i), f // hps + hg)))
        args.append(arr)
    if lat:
        for arr, first, grp in kv_srcs:
            in_specs.append(kv_spec(seq, lambda bi: bi, first, grp))
            args.append(arr)
    for arr, first, grp in kv_srcs:
        in_specs.append(kv_spec(nctx, lambda bi: ctx0 + bi, first, grp))
        args.append(arr)
    bias_per_head = bias is not None and bias.shape[1] != 1
    if bias is not None:
        in_specs.append(pl.BlockSpec(
            (1, hps if bias_per_head else 1, tq, window[2]),
            lambda bi, hg, i: (jnp.where(i == 0, 0, jnp.where(i == nqb - 1, 2, 1)), hg if bias_per_head else 0, 0, 0)))
        args.append(bias)
    if sink is not None:
        in_specs.append(pl.BlockSpec(memory_space=pltpu.SMEM))
        args.append(sink.astype(F32))
    aliases = {}
    if out_prev is not None:
        aliases = {len(args): 0}
        in_specs.append(pl.BlockSpec(memory_space=pl.ANY))
        args.append(out_prev)
        out_rows = out_prev.shape[0]
    if window is not None:
        assert seq % tq == 0 and nqb >= 2 and seq >= window[2]
    return pl.pallas_call(
        functools.partial(_attn_kernel, n_parts=n_parts, has_lat=lat, window=window, has_bias=bias is not None,
                          bias_per_head=bias_per_head, has_sink=sink is not None, scale=scale, n_lat=seq,
                          n_sub=DENSE_SUB if (lat and window is None) else 1, hps=hps, per_head=per_head),
        grid=(bsz, heads // hps, nqb),
        in_specs=in_specs,
        out_specs=pl.BlockSpec((tq, hps * LANE), lambda bi, hg, i: (q_row(bi, i), hg)),
        out_shape=jax.ShapeDtypeStruct((out_rows, heads * LANE), BF16),
        input_output_aliases=aliases,
        compiler_params=_cparams(("parallel", "parallel", "arbitrary")),
        name=name,
    )(*args)


def _window_q_offsets(geom):
    tq, back, wk = geom
    return (0, back, wk - tq)


def band_bias_table():
    tq, _, wk = (B_TQ, B_BACK, B_WK)
    kpos = np.arange(wk)[None, :]
    tabs = []
    for off in _window_q_offsets((B_TQ, B_BACK, B_WK)):
        qpos = off + np.arange(tq)[:, None]
        tabs.append(np.where(np.abs(qpos - kpos) <= B_WINDOW, 0.0, NEG_INF))
    return jnp.asarray(np.stack(tabs)[:, None], F32)


def neighbourhood_bias_table(rpb, rows):
    tq, back, wk = (C_TQ, C_BACK, C_WK)
    assert rows >= wk // GRID_W and rows >= NA_ROWS
    nqb = rows * GRID_W // tq
    nq_rows, nk_rows = tq // GRID_W, wk // GRID_W
    cols = np.arange(GRID_W)
    col_start = np.clip(cols - NA_COLS // 2, 0, GRID_W - NA_COLS)
    col_ok = (cols[None, :] >= col_start[:, None]) & (cols[None, :] < col_start[:, None] + NA_COLS)
    col_off = np.clip(cols[None, :] - cols[:, None] + (NA_COLS - 1), 0, 2 * NA_COLS - 2)
    onehot_c = jnp.asarray(col_off[..., None] == np.arange(2 * NA_COLS - 1), F32)
    tabs = []
    for cls, off in enumerate(_window_q_offsets((C_TQ, C_BACK, C_WK))):
        blk = (0, 1, nqb - 1)[cls]
        w0 = int(np.clip(blk * tq - back, 0, rows * GRID_W - wk)) // GRID_W
        assert w0 * GRID_W + off == blk * tq and off % GRID_W == 0
        q_row = w0 + off // GRID_W + np.arange(nq_rows)
        k_row = w0 + np.arange(nk_rows)
        row_start = np.clip(q_row - NA_ROWS // 2, 0, rows - NA_ROWS)
        row_ok = (k_row[None, :] >= row_start[:, None]) & (k_row[None, :] < row_start[:, None] + NA_ROWS)
        row_off = np.clip(k_row[None, :] - q_row[:, None] + (NA_ROWS - 1), 0, 2 * NA_ROWS - 2)
        by_row = rpb[:, row_off, :].astype(F32)
        vals = jnp.einsum("hrsc,qkc->hrqsk", by_row, onehot_c, precision=lax.Precision.HIGHEST)
        ok = row_ok[:, None, :, None] & col_ok[None, :, None, :]
        tabs.append(jnp.where(jnp.asarray(ok)[None], vals, NEG_INF).reshape(rpb.shape[0], tq, wk))
    return jnp.stack(tabs)


def _merge_out_ln_kernel(*refs, alpha, n_branch):
    o_refs = refs[:n_branch]
    wbr_refs = refs[n_branch:2 * n_branch]
    g_ref, wout_ref, x_ref, gate_ref, lng_ref, lnb_ref, out_ref = refs[2 * n_branch:]
    d = x_ref.shape[1]
    y = None
    for r, (o_ref, wbr_ref) in enumerate(zip(o_refs, wbr_refs)):
        t = g_ref[:, r * d:(r + 1) * d].astype(F32) * _dot(o_ref[...], wbr_ref[...])
        y = t if y is None else y + t
    mix = _dot(y.astype(BF16), wout_ref[...])
    z = alpha * x_ref[...] + gate_ref[0] * mix
    mu = jnp.mean(z, axis=-1, keepdims=True)
    zc = z - mu
    var = jnp.mean(zc * zc, axis=-1, keepdims=True)
    out_ref[...] = (zc * lax.rsqrt(var + LN_EPS)) * lng_ref[...] + lnb_ref[...]


def merge_out_ln(branches, w_branch, gates, w_out, x, mod, gate_slot, group_of, ln_g, ln_b, alpha):
    rows = gates.shape[0]
    nbr = len(branches)
    d = w_out.shape[0]
    tm = _pick(rows, (256,))
    tm_group = group_of(tm)
    resident = pl.Buffered(1)
    return pl.pallas_call(
        functools.partial(_merge_out_ln_kernel, alpha=alpha, n_branch=nbr),
        grid=(rows // tm,),
        in_specs=[pl.BlockSpec((tm, o.shape[1]), lambda i: (i, 0)) for o in branches] + [
                  pl.BlockSpec(w.shape, lambda i: (0, 0), pipeline_mode=resident) for w in w_branch] + [
                  pl.BlockSpec((tm, nbr * d), lambda i: (i, 0)),
                  pl.BlockSpec((d, d), lambda i: (0, 0), pipeline_mode=resident),
                  pl.BlockSpec((tm, d), lambda i: (i, 0)),
                  pl.BlockSpec((1, 1, d), lambda i: (tm_group(i) * 6 + gate_slot, 0, 0)),
                  pl.BlockSpec((1, d), lambda i: (0, 0)),
                  pl.BlockSpec((1, d), lambda i: (0, 0))],
        out_specs=pl.BlockSpec((tm, d), lambda i: (i, 0)),
        out_shape=jax.ShapeDtypeStruct((rows, d), F32),
        compiler_params=_cparams(("parallel",)),
        name="merge_out_ln",
    )(*branches, *w_branch, gates, w_out, x, mod, ln_g.reshape(1, d), ln_b.reshape(1, d))


def _peer_pairs():
    return [(a, b) for a in range(PEER_TOPK) for b in range(PEER_TOPK) if (a + 1) * (b + 1) <= PEER_TOPK]


def _peer_topk_kernel(q_ref, keys_ref, cnt_ref, e0_ref, rank_ref, e1_ref, s0_scr):
    tops = [[[None] * PEER_HEADS for _ in range(PEER_TOPK)] for _ in range(2)]
    for h in range(PEER_HEADS):
        for p in range(2):
            hp = 2 * h + p
            qs = q_ref[:, hp * PEER_KEYS:(hp + 1) * PEER_KEYS]
            s = _dot_nt(keys_ref[hp], qs)
            work = s
            rank = jnp.full(s.shape, float(PEER_TOPK), F32)
            for r in range(PEER_TOPK):
                m = jnp.max(work, axis=0, keepdims=True)
                tops[p][r][h] = m
                if p == 1:
                    hit = work == m
                    rank = jnp.where(hit, float(r), rank)
                    work = jnp.where(hit, -jnp.inf, work)
                elif r + 1 < PEER_TOPK:
                    work = jnp.where(work == m, -jnp.inf, work)
            if p == 0:
                s0_scr[h] = s
            else:
                rank_ref[h, 0] = rank.astype(BF16)
                e1_ref[h, 0] = jnp.exp(s - tops[1][0][h]).astype(BF16)
    top = [[jnp.concatenate(tops[p][r], axis=0) for r in range(PEER_TOPK)] for p in range(2)]
    cands = [top[0][a] + top[1][b] for a, b in _peer_pairs()]
    work = list(cands)
    for r in range(PEER_TOPK):
        m = functools.reduce(jnp.maximum, work)
        if r + 1 < PEER_TOPK:
            work = [jnp.where(w == m, -jnp.inf, w) for w in work]
    tau = m
    cmax = cands[0]
    z = functools.reduce(jnp.add, [jnp.where(c >= tau, jnp.exp(c - cmax), 0.0) for c in cands])
    rz = 1.0 / z
    for h in range(PEER_HEADS):
        s0 = s0_scr[h]
        tau_h = tau[h:h + 1]
        cnt = None
        for b in range(PEER_TOPK):
            inc = jnp.where(s0 + top[1][b][h:h + 1] >= tau_h, 1.0, 0.0)
            cnt = inc if cnt is None else cnt + inc
        cnt_ref[h, 0] = cnt
        e0_ref[h, 0] = jnp.exp(s0 - top[0][0][h:h + 1]) * rz[h:h + 1]


def peer_topk(q, keys):
    rows = q.shape[0]
    tt = PEER_TOPK_TILE
    per_chunk = PEER_CHUNK // tt
    nch = rows // PEER_CHUNK
    big = pl.BlockSpec((PEER_HEADS, 1, PEER_KEYS, tt), lambda i: (0, i // per_chunk, 0, i % per_chunk))
    shape_f32 = jax.ShapeDtypeStruct((PEER_HEADS, nch, PEER_KEYS, PEER_CHUNK), F32)
    shape_bf16 = jax.ShapeDtypeStruct((PEER_HEADS, nch, PEER_KEYS, PEER_CHUNK), BF16)
    return pl.pallas_call(
        _peer_topk_kernel,
        grid=(rows // tt,),
        in_specs=[pl.BlockSpec((tt, q.shape[1]), lambda i: (i, 0)),
                  pl.BlockSpec(keys.shape, lambda i: (0, 0, 0))],
        out_specs=[big, big, big, big],
        out_shape=[shape_f32, shape_f32, shape_bf16, shape_bf16],
        scratch_shapes=[pltpu.VMEM((PEER_HEADS, PEER_KEYS, tt), F32)],
        compiler_params=_cparams(("parallel",)),
        name="peer_topk",
    )(q, keys)


def _modulate_t_kernel(x_ref, sh_ref, sc_ref, o_ref):
    h = x_ref[...] * (1.0 + sc_ref[0]) + sh_ref[0]
    o_ref[0] = h.T.astype(BF16)


def modulate_transposed(x, mod, mod_idx, group_of):
    rows, d = x.shape
    tt = PEER_CHUNK
    tt_group = group_of(tt)
    mod_specs = [pl.BlockSpec((1, 1, d), lambda i, s=slot: (tt_group(i) * 6 + s, 0, 0)) for slot in mod_idx]
    return pl.pallas_call(
        _modulate_t_kernel,
        grid=(rows // tt,),
        in_specs=[pl.BlockSpec((tt, d), lambda i: (i, 0))] + mod_specs,
        out_specs=pl.BlockSpec((1, d, tt), lambda i: (i, 0, 0)),
        out_shape=jax.ShapeDtypeStruct((rows // tt, d, tt), BF16),
        compiler_params=_cparams(("parallel",)),
        name="modulate_transposed",
    )(x, mod, mod)


def _peer_dense_kernel(ht_ref, u_ref, vt_ref, cnt_ref, e0_ref, rank_ref, e1_ref, o_ref, a_scr, w_scr, p_scr, *, n_e):
    e = pl.program_id(1)
    cur = e % 2
    prev = 1 - cur
    i_per_eb = PEER_EB // PEER_KEYS
    tc = w_scr.shape[-1]

    def gate_and_down():
        for ii in range(i_per_eb):
            sl = slice(ii * PEER_KEYS, (ii + 1) * PEER_KEYS)
            a = a_scr[prev, sl, :]
            act = 0.5 * a * (1.0 + lax.erf(a * (1.0 / math.sqrt(2.0))))
            p_scr[sl, :] = act.astype(BF16) * w_scr[prev, sl, :]
        o_ref[0] += _dot(vt_ref[0], p_scr[...])

    def up_and_weights():
        a_scr[cur] = _dot(u_ref[...], ht_ref[0])
        for ii in range(i_per_eb):
            i = e * i_per_eb + ii
            cnt = [jnp.broadcast_to(cnt_ref[h, 0, pl.ds(i, 1), :], (BF16_ROWS, tc)).astype(BF16)
                   for h in range(PEER_HEADS)]
            e0 = [jnp.broadcast_to(e0_ref[h, 0, pl.ds(i, 1), :], (BF16_ROWS, tc)).astype(BF16)
                  for h in range(PEER_HEADS)]
            for g in range(PEER_KEYS // BF16_ROWS):
                rows = slice(g * BF16_ROWS, (g + 1) * BF16_ROWS)
                w = None
                for h in range(PEER_HEADS):
                    e1 = e1_ref[h, 0, rows, :]
                    t = jnp.where(rank_ref[h, 0, rows, :] < cnt[h], e1, jnp.zeros_like(e1)) * e0[h]
                    w = t if w is None else w + t
                w_scr[cur, ii * PEER_KEYS + g * BF16_ROWS:ii * PEER_KEYS + (g + 1) * BF16_ROWS, :] = w

    @pl.when(e == 0)
    def _():
        o_ref[...] = jnp.zeros_like(o_ref)
        up_and_weights()

    @pl.when(jnp.logical_and(e > 0, e < n_e))
    def _():
        gate_and_down()
        up_and_weights()

    @pl.when(e == n_e)
    def _():
        gate_and_down()


def peer_dense(ht, u, vt, cnt, e0, rank, e1):
    nch, d, tc = ht.shape
    n_e = u.shape[0] // PEER_EB
    sel = pl.BlockSpec((PEER_HEADS, 1, PEER_KEYS, tc), lambda t, e: (0, t, 0, 0))
    return pl.pallas_call(
        functools.partial(_peer_dense_kernel, n_e=n_e),
        grid=(nch, n_e + 1),
        in_specs=[pl.BlockSpec((1, d, tc), lambda t, e: (t, 0, 0)),
                  pl.BlockSpec((PEER_EB, d), lambda t, e: (jnp.minimum(e, n_e - 1), 0)),
                  pl.BlockSpec((1, d, PEER_EB), lambda t, e: (jnp.maximum(e - 1, 0), 0, 0)),
                  sel, sel, sel, sel],
        out_specs=pl.BlockSpec((1, d, tc), lambda t, e: (t, 0, 0)),
        out_shape=jax.ShapeDtypeStruct((nch, d, tc), F32),
        scratch_shapes=[pltpu.VMEM((2, PEER_EB, tc), F32), pltpu.VMEM((2, PEER_EB, tc), BF16),
                        pltpu.VMEM((PEER_EB, tc), BF16)],
        compiler_params=_cparams(("parallel", "arbitrary")),
        name="peer_dense",
    )(ht, u, vt, cnt, e0, rank, e1)


def _axial_angles(n, rot_dim):
    t = jnp.arange(n)
    rows = (t // GRID_W).astype(F32)
    cols = (t % GRID_W).astype(F32)
    nf = rot_dim // 4
    inv = ROPE_THETA ** (-jnp.arange(nf, dtype=F32) / nf)
    ang = jnp.concatenate([rows[:, None] * inv, cols[:, None] * inv], axis=-1)
    return jnp.cos(ang), jnp.sin(ang)


def _rope_tables(seq):
    assert A_ROPE == B_DIM == LANE // 2 and D_DIM == LANE

    def pad(t, fill):
        return jnp.concatenate([t, jnp.full((ROPE_TM, LANE), fill, F32)], axis=0)

    cos, sin = _axial_angles(seq, B_DIM)
    z = jnp.zeros_like(sin)
    c32 = jnp.concatenate([cos, cos, cos, cos], axis=1)
    sa = jnp.concatenate([z, sin, z, sin], axis=1)
    sb = jnp.concatenate([-sin, z, -sin, z], axis=1)
    cos, sin = _axial_angles(seq, D_DIM)
    c64 = jnp.concatenate([cos, cos], axis=1)
    s64 = jnp.concatenate([-sin, sin], axis=1)
    return [pad(c32, 1.0), pad(sa, 0.0), pad(sb, 0.0), pad(c64, 1.0), pad(s64, 0.0)]


def _half_block(t, first_half, axis):
    z = jnp.zeros_like(t)
    return jnp.concatenate([t, z] if first_half else [z, t], axis=axis)


def _in_proj_weight(w_in):
    d = w_in.shape[0]
    o = np.cumsum((0,) + IN_SIZES)
    part = [w_in[:, o[i]:o[i + 1]] for i in range(len(IN_SIZES))]
    per_kv = B_HEADS // B_KV_HEADS
    bq = [_half_block(part[3][:, h * B_DIM:(h + 1) * B_DIM], h // per_kv == 0, 1) for h in range(B_HEADS)]
    cols = bq + [_half_block(part[2], True, 1), part[4], part[9], part[10], part[6], part[7], part[8], part[1],
                 part[5], part[0], part[11], jnp.zeros((d, LANE), w_in.dtype)]
    out = jnp.concatenate(cols, axis=1).astype(BF16)
    assert out.shape[1] == U_BLOCKS * LANE
    return out


def _mla_up_weights(w_uq, w_ukv):
    dq, dkv = A_NOPE + A_ROPE, A_NOPE + A_V
    q_nope = [w_uq[:, h * dq:h * dq + A_NOPE] for h in range(A_HEADS)]
    q_rope = [_half_block(w_uq[:, h * dq + A_NOPE:(h + 1) * dq], True, 1) for h in range(A_HEADS)]
    k_nope = [w_ukv[:, h * dkv:h * dkv + A_NOPE] for h in range(A_HEADS)]
    v = [w_ukv[:, h * dkv + A_NOPE:(h + 1) * dkv] for h in range(A_HEADS)]
    return (jnp.concatenate(q_nope + q_rope, axis=1).astype(BF16), jnp.concatenate(k_nope + v, axis=1).astype(BF16))


def _branch_weights(w_branch):
    per_kv = B_HEADS // B_KV_HEADS
    wb = jnp.concatenate([_half_block(w_branch[1][h * B_DIM:(h + 1) * B_DIM], h // per_kv == 0, 0)
                          for h in range(B_HEADS)], axis=0)
    return [w_branch[0].astype(BF16), wb.astype(BF16), w_branch[2].astype(BF16), w_branch[3].astype(BF16)]


def kernel(x, c, ctx, c_ctx, w_ada, b_ada, w_in, a_q_norm, a_kv_norm, a_w_uq, a_w_ukv, b_sink, c_rpb, d_q_norm, d_k_norm, w_branch, w_gate, b_gate, w_out, ln1_g, ln1_b, ln2_g, ln2_b, peer_w_q, peer_sub_keys, peer_u, peer_v):
    bsz, seq, d = x.shape
    nctx = ctx.shape[1]
    depth = w_in.shape[0]
    t_lat, t_ctx = bsz * seq, bsz * nctx
    t_all = t_lat + t_ctx
    rows_grid = seq // GRID_W
    alpha = (2 * depth) ** 0.25

    tables = _rope_tables(seq)
    band_bias = band_bias_table()
    dims = dict(bsz=bsz, seq=seq, nctx=nctx, t_lat=t_lat)

    n_groups = -(-(bsz + 1) // 8) * 8
    cc = jnp.zeros((n_groups, d), F32).at[:bsz].set(c).at[bsz].set(c_ctx)

    def group_of(tm):
        lat_blocks, per_batch = t_lat // tm, seq // tm
        assert t_lat % tm == 0 and seq % tm == 0
        return lambda i: jnp.where(i < lat_blocks, i // per_batch, bsz)

    xs = jnp.concatenate([x.reshape(t_lat, d), ctx.reshape(t_ctx, d)], axis=0)

    for l in range(depth):
        need_ctx = l < depth - 1
        rows_out = t_all if need_ctx else t_lat
        mod = ada_modulation(cc, w_ada[l], b_ada[l]).reshape(n_groups * 6, 1, d)

        u = token_matmul(xs, _in_proj_weight(w_in[l]), prologue="modulate", mod=mod, mod_idx=(0, 1),
                         group_of=group_of, out_dtype=BF16, name="in_proj")
        w_gate_l = jnp.transpose(w_gate[l], (1, 0, 2)).reshape(d, N_BRANCH * d).astype(BF16)
        gates = token_matmul(xs, w_gate_l, rows=rows_out, prologue="modulate", mod=mod, mod_idx=(0, 1),
                             group_of=group_of, bias=b_gate[l].reshape(-1), act="sigmoid", out_dtype=BF16,
                             name="branch_gates")

        w_uq, w_ukv = _mla_up_weights(a_w_uq[l], a_w_ukv[l])
        qa = token_matmul(u, w_uq, x_cols=(A_Q_LORA, U_CQ * LANE // A_Q_LORA), prologue="rmsnorm", norm_g=a_q_norm[l], out_dtype=BF16,
                          name="mla_q_up")
        kva = token_matmul(u, w_ukv, x_cols=(A_KV_LORA, U_CKV), prologue="rmsnorm", norm_g=a_kv_norm[l],
                           out_dtype=BF16, name="mla_kv_up")
        ur = qk_prep(u, 0, U_PREP_N, U_PREP_R32, U_PREP_RMSQ, tables, d_q_norm[l], d_k_norm[l], t_lat, seq, "qk_prep")
        qr = qk_prep(qa, 1, A_HEADS, A_HEADS, 0, tables, d_q_norm[l], d_k_norm[l], t_lat, seq, "mla_q_rope")
        mixers = {
            "mla": dict(q_srcs=[(qa, 0), (qr, 0)], k_srcs=[(kva, 0, 1), (ur, R_KR, 0)], v_src=(kva, A_HEADS, 1),
                        heads=A_HEADS, scale=A_SCALE),
            "swa": dict(q_srcs=[(ur, R_BQ)], k_srcs=[(ur, R_BK, 0)], v_src=(u, U_BV, 0), heads=B_HEADS,
                        scale=B_SCALE, sink=b_sink[l], heads_per_step=B_HEADS),
            "na": dict(q_srcs=[(u, U_CQ_)], k_srcs=[(u, U_CK, 1)], v_src=(u, U_CV, 1), heads=C_HEADS,
                       scale=C_SCALE, heads_per_step=C_HEADS),
            "gqa": dict(q_srcs=[(ur, R_DQ)], k_srcs=[(ur, R_DK, D_HEADS // D_KV_HEADS)],
                        v_src=(u, U_DV, D_HEADS // D_KV_HEADS), heads=D_HEADS, scale=D_SCALE),
        }
        windows = {"swa": dict(window=(B_TQ, B_BACK, B_WK), bias=band_bias),
                   "na": dict(window=(C_TQ, C_BACK, C_WK), bias=neighbourhood_bias_table(c_rpb[l], rows_grid))}
        branches = []
        for mixer, cfg in mixers.items():
            o = attention(mode="lat", out_rows=rows_out, name=mixer + "_attention", **cfg, **windows.get(mixer, {}),
                          **dims)
            if need_ctx:
                o = attention(mode="ctx", out_prev=o, name=mixer + "_attention_ctx", **cfg, **dims)
            branches.append(o)

        xs = merge_out_ln(branches, _branch_weights(w_branch[l]), gates, w_out[l].astype(BF16), xs, mod, 2, group_of,
                          ln1_g[l], ln1_b[l], alpha)

        pq = token_matmul(xs, peer_w_q[l].astype(BF16), prologue="modulate", mod=mod, mod_idx=(3, 4),
                          group_of=group_of, out_dtype=BF16, name="peer_query")
        keys = peer_sub_keys[l].reshape(2 * PEER_HEADS, PEER_KEYS, PEER_QDIM // 2).astype(BF16)
        cnt, e0, rank, e1 = peer_topk(pq, keys)
        ht = modulate_transposed(xs, mod, (3, 4), group_of)
        n_exp = peer_u.shape[1]
        vt = jnp.transpose(peer_v[l].astype(BF16).reshape(n_exp // PEER_EB, PEER_EB, d), (0, 2, 1))
        ffn_t = peer_dense(ht, peer_u[l].astype(BF16), vt, cnt, e0, rank, e1)
        xs = residual_ln(xs, ffn_t, mod, 5, group_of, ln2_g[l], ln2_b[l], alpha)

    return xs[:t_lat].reshape(bsz, seq, d)
```

```python
import functools
import math

import jax
import jax.numpy as jnp
import numpy as np
from jax import lax
from jax.experimental import pallas as pl
from jax.experimental.pallas import tpu as pltpu

F32 = jnp.float32
BF16 = jnp.bfloat16

GRID_W = 64
ROPE_THETA = 10000.0
LN_EPS = 1e-5
RMS_EPS = 1e-6
NEG_INF = -1e30

A_HEADS, A_NOPE, A_ROPE, A_V, A_Q_LORA, A_KV_LORA = 4, 128, 64, 128, 384, 128
B_HEADS, B_KV_HEADS, B_DIM, B_WINDOW = 8, 2, 64, 128
C_HEADS, C_DIM, NA_ROWS, NA_COLS = 4, 128, 8, 16
D_HEADS, D_KV_HEADS, D_DIM = 4, 2, 128
N_BRANCH, BRANCH_W = 4, 512
PEER_HEADS, PEER_KEYS, PEER_QDIM, PEER_TOPK = 8, 128, 256, 16

A_SCALE = (A_NOPE + A_ROPE) ** -0.5
B_SCALE = B_DIM ** -0.5
C_SCALE = C_DIM ** -0.5
D_SCALE = D_DIM ** -0.5

IN_SIZES = (A_Q_LORA, A_KV_LORA, A_ROPE,
            B_HEADS * B_DIM, B_KV_HEADS * B_DIM, B_KV_HEADS * B_DIM,
            C_HEADS * C_DIM, C_HEADS * C_DIM, C_HEADS * C_DIM,
            D_HEADS * D_DIM, D_KV_HEADS * D_DIM, D_KV_HEADS * D_DIM)
LANE = 128
BF16_ROWS = 16

VMEM_LIMIT = 56 * 1024 * 1024

PEER_CHUNK = 512
PEER_EB = 1024
PEER_TOPK_TILE = 256

B_TQ, B_BACK, B_WK = 512, 128, 768
C_TQ, C_BACK, C_WK = 4 * GRID_W, 4 * GRID_W, 12 * GRID_W
DENSE_TQ = 1024
DENSE_SUB = 4
ROPE_TM = 512

R_BQ, R_KR, R_BK, R_DQ, R_DK = 0, 8, 9, 10, 14
U_PREP_N, U_PREP_R32, U_PREP_RMSQ = 16, 10, 4
U_CQ_, U_CK, U_CV, U_CKV, U_BV, U_CQ, U_DV, U_BLOCKS = 16, 20, 24, 28, 29, 30, 33, 36


def _cparams(sem):
    return pltpu.CompilerParams(dimension_semantics=sem, vmem_limit_bytes=VMEM_LIMIT)


def _pick(n, cands):
    for c in cands:
        if n % c == 0:
            return c
    raise ValueError(f"no tile in {cands} divides {n}")


def _dot(a, b):
    return jnp.dot(a, b, preferred_element_type=F32)


def _dot_nt(a, b):
    return lax.dot_general(a, b, (((1,), (1,)), ((), ())), preferred_element_type=F32)


def _ada_kernel(c_ref, w_ref, b_ref, o_ref):
    c = c_ref[...]
    s = c * (1.0 / (1.0 + jnp.exp(-c)))
    w = w_ref[...]
    s_hi = s.astype(BF16)
    s_lo = (s - s_hi.astype(F32)).astype(BF16)
    w_hi = w.astype(BF16)
    w_lo = (w - w_hi.astype(F32)).astype(BF16)
    acc = _dot(s_hi, w_hi) + _dot(s_lo, w_hi) + _dot(s_hi, w_lo)
    o_ref[...] = acc + b_ref[...]


def ada_modulation(cc, w, b):
    r, k = cc.shape
    n = w.shape[1]
    tn = _pick(n, (512, 256, 128))
    return pl.pallas_call(
        _ada_kernel,
        grid=(n // tn,),
        in_specs=[pl.BlockSpec((r, k), lambda j: (0, 0)),
                  pl.BlockSpec((k, tn), lambda j: (0, j)),
                  pl.BlockSpec((1, tn), lambda j: (0, j))],
        out_specs=pl.BlockSpec((r, tn), lambda j: (0, j)),
        out_shape=jax.ShapeDtypeStruct((r, n), F32),
        compiler_params=_cparams(("arbitrary",)),
        name="ada_modulation",
    )(cc, w, b.reshape(1, n))


def _mm_kernel(*refs, prologue, has_bias, act):
    it = iter(refs)
    x_ref = next(it)
    if prologue == "modulate":
        sh_ref, sc_ref = next(it), next(it)
    elif prologue == "rmsnorm":
        g_ref = next(it)
    w_ref = next(it)
    b_ref = next(it) if has_bias else None
    o_ref = next(it)
    a_scr = next(it)

    @pl.when(pl.program_id(1) == 0)
    def _():
        x = x_ref[...].astype(F32)
        if prologue == "modulate":
            a = x * (1.0 + sc_ref[0]) + sh_ref[0]
        elif prologue == "rmsnorm":
            ms = jnp.mean(x * x, axis=-1, keepdims=True)
            a = (x * lax.rsqrt(ms + RMS_EPS)) * g_ref[...]
        else:
            a = x
        a_scr[...] = a.astype(BF16)

    acc = _dot(a_scr[...], w_ref[...])
    if has_bias:
        acc = acc + b_ref[...]
    if act == "sigmoid":
        acc = 1.0 / (1.0 + jnp.exp(-acc))
    o_ref[...] = acc.astype(o_ref.dtype)


def token_matmul(x, w, *, rows=None, x_cols=None, prologue=None, mod=None, mod_idx=None, group_of=None,
                 norm_g=None, bias=None, act=None, out_dtype=F32, name="token_matmul"):
    rows = x.shape[0] if rows is None else rows
    kdim, n = w.shape
    kw, kblk = (x.shape[1], 0) if x_cols is None else x_cols
    assert kw == kdim
    tm = _pick(rows, (512, 256))
    tn = _pick(n, (2048, 1536, 1024, 512, 256, 128))
    in_specs = [pl.BlockSpec((tm, kdim), lambda i, j: (i, kblk))]
    args = [x]
    if prologue == "modulate":
        tm_group = group_of(tm)
        for slot in mod_idx:
            in_specs.append(pl.BlockSpec((1, 1, kdim), lambda i, j, s=slot: (tm_group(i) * 6 + s, 0, 0)))
            args.append(mod)
    elif prologue == "rmsnorm":
        in_specs.append(pl.BlockSpec((1, kdim), lambda i, j: (0, 0)))
        args.append(norm_g.reshape(1, kdim))
    in_specs.append(pl.BlockSpec((kdim, tn), lambda i, j: (0, j)))
    args.append(w)
    if bias is not None:
        in_specs.append(pl.BlockSpec((1, tn), lambda i, j: (0, j)))
        args.append(bias.reshape(1, n))
    return pl.pallas_call(
        functools.partial(_mm_kernel, prologue=prologue, has_bias=bias is not None, act=act),
        grid=(rows // tm, n // tn),
        in_specs=in_specs,
        out_specs=pl.BlockSpec((tm, tn), lambda i, j: (i, j)),
        out_shape=jax.ShapeDtypeStruct((rows, n), out_dtype),
        scratch_shapes=[pltpu.VMEM((tm, kdim), BF16)],
        compiler_params=_cparams(("parallel", "arbitrary")),
        name=name,
    )(*args)


def _res_ln_kernel(x_ref, yt_ref, gate_ref, g_ref, b_ref, o_ref, *, alpha):
    z = alpha * x_ref[...] + gate_ref[0] * yt_ref[0].T
    mu = jnp.mean(z, axis=-1, keepdims=True)
    zc = z - mu
    var = jnp.mean(zc * zc, axis=-1, keepdims=True)
    o_ref[...] = (zc * lax.rsqrt(var + LN_EPS)) * g_ref[...] + b_ref[...]


def residual_ln(x, yt, mod, gate_slot, group_of, ln_g, ln_b, alpha):
    nch, d, tm = yt.shape
    rows = nch * tm
    tm_group = group_of(tm)
    return pl.pallas_call(
        functools.partial(_res_ln_kernel, alpha=alpha),
        grid=(nch,),
        in_specs=[pl.BlockSpec((tm, d), lambda i: (i, 0)),
                  pl.BlockSpec((1, d, tm), lambda i: (i, 0, 0)),
                  pl.BlockSpec((1, 1, d), lambda i: (tm_group(i) * 6 + gate_slot, 0, 0)),
                  pl.BlockSpec((1, d), lambda i: (0, 0)),
                  pl.BlockSpec((1, d), lambda i: (0, 0))],
        out_specs=pl.BlockSpec((tm, d), lambda i: (i, 0)),
        out_shape=jax.ShapeDtypeStruct((rows, d), F32),
        compiler_params=_cparams(("parallel",)),
        name="residual_ln",
    )(x, yt, mod, ln_g.reshape(1, d), ln_b.reshape(1, d))


def _qk_prep_kernel(x_ref, c32_ref, sa_ref, sb_ref, c64_ref, s64_ref, gq_ref, gk_ref, o_ref, *, n_r32, n_rms_q):
    for j in range(x_ref.shape[1] // LANE):
        cols = slice(j * LANE, (j + 1) * LANE)
        x = x_ref[:, cols].astype(F32)
        if j < n_r32:
            y = x * c32_ref[...] + pltpu.roll(x, 32, 1) * sa_ref[...] + pltpu.roll(x, 96, 1) * sb_ref[...]
        else:
            g = gq_ref[...] if j < n_r32 + n_rms_q else gk_ref[...]
            ms = jnp.mean(x * x, axis=-1, keepdims=True)
            y = (x * lax.rsqrt(ms + RMS_EPS)) * g
            y = y * c64_ref[...] + pltpu.roll(y, 64, 1) * s64_ref[...]
        o_ref[:, cols] = y.astype(o_ref.dtype)


def qk_prep(x, group_idx, n_blocks, n_r32, n_rms_q, tables, gq, gk, lat_rows, seq, name):
    rows = x.shape[0]
    tm = ROPE_TM
    width = n_blocks * LANE
    lat_blocks, per_seq = lat_rows // tm, seq // tm
    tab_spec = pl.BlockSpec((tm, LANE), lambda i: (jnp.where(i < lat_blocks, i % per_seq, per_seq), 0))
    vec_spec = pl.BlockSpec((1, LANE), lambda i: (0, 0))
    return pl.pallas_call(
        functools.partial(_qk_prep_kernel, n_r32=n_r32, n_rms_q=n_rms_q),
        grid=(rows // tm,),
        in_specs=[pl.BlockSpec((tm, width), lambda i: (i, group_idx))] + [tab_spec] * 5 + [vec_spec] * 2,
        out_specs=pl.BlockSpec((tm, width), lambda i: (i, 0)),
        out_shape=jax.ShapeDtypeStruct((rows, width), BF16),
        compiler_params=_cparams(("parallel",)),
        name=name,
    )(x, *tables, gq.reshape(1, LANE), gk.reshape(1, LANE))


def _attn_kernel(*refs, n_parts, has_lat, window, has_bias, bias_per_head, has_sink, scale, n_lat, n_sub, hps,
                 per_head):
    it = iter(refs)
    q_refs = [next(it) for _ in range(n_parts)]
    if has_lat:
        kl_refs = [next(it) for _ in range(n_parts)]
        vl_ref = next(it)
    kc_refs = [next(it) for _ in range(n_parts)]
    vc_ref = next(it)
    bias_ref = next(it) if has_bias else None
    sink_ref = next(it) if has_sink else None
    refs = list(it)
    o_ref = refs[-1]

    if has_lat and window is not None:
        tq, back, wk = window
        start = pl.multiple_of(jnp.clip(pl.program_id(2) * tq - back, 0, n_lat - wk), LANE)
        key_rows = pl.ds(start, wk)
    else:
        key_rows = slice(None)
    sub = o_ref.shape[0] // n_sub
    for hh in range(hps):
        lanes = slice(hh * LANE, (hh + 1) * LANE)

        def cols(j):
            return lanes if per_head[j] else slice(None)

        kcs = [k[:, cols(j)] for j, k in enumerate(kc_refs)]
        vc = vc_ref[:, cols(n_parts)]
        if has_lat:
            kls = [k[key_rows, cols(j)] for j, k in enumerate(kl_refs)]
            vl = vl_ref[key_rows, cols(n_parts)]
        if has_sink:
            sink = sink_ref[pl.program_id(1) * hps + hh]
        for sb in range(n_sub):
            rows = slice(sb * sub, (sb + 1) * sub)
            qs = [q[rows, lanes] for q in q_refs]
            s2 = functools.reduce(jnp.add, [_dot_nt(q, k) for q, k in zip(qs, kcs)]) * scale
            m = jnp.max(s2, axis=-1, keepdims=True)
            if has_lat:
                s1 = functools.reduce(jnp.add, [_dot_nt(q, k) for q, k in zip(qs, kls)]) * scale
                if has_bias:
                    s1 = s1 + bias_ref[0, hh if bias_per_head else 0, rows, :]
                m = jnp.maximum(m, jnp.max(s1, axis=-1, keepdims=True))
            if has_sink:
                m = jnp.maximum(m, sink)
            p2 = jnp.exp(s2 - m)
            l = jnp.sum(p2, axis=-1, keepdims=True)
            o = _dot(p2.astype(BF16), vc)
            if has_lat:
                p1 = jnp.exp(s1 - m)
                l = l + jnp.sum(p1, axis=-1, keepdims=True)
                o = o + _dot(p1.astype(BF16), vl)
            if has_sink:
                l = l + jnp.exp(sink - m)
            o_ref[rows, lanes] = (o / l).astype(o_ref.dtype)


def attention(q_srcs, k_srcs, v_src, *, heads, scale, bsz, seq, nctx, t_lat, mode, out_rows=None, out_prev=None,
              window=None, bias=None, sink=None, heads_per_step=1, name="attention"):
    lat = mode == "lat"
    hps = heads_per_step
    n_parts = len(q_srcs)
    tq = (window[0] if window is not None else DENSE_TQ) if lat else nctx
    nqb = (seq if lat else nctx) // tq
    ctx0 = t_lat // nctx
    kv_srcs = list(k_srcs) + [v_src]
    per_head = tuple(grp == 1 for _, _, grp in kv_srcs)
    assert heads % hps == 0 and all(first % hps == 0 for _, first in q_srcs)
    assert hps == 1 or all(grp == 0 or (grp == 1 and first % hps == 0) for _, first, grp in kv_srcs)

    def kv_spec(rows, row_blk, first, grp):
        if hps > 1 and grp == 1:
            return pl.BlockSpec((rows, hps * LANE), lambda bi, hg, i: (row_blk(bi), first // hps + hg))
        return pl.BlockSpec((rows, LANE), lambda bi, hg, i: (row_blk(bi), first + (hg // grp if grp else 0)))

    def q_row(bi, i):
        return bi * nqb + i if lat else t_lat // tq + bi

    in_specs, args = [], []
    for arr, first in q_srcs:
        in_specs.append(pl.BlockSpec((tq, hps * LANE), lambda bi, hg, i, f=first: (q_row(bi, i), f // hps + hg)))
        args.append(arr)
    if lat:
        for arr, first, grp in kv_srcs:
            in_specs.append(kv_spec(seq, lambda bi: bi, first, grp))
            args.append(arr)
    for arr, first, grp in kv_srcs:
        in_specs.append(kv_spec(nctx, lambda bi: ctx0 + bi, first, grp))
        args.append(arr)
    bias_per_head = bias is not None and bias.shape[1] != 1
    if bias is not None:
        in_specs.append(pl.BlockSpec(
            (1, hps if bias_per_head else 1, tq, window[2]),
            lambda bi, hg, i: (jnp.where(i == 0, 0, jnp.where(i == nqb - 1, 2, 1)), hg if bias_per_head else 0, 0, 0)))
        args.append(bias)
    if sink is not None:
        in_specs.append(pl.BlockSpec(memory_space=pltpu.SMEM))
        args.append(sink.astype(F32))
    aliases = {}
    if out_prev is not None:
        aliases = {len(args): 0}
        in_specs.append(pl.BlockSpec(memory_space=pl.ANY))
        args.append(out_prev)
        out_rows = out_prev.shape[0]
    if window is not None:
        assert seq % tq == 0 and nqb >= 2 and seq >= window[2]
    return pl.pallas_call(
        functools.partial(_attn_kernel, n_parts=n_parts, has_lat=lat, window=window, has_bias=bias is not None,
                          bias_per_head=bias_per_head, has_sink=sink is not None, scale=scale, n_lat=seq,
                          n_sub=DENSE_SUB if (lat and window is None) else 1, hps=hps, per_head=per_head),
        grid=(bsz, heads // hps, nqb),
        in_specs=in_specs,
        out_specs=pl.BlockSpec((tq, hps * LANE), lambda bi, hg, i: (q_row(bi, i), hg)),
        out_shape=jax.ShapeDtypeStruct((out_rows, heads * LANE), BF16),
        input_output_aliases=aliases,
        compiler_params=_cparams(("parallel", "parallel", "arbitrary")),
        name=name,
    )(*args)


def _window_q_offsets(geom):
    tq, back, wk = geom
    return (0, back, wk - tq)


def band_bias_table():
    tq, _, wk = (B_TQ, B_BACK, B_WK)
    kpos = np.arange(wk)[None, :]
    tabs = []
    for off in _window_q_offsets((B_TQ, B_BACK, B_WK)):
        qpos = off + np.arange(tq)[:, None]
        tabs.append(np.where(np.abs(qpos - kpos) <= B_WINDOW, 0.0, NEG_INF))
    return jnp.asarray(np.stack(tabs)[:, None], F32)


def neighbourhood_bias_table(rpb, rows):
    tq, back, wk = (C_TQ, C_BACK, C_WK)
    assert rows >= wk // GRID_W and rows >= NA_ROWS
    nqb = rows * GRID_W // tq
    nq_rows, nk_rows = tq // GRID_W, wk // GRID_W
    cols = np.arange(GRID_W)
    col_start = np.clip(cols - NA_COLS // 2, 0, GRID_W - NA_COLS)
    col_ok = (cols[None, :] >= col_start[:, None]) & (cols[None, :] < col_start[:, None] + NA_COLS)
    col_off = np.clip(cols[None, :] - cols[:, None] + (NA_COLS - 1), 0, 2 * NA_COLS - 2)
    onehot_c = jnp.asarray(col_off[..., None] == np.arange(2 * NA_COLS - 1), F32)
    tabs = []
    for cls, off in enumerate(_window_q_offsets((C_TQ, C_BACK, C_WK))):
        blk = (0, 1, nqb - 1)[cls]
        w0 = int(np.clip(blk * tq - back, 0, rows * GRID_W - wk)) // GRID_W
        assert w0 * GRID_W + off == blk * tq and off % GRID_W == 0
        q_row = w0 + off // GRID_W + np.arange(nq_rows)
        k_row = w0 + np.arange(nk_rows)
        row_start = np.clip(q_row - NA_ROWS // 2, 0, rows - NA_ROWS)
        row_ok = (k_row[None, :] >= row_start[:, None]) & (k_row[None, :] < row_start[:, None] + NA_ROWS)
        row_off = np.clip(k_row[None, :] - q_row[:, None] + (NA_ROWS - 1), 0, 2 * NA_ROWS - 2)
        by_row = rpb[:, row_off, :].astype(F32)
        vals = jnp.einsum("hrsc,qkc->hrqsk", by_row, onehot_c, precision=lax.Precision.HIGHEST)
        ok = row_ok[:, None, :, None] & col_ok[None, :, None, :]
        tabs.append(jnp.where(jnp.asarray(ok)[None], vals, NEG_INF).reshape(rpb.shape[0], tq, wk))
    return jnp.stack(tabs)


def _merge_out_ln_kernel(*refs, alpha, n_branch):
    o_refs = refs[:n_branch]
    wbr_refs = refs[n_branch:2 * n_branch]
    g_ref, wout_ref, x_ref, gate_ref, lng_ref, lnb_ref, out_ref = refs[2 * n_branch:]
    d = x_ref.shape[1]
    y = None
    for r, (o_ref, wbr_ref) in enumerate(zip(o_refs, wbr_refs)):
        t = g_ref[:, r * d:(r + 1) * d].astype(F32) * _dot(o_ref[...], wbr_ref[...])
        y = t if y is None else y + t
    mix = _dot(y.astype(BF16), wout_ref[...])
    z = alpha * x_ref[...] + gate_ref[0] * mix
    mu = jnp.mean(z, axis=-1, keepdims=True)
    zc = z - mu
    var = jnp.mean(zc * zc, axis=-1, keepdims=True)
    out_ref[...] = (zc * lax.rsqrt(var + LN_EPS)) * lng_ref[...] + lnb_ref[...]


def merge_out_ln(branches, w_branch, gates, w_out, x, mod, gate_slot, group_of, ln_g, ln_b, alpha):
    rows = gates.shape[0]
    nbr = len(branches)
    d = w_out.shape[0]
    tm = _pick(rows, (256,))
    tm_group = group_of(tm)
    resident = pl.Buffered(1)
    return pl.pallas_call(
        functools.partial(_merge_out_ln_kernel, alpha=alpha, n_branch=nbr),
        grid=(rows // tm,),
        in_specs=[pl.BlockSpec((tm, o.shape[1]), lambda i: (i, 0)) for o in branches] + [
                  pl.BlockSpec(w.shape, lambda i: (0, 0), pipeline_mode=resident) for w in w_branch] + [
                  pl.BlockSpec((tm, nbr * d), lambda i: (i, 0)),
                  pl.BlockSpec((d, d), lambda i: (0, 0), pipeline_mode=resident),
                  pl.BlockSpec((tm, d), lambda i: (i, 0)),
                  pl.BlockSpec((1, 1, d), lambda i: (tm_group(i) * 6 + gate_slot, 0, 0)),
                  pl.BlockSpec((1, d), lambda i: (0, 0)),
                  pl.BlockSpec((1, d), lambda i: (0, 0))],
        out_specs=pl.BlockSpec((tm, d), lambda i: (i, 0)),
        out_shape=jax.ShapeDtypeStruct((rows, d), F32),
        compiler_params=_cparams(("parallel",)),
        name="merge_out_ln",
    )(*branches, *w_branch, gates, w_out, x, mod, ln_g.reshape(1, d), ln_b.reshape(1, d))


def _peer_pairs():
    return [(a, b) for a in range(PEER_TOPK) for b in range(PEER_TOPK) if (a + 1) * (b + 1) <= PEER_TOPK]


def _peer_topk_kernel(q_ref, keys_ref, cnt_ref, e0_ref, rank_ref, e1_ref, s0_scr):
    tops = [[[None] * PEER_HEADS for _ in range(PEER_TOPK)] for _ in range(2)]
    for h in range(PEER_HEADS):
        for p in range(2):
            hp = 2 * h + p
            qs = q_ref[:, hp * PEER_KEYS:(hp + 1) * PEER_KEYS]
            s = _dot_nt(keys_ref[hp], qs)
            work = s
            rank = jnp.full(s.shape, float(PEER_TOPK), F32)
            for r in range(PEER_TOPK):
                m = jnp.max(work, axis=0, keepdims=True)
                tops[p][r][h] = m
                if p == 1:
                    hit = work == m
                    rank = jnp.where(hit, float(r), rank)
                    work = jnp.where(hit, -jnp.inf, work)
                elif r + 1 < PEER_TOPK:
                    work = jnp.where(work == m, -jnp.inf, work)
            if p == 0:
                s0_scr[h] = s
            else:
                rank_ref[h, 0] = rank.astype(BF16)
                e1_ref[h, 0] = jnp.exp(s - tops[1][0][h]).astype(BF16)
    top = [[jnp.concatenate(tops[p][r], axis=0) for r in range(PEER_TOPK)] for p in range(2)]
    cands = [top[0][a] + top[1][b] for a, b in _peer_pairs()]
    work = list(cands)
    for r in range(PEER_TOPK):
        m = functools.reduce(jnp.maximum, work)
        if r + 1 < PEER_TOPK:
            work = [jnp.where(w == m, -jnp.inf, w) for w in work]
    tau = m
    cmax = cands[0]
    z = functools.reduce(jnp.add, [jnp.where(c >= tau, jnp.exp(c - cmax), 0.0) for c in cands])
    rz = 1.0 / z
    for h in range(PEER_HEADS):
        s0 = s0_scr[h]
        tau_h = tau[h:h + 1]
        cnt = None
        for b in range(PEER_TOPK):
            inc = jnp.where(s0 + top[1][b][h:h + 1] >= tau_h, 1.0, 0.0)
            cnt = inc if cnt is None else cnt + inc
        cnt_ref[h, 0] = cnt
        e0_ref[h, 0] = jnp.exp(s0 - top[0][0][h:h + 1]) * rz[h:h + 1]


def peer_topk(q, keys):
    rows = q.shape[0]
    tt = PEER_TOPK_TILE
    per_chunk = PEER_CHUNK // tt
    nch = rows // PEER_CHUNK
    big = pl.BlockSpec((PEER_HEADS, 1, PEER_KEYS, tt), lambda i: (0, i // per_chunk, 0, i % per_chunk))
    shape_f32 = jax.ShapeDtypeStruct((PEER_HEADS, nch, PEER_KEYS, PEER_CHUNK), F32)
    shape_bf16 = jax.ShapeDtypeStruct((PEER_HEADS, nch, PEER_KEYS, PEER_CHUNK), BF16)
    return pl.pallas_call(
        _peer_topk_kernel,
        grid=(rows // tt,),
        in_specs=[pl.BlockSpec((tt, q.shape[1]), lambda i: (i, 0)),
                  pl.BlockSpec(keys.shape, lambda i: (0, 0, 0))],
        out_specs=[big, big, big, big],
        out_shape=[shape_f32, shape_f32, shape_bf16, shape_bf16],
        scratch_shapes=[pltpu.VMEM((PEER_HEADS, PEER_KEYS, tt), F32)],
        compiler_params=_cparams(("parallel",)),
        name="peer_topk",
    )(q, keys)


def _modulate_t_kernel(x_ref, sh_ref, sc_ref, o_ref):
    h = x_ref[...] * (1.0 + sc_ref[0]) + sh_ref[0]
    o_ref[0] = h.T.astype(BF16)


def modulate_transposed(x, mod, mod_idx, group_of):
    rows, d = x.shape
    tt = PEER_CHUNK
    tt_group = group_of(tt)
    mod_specs = [pl.BlockSpec((1, 1, d), lambda i, s=slot: (tt_group(i) * 6 + s, 0, 0)) for slot in mod_idx]
    return pl.pallas_call(
        _modulate_t_kernel,
        grid=(rows // tt,),
        in_specs=[pl.BlockSpec((tt, d), lambda i: (i, 0))] + mod_specs,
        out_specs=pl.BlockSpec((1, d, tt), lambda i: (i, 0, 0)),
        out_shape=jax.ShapeDtypeStruct((rows // tt, d, tt), BF16),
        compiler_params=_cparams(("parallel",)),
        name="modulate_transposed",
    )(x, mod, mod)


def _peer_dense_kernel(ht_ref, u_ref, vt_ref, cnt_ref, e0_ref, rank_ref, e1_ref, o_ref, a_scr, w_scr, p_scr, *, n_e,
                       n_pairs):
    s = pl.program_id(0)
    e = s % n_e
    cur = s % 2
    prev = 1 - cur

    @pl.when(jnp.logical_and(s > 0, (s - 1) % n_e == 0))
    def _():
        o_ref[...] = jnp.zeros_like(o_ref)
    i_per_eb = PEER_EB // PEER_KEYS
    tc = w_scr.shape[-1]

    def gate_and_down():
        for ii in range(i_per_eb):
            sl = slice(ii * PEER_KEYS, (ii + 1) * PEER_KEYS)
            a = a_scr[prev, sl, :]
            act = 0.5 * a * (1.0 + lax.erf(a * (1.0 / math.sqrt(2.0))))
            p_scr[sl, :] = act.astype(BF16) * w_scr[prev, sl, :]
        o_ref[0] += _dot(vt_ref[0], p_scr[...])

    def up_and_weights():
        a_scr[cur] = _dot(u_ref[...], ht_ref[0])
        for ii in range(i_per_eb):
            i = e * i_per_eb + ii
            cnt = [jnp.broadcast_to(cnt_ref[h, 0, pl.ds(i, 1), :], (BF16_ROWS, tc)).astype(BF16)
                   for h in range(PEER_HEADS)]
            e0 = [jnp.broadcast_to(e0_ref[h, 0, pl.ds(i, 1), :], (BF16_ROWS, tc)).astype(BF16)
                  for h in range(PEER_HEADS)]
            for g in range(PEER_KEYS // BF16_ROWS):
                rows = slice(g * BF16_ROWS, (g + 1) * BF16_ROWS)
                w = None
                for h in range(PEER_HEADS):
                    e1 = e1_ref[h, 0, rows, :]
                    t = jnp.where(rank_ref[h, 0, rows, :] < cnt[h], e1, jnp.zeros_like(e1)) * e0[h]
                    w = t if w is None else w + t
                w_scr[cur, ii * PEER_KEYS + g * BF16_ROWS:ii * PEER_KEYS + (g + 1) * BF16_ROWS, :] = w

    @pl.when(s == 0)
    def _():
        up_and_weights()

    @pl.when(jnp.logical_and(s > 0, s < n_pairs))
    def _():
        gate_and_down()
        up_and_weights()

    @pl.when(s == n_pairs)
    def _():
        gate_and_down()


def peer_dense(ht, u, vt, cnt, e0, rank, e1):
    nch, d, tc = ht.shape
    n_e = u.shape[0] // PEER_EB
    n_pairs = nch * n_e

    def cur_chunk(s):
        return jnp.minimum(s, n_pairs - 1) // n_e

    def prev_chunk(s):
        return jnp.maximum(s - 1, 0) // n_e

    sel = pl.BlockSpec((PEER_HEADS, 1, PEER_KEYS, tc), lambda s: (0, cur_chunk(s), 0, 0))
    return pl.pallas_call(
        functools.partial(_peer_dense_kernel, n_e=n_e, n_pairs=n_pairs),
        grid=(n_pairs + 1,),
        in_specs=[pl.BlockSpec((1, d, tc), lambda s: (cur_chunk(s), 0, 0)),
                  pl.BlockSpec((PEER_EB, d), lambda s: (jnp.minimum(s, n_pairs - 1) % n_e, 0)),
                  pl.BlockSpec((1, d, PEER_EB), lambda s: (jnp.maximum(s - 1, 0) % n_e, 0, 0)),
                  sel, sel, sel, sel],
        out_specs=pl.BlockSpec((1, d, tc), lambda s: (prev_chunk(s), 0, 0)),
        out_shape=jax.ShapeDtypeStruct((nch, d, tc), F32),
        scratch_shapes=[pltpu.VMEM((2, PEER_EB, tc), F32), pltpu.VMEM((2, PEER_EB, tc), BF16),
                        pltpu.VMEM((PEER_EB, tc), BF16)],
        compiler_params=_cparams(("arbitrary",)),
        name="peer_dense",
    )(ht, u, vt, cnt, e0, rank, e1)


def _axial_angles(n, rot_dim):
    t = jnp.arange(n)
    rows = (t // GRID_W).astype(F32)
    cols = (t % GRID_W).astype(F32)
    nf = rot_dim // 4
    inv = ROPE_THETA ** (-jnp.arange(nf, dtype=F32) / nf)
    ang = jnp.concatenate([rows[:, None] * inv, cols[:, None] * inv], axis=-1)
    return jnp.cos(ang), jnp.sin(ang)


def _rope_tables(seq):
    assert A_ROPE == B_DIM == LANE // 2 and D_DIM == LANE

    def pad(t, fill):
        return jnp.concatenate([t, jnp.full((ROPE_TM, LANE), fill, F32)], axis=0)

    cos, sin = _axial_angles(seq, B_DIM)
    z = jnp.zeros_like(sin)
    c32 = jnp.concatenate([cos, cos, cos, cos], axis=1)
    sa = jnp.concatenate([z, sin, z, sin], axis=1)
    sb = jnp.concatenate([-sin, z, -sin, z], axis=1)
    cos, sin = _axial_angles(seq, D_DIM)
    c64 = jnp.concatenate([cos, cos], axis=1)
    s64 = jnp.concatenate([-sin, sin], axis=1)
    return [pad(c32, 1.0), pad(sa, 0.0), pad(sb, 0.0), pad(c64, 1.0), pad(s64, 0.0)]


def _half_block(t, first_half, axis):
    z = jnp.zeros_like(t)
    return jnp.concatenate([t, z] if first_half else [z, t], axis=axis)


def _in_proj_weight(w_in):
    d = w_in.shape[0]
    o = np.cumsum((0,) + IN_SIZES)
    part = [w_in[:, o[i]:o[i + 1]] for i in range(len(IN_SIZES))]
    per_kv = B_HEADS // B_KV_HEADS
    bq = [_half_block(part[3][:, h * B_DIM:(h + 1) * B_DIM], h // per_kv == 0, 1) for h in range(B_HEADS)]
    cols = bq + [_half_block(part[2], True, 1), part[4], part[9], part[10], part[6], part[7], part[8], part[1],
                 part[5], part[0], part[11], jnp.zeros((d, LANE), w_in.dtype)]
    out = jnp.concatenate(cols, axis=1).astype(BF16)
    assert out.shape[1] == U_BLOCKS * LANE
    return out


def _mla_up_weights(w_uq, w_ukv):
    dq, dkv = A_NOPE + A_ROPE, A_NOPE + A_V
    q_nope = [w_uq[:, h * dq:h * dq + A_NOPE] for h in range(A_HEADS)]
    q_rope = [_half_block(w_uq[:, h * dq + A_NOPE:(h + 1) * dq], True, 1) for h in range(A_HEADS)]
    k_nope = [w_ukv[:, h * dkv:h * dkv + A_NOPE] for h in range(A_HEADS)]
    v = [w_ukv[:, h * dkv + A_NOPE:(h + 1) * dkv] for h in range(A_HEADS)]
    return (jnp.concatenate(q_nope + q_rope, axis=1).astype(BF16), jnp.concatenate(k_nope + v, axis=1).astype(BF16))


def _branch_weights(w_branch):
    per_kv = B_HEADS // B_KV_HEADS
    wb = jnp.concatenate([_half_block(w_branch[1][h * B_DIM:(h + 1) * B_DIM], h // per_kv == 0, 0)
                          for h in range(B_HEADS)], axis=0)
    return [w_branch[0].astype(BF16), wb.astype(BF16), w_branch[2].astype(BF16), w_branch[3].astype(BF16)]


def kernel(x, c, ctx, c_ctx, w_ada, b_ada, w_in, a_q_norm, a_kv_norm, a_w_uq, a_w_ukv, b_sink, c_rpb, d_q_norm, d_k_norm, w_branch, w_gate, b_gate, w_out, ln1_g, ln1_b, ln2_g, ln2_b, peer_w_q, peer_sub_keys, peer_u, peer_v):
    bsz, seq, d = x.shape
    nctx = ctx.shape[1]
    depth = w_in.shape[0]
    t_lat, t_ctx = bsz * seq, bsz * nctx
    t_all = t_lat + t_ctx
    rows_grid = seq // GRID_W
    alpha = (2 * depth) ** 0.25

    tables = _rope_tables(seq)
    band_bias = band_bias_table()
    dims = dict(bsz=bsz, seq=seq, nctx=nctx, t_lat=t_lat)

    n_groups = -(-(bsz + 1) // 8) * 8
    cc = jnp.zeros((n_groups, d), F32).at[:bsz].set(c).at[bsz].set(c_ctx)

    def group_of(tm):
        lat_blocks, per_batch = t_lat // tm, seq // tm
        assert t_lat % tm == 0 and seq % tm == 0
        return lambda i: jnp.where(i < lat_blocks, i // per_batch, bsz)

    xs = jnp.concatenate([x.reshape(t_lat, d), ctx.reshape(t_ctx, d)], axis=0)

    for l in range(depth):
        need_ctx = l < depth - 1
        rows_out = t_all if need_ctx else t_lat
        mod = ada_modulation(cc, w_ada[l], b_ada[l]).reshape(n_groups * 6, 1, d)

        u = token_matmul(xs, _in_proj_weight(w_in[l]), prologue="modulate", mod=mod, mod_idx=(0, 1),
                         group_of=group_of, out_dtype=BF16, name="in_proj")
        w_gate_l = jnp.transpose(w_gate[l], (1, 0, 2)).reshape(d, N_BRANCH * d).astype(BF16)
        gates = token_matmul(xs, w_gate_l, rows=rows_out, prologue="modulate", mod=mod, mod_idx=(0, 1),
                             group_of=group_of, bias=b_gate[l].reshape(-1), act="sigmoid", out_dtype=BF16,
                             name="branch_gates")

        w_uq, w_ukv = _mla_up_weights(a_w_uq[l], a_w_ukv[l])
        qa = token_matmul(u, w_uq, x_cols=(A_Q_LORA, U_CQ * LANE // A_Q_LORA), prologue="rmsnorm", norm_g=a_q_norm[l], out_dtype=BF16,
                          name="mla_q_up")
        kva = token_matmul(u, w_ukv, x_cols=(A_KV_LORA, U_CKV), prologue="rmsnorm", norm_g=a_kv_norm[l],
                           out_dtype=BF16, name="mla_kv_up")
        ur = qk_prep(u, 0, U_PREP_N, U_PREP_R32, U_PREP_RMSQ, tables, d_q_norm[l], d_k_norm[l], t_lat, seq, "qk_prep")
        qr = qk_prep(qa, 1, A_HEADS, A_HEADS, 0, tables, d_q_norm[l], d_k_norm[l], t_lat, seq, "mla_q_rope")
        mixers = {
            "mla": dict(q_srcs=[(qa, 0), (qr, 0)], k_srcs=[(kva, 0, 1), (ur, R_KR, 0)], v_src=(kva, A_HEADS, 1),
                        heads=A_HEADS, scale=A_SCALE),
            "swa": dict(q_srcs=[(ur, R_BQ)], k_srcs=[(ur, R_BK, 0)], v_src=(u, U_BV, 0), heads=B_HEADS,
                        scale=B_SCALE, sink=b_sink[l], heads_per_step=B_HEADS),
            "na": dict(q_srcs=[(u, U_CQ_)], k_srcs=[(u, U_CK, 1)], v_src=(u, U_CV, 1), heads=C_HEADS,
                       scale=C_SCALE, heads_per_step=C_HEADS),
            "gqa": dict(q_srcs=[(ur, R_DQ)], k_srcs=[(ur, R_DK, D_HEADS // D_KV_HEADS)],
                        v_src=(u, U_DV, D_HEADS // D_KV_HEADS), heads=D_HEADS, scale=D_SCALE),
        }
        windows = {"swa": dict(window=(B_TQ, B_BACK, B_WK), bias=band_bias),
                   "na": dict(window=(C_TQ, C_BACK, C_WK), bias=neighbourhood_bias_table(c_rpb[l], rows_grid))}
        branches = []
        for mixer, cfg in mixers.items():
            o = attention(mode="lat", out_rows=rows_out, name=mixer + "_attention", **cfg, **windows.get(mixer, {}),
                          **dims)
            if need_ctx:
                o = attention(mode="ctx", out_prev=o, name=mixer + "_attention_ctx", **cfg, **dims)
            branches.append(o)

        xs = merge_out_ln(branches, _branch_weights(w_branch[l]), gates, w_out[l].astype(BF16), xs, mod, 2, group_of,
                          ln1_g[l], ln1_b[l], alpha)

        pq = token_matmul(xs, peer_w_q[l].astype(BF16), prologue="modulate", mod=mod, mod_idx=(3, 4),
                          group_of=group_of, out_dtype=BF16, name="peer_query")
        keys = peer_sub_keys[l].reshape(2 * PEER_HEADS, PEER_KEYS, PEER_QDIM // 2).astype(BF16)
        cnt, e0, rank, e1 = peer_topk(pq, keys)
        ht = modulate_transposed(xs, mod, (3, 4), group_of)
        n_exp = peer_u.shape[1]
        vt = jnp.transpose(peer_v[l].astype(BF16).reshape(n_exp // PEER_EB, PEER_EB, d), (0, 2, 1))
        ffn_t = peer_dense(ht, peer_u[l].astype(BF16), vt, cnt, e0, rank, e1)
        xs = residual_ln(xs, ffn_t, mod, 5, group_of, ln2_g[l], ln2_b[l], alpha)

    return xs[:t_lat].reshape(bsz, seq, d)
```

```python
import functools
import math

import jax
import jax.numpy as jnp
import numpy as np
from jax import lax
from jax.experimental import pallas as pl
from jax.experimental.pallas import tpu as pltpu

F32 = jnp.float32
BF16 = jnp.bfloat16

GRID_W = 64
ROPE_THETA = 10000.0
LN_EPS = 1e-5
RMS_EPS = 1e-6
NEG_INF = -1e30

A_HEADS, A_NOPE, A_ROPE, A_V, A_Q_LORA, A_KV_LORA = 4, 128, 64, 128, 384, 128
B_HEADS, B_KV_HEADS, B_DIM, B_WINDOW = 8, 2, 64, 128
C_HEADS, C_DIM, NA_ROWS, NA_COLS = 4, 128, 8, 16
D_HEADS, D_KV_HEADS, D_DIM = 4, 2, 128
N_BRANCH, BRANCH_W = 4, 512
PEER_HEADS, PEER_KEYS, PEER_QDIM, PEER_TOPK = 8, 128, 256, 16

A_SCALE = (A_NOPE + A_ROPE) ** -0.5
B_SCALE = B_DIM ** -0.5
C_SCALE = C_DIM ** -0.5
D_SCALE = D_DIM ** -0.5

IN_SIZES = (A_Q_LORA, A_KV_LORA, A_ROPE,
            B_HEADS * B_DIM, B_KV_HEADS * B_DIM, B_KV_HEADS * B_DIM,
            C_HEADS * C_DIM, C_HEADS * C_DIM, C_HEADS * C_DIM,
            D_HEADS * D_DIM, D_KV_HEADS * D_DIM, D_KV_HEADS * D_DIM)
LANE = 128
BF16_ROWS = 16

VMEM_LIMIT = 56 * 1024 * 1024

PEER_CHUNK = 512
PEER_EB = 1024
PEER_TOPK_TILE = 256

B_TQ, B_BACK, B_WK = 512, 128, 768
C_TQ, C_BACK, C_WK = 4 * GRID_W, 4 * GRID_W, 12 * GRID_W
DENSE_TQ = 1024
DENSE_SUB = 4
ROPE_TM = 512

R_BQ, R_KR, R_BK, R_DQ, R_DK = 0, 8, 9, 10, 14
U_PREP_N, U_PREP_R32, U_PREP_RMSQ = 16, 10, 4
U_CQ_, U_CK, U_CV, U_CKV, U_BV, U_CQ, U_DV, U_BLOCKS = 16, 20, 24, 28, 29, 30, 33, 36


def _cparams(sem):
    return pltpu.CompilerParams(dimension_semantics=sem, vmem_limit_bytes=VMEM_LIMIT)


def _pick(n, cands):
    for c in cands:
        if n % c == 0:
            return c
    raise ValueError(f"no tile in {cands} divides {n}")


def _dot(a, b):
    return jnp.dot(a, b, preferred_element_type=F32)


def _dot_nt(a, b):
    return lax.dot_general(a, b, (((1,), (1,)), ((), ())), preferred_element_type=F32)


def _ada_kernel(c_ref, w_ref, b_ref, o_ref):
    c = c_ref[...]
    s = c * (1.0 / (1.0 + jnp.exp(-c)))
    w = w_ref[0]
    s_hi = s.astype(BF16)
    s_lo = (s - s_hi.astype(F32)).astype(BF16)
    w_hi = w.astype(BF16)
    w_lo = (w - w_hi.astype(F32)).astype(BF16)
    acc = _dot(s_hi, w_hi) + _dot(s_lo, w_hi) + _dot(s_hi, w_lo)
    o_ref[...] = acc + b_ref[...]


def ada_modulation(cc, w_all, b, layer):
    r, k = cc.shape
    n = w_all.shape[2]
    tn = _pick(n, (512, 256, 128))
    return pl.pallas_call(
        _ada_kernel,
        grid=(n // tn,),
        in_specs=[pl.BlockSpec((r, k), lambda j: (0, 0)),
                  pl.BlockSpec((1, k, tn), lambda j: (layer, 0, j)),
                  pl.BlockSpec((1, tn), lambda j: (0, j))],
        out_specs=pl.BlockSpec((r, tn), lambda j: (0, j)),
        out_shape=jax.ShapeDtypeStruct((r, n), F32),
        compiler_params=_cparams(("arbitrary",)),
        name="ada_modulation",
    )(cc, w_all, b.reshape(1, n))


def _mm_kernel(*refs, prologue, has_bias, act):
    it = iter(refs)
    x_ref = next(it)
    if prologue == "modulate":
        sh_ref, sc_ref = next(it), next(it)
    elif prologue == "rmsnorm":
        g_ref = next(it)
    w_ref = next(it)
    b_ref = next(it) if has_bias else None
    o_ref = next(it)
    a_scr = next(it)

    @pl.when(pl.program_id(1) == 0)
    def _():
        x = x_ref[...].astype(F32)
        if prologue == "modulate":
            a = x * (1.0 + sc_ref[0]) + sh_ref[0]
        elif prologue == "rmsnorm":
            ms = jnp.mean(x * x, axis=-1, keepdims=True)
            a = (x * lax.rsqrt(ms + RMS_EPS)) * g_ref[...]
        else:
            a = x
        a_scr[...] = a.astype(BF16)

    acc = _dot(a_scr[...], w_ref[...])
    if has_bias:
        acc = acc + b_ref[...]
    if act == "sigmoid":
        acc = 1.0 / (1.0 + jnp.exp(-acc))
    o_ref[...] = acc.astype(o_ref.dtype)


def token_matmul(x, w, *, rows=None, x_cols=None, prologue=None, mod=None, mod_idx=None, group_of=None,
                 norm_g=None, bias=None, act=None, out_dtype=F32, name="token_matmul"):
    rows = x.shape[0] if rows is None else rows
    kdim, n = w.shape
    kw, kblk = (x.shape[1], 0) if x_cols is None else x_cols
    assert kw == kdim
    tm = _pick(rows, (1024, 512, 256))
    tn = _pick(n, (2048, 1536, 1024, 512, 256, 128))
    in_specs = [pl.BlockSpec((tm, kdim), lambda i, j: (i, kblk))]
    args = [x]
    if prologue == "modulate":
        tm_group = group_of(tm)
        for slot in mod_idx:
            in_specs.append(pl.BlockSpec((1, 1, kdim), lambda i, j, s=slot: (tm_group(i) * 6 + s, 0, 0)))
            args.append(mod)
    elif prologue == "rmsnorm":
        in_specs.append(pl.BlockSpec((1, kdim), lambda i, j: (0, 0)))
        args.append(norm_g.reshape(1, kdim))
    in_specs.append(pl.BlockSpec((kdim, tn), lambda i, j: (0, j)))
    args.append(w)
    if bias is not None:
        in_specs.append(pl.BlockSpec((1, tn), lambda i, j: (0, j)))
        args.append(bias.reshape(1, n))
    return pl.pallas_call(
        functools.partial(_mm_kernel, prologue=prologue, has_bias=bias is not None, act=act),
        grid=(rows // tm, n // tn),
        in_specs=in_specs,
        out_specs=pl.BlockSpec((tm, tn), lambda i, j: (i, j)),
        out_shape=jax.ShapeDtypeStruct((rows, n), out_dtype),
        scratch_shapes=[pltpu.VMEM((tm, kdim), BF16)],
        compiler_params=_cparams(("parallel", "arbitrary")),
        name=name,
    )(*args)


def _res_ln_kernel(x_ref, yt_ref, gate_ref, g_ref, b_ref, o_ref, *, alpha):
    z = alpha * x_ref[...] + gate_ref[0] * yt_ref[0].T
    mu = jnp.mean(z, axis=-1, keepdims=True)
    zc = z - mu
    var = jnp.mean(zc * zc, axis=-1, keepdims=True)
    o_ref[...] = (zc * lax.rsqrt(var + LN_EPS)) * g_ref[...] + b_ref[...]


def residual_ln(x, yt, mod, gate_slot, group_of, ln_g, ln_b, alpha):
    nch, d, tm = yt.shape
    rows = nch * tm
    tm_group = group_of(tm)
    return pl.pallas_call(
        functools.partial(_res_ln_kernel, alpha=alpha),
        grid=(nch,),
        in_specs=[pl.BlockSpec((tm, d), lambda i: (i, 0)),
                  pl.BlockSpec((1, d, tm), lambda i: (i, 0, 0)),
                  pl.BlockSpec((1, 1, d), lambda i: (tm_group(i) * 6 + gate_slot, 0, 0)),
                  pl.BlockSpec((1, d), lambda i: (0, 0)),
                  pl.BlockSpec((1, d), lambda i: (0, 0))],
        out_specs=pl.BlockSpec((tm, d), lambda i: (i, 0)),
        out_shape=jax.ShapeDtypeStruct((rows, d), F32),
        compiler_params=_cparams(("parallel",)),
        name="residual_ln",
    )(x, yt, mod, ln_g.reshape(1, d), ln_b.reshape(1, d))


def _qk_prep_kernel(x_ref, c32_ref, sa_ref, sb_ref, c64_ref, s64_ref, gq_ref, gk_ref, o_ref, *, n_r32, n_rms_q):
    for j in range(x_ref.shape[1] // LANE):
        cols = slice(j * LANE, (j + 1) * LANE)
        x = x_ref[:, cols].astype(F32)
        if j < n_r32:
            y = x * c32_ref[...] + pltpu.roll(x, 32, 1) * sa_ref[...] + pltpu.roll(x, 96, 1) * sb_ref[...]
        else:
            g = gq_ref[...] if j < n_r32 + n_rms_q else gk_ref[...]
            ms = jnp.mean(x * x, axis=-1, keepdims=True)
            y = (x * lax.rsqrt(ms + RMS_EPS)) * g
            y = y * c64_ref[...] + pltpu.roll(y, 64, 1) * s64_ref[...]
        o_ref[:, cols] = y.astype(o_ref.dtype)


def qk_prep(x, group_idx, n_blocks, n_r32, n_rms_q, tables, gq, gk, lat_rows, seq, name):
    rows = x.shape[0]
    tm = ROPE_TM
    width = n_blocks * LANE
    lat_blocks, per_seq = lat_rows // tm, seq // tm
    tab_spec = pl.BlockSpec((tm, LANE), lambda i: (jnp.where(i < lat_blocks, i % per_seq, per_seq), 0))
    vec_spec = pl.BlockSpec((1, LANE), lambda i: (0, 0))
    return pl.pallas_call(
        functools.partial(_qk_prep_kernel, n_r32=n_r32, n_rms_q=n_rms_q),
        grid=(rows // tm,),
        in_specs=[pl.BlockSpec((tm, width), lambda i: (i, group_idx))] + [tab_spec] * 5 + [vec_spec] * 2,
        out_specs=pl.BlockSpec((tm, width), lambda i: (i, 0)),
        out_shape=jax.ShapeDtypeStruct((rows, width), BF16),
        compiler_params=_cparams(("parallel",)),
        name=name,
    )(x, *tables, gq.reshape(1, LANE), gk.reshape(1, LANE))


def _attn_kernel(*refs, n_parts, has_lat, window, has_bias, bias_per_head, has_sink, scale, n_lat, n_sub, hps,
                 per_head):
    it = iter(refs)
    q_refs = [next(it) for _ in range(n_parts)]
    if has_lat:
        kl_refs = [next(it) for _ in range(n_parts)]
        vl_ref = next(it)
    kc_refs = [next(it) for _ in range(n_parts)]
    vc_ref = next(it)
    bias_ref = next(it) if has_bias else None
    sink_ref = next(it) if has_sink else None
    refs = list(it)
    o_ref = refs[-1]

    if has_lat and window is not None:
        tq, back, wk = window
        start = pl.multiple_of(jnp.clip(pl.program_id(2) * tq - back, 0, n_lat - wk), LANE)
        key_rows = pl.ds(start, wk)
    else:
        key_rows = slice(None)
    sub = o_ref.shape[0] // n_sub
    for hh in range(hps):
        lanes = slice(hh * LANE, (hh + 1) * LANE)

        def cols(j):
            return lanes if per_head[j] else slice(None)

        kcs = [k[:, cols(j)] for j, k in enumerate(kc_refs)]
        vc = vc_ref[:, cols(n_parts)]
        if has_lat:
            kls = [k[key_rows, cols(j)] for j, k in enumerate(kl_refs)]
            vl = vl_ref[key_rows, cols(n_parts)]
        if has_sink:
            sink = sink_ref[pl.program_id(1) * hps + hh]
        for sb in range(n_sub):
            rows = slice(sb * sub, (sb + 1) * sub)
            qs = [q[rows, lanes] for q in q_refs]
            s2 = functools.reduce(jnp.add, [_dot_nt(q, k) for q, k in zip(qs, kcs)]) * scale
            m = jnp.max(s2, axis=-1, keepdims=True)
            if has_lat:
                s1 = functools.reduce(jnp.add, [_dot_nt(q, k) for q, k in zip(qs, kls)]) * scale
                if has_bias:
                    s1 = s1 + bias_ref[0, hh if bias_per_head else 0, rows, :]
                m = jnp.maximum(m, jnp.max(s1, axis=-1, keepdims=True))
            if has_sink:
                m = jnp.maximum(m, sink)
            p2 = jnp.exp(s2 - m)
            l = jnp.sum(p2, axis=-1, keepdims=True)
            o = _dot(p2.astype(BF16), vc)
            if has_lat:
                p1 = jnp.exp(s1 - m)
                l = l + jnp.sum(p1, axis=-1, keepdims=True)
                o = o + _dot(p1.astype(BF16), vl)
            if has_sink:
                l = l + jnp.exp(sink - m)
            o_ref[rows, lanes] = (o / l).astype(o_ref.dtype)


def attention(q_srcs, k_srcs, v_src, *, heads, scale, bsz, seq, nctx, t_lat, mode, out_rows=None, out_prev=None,
              window=None, bias=None, sink=None, heads_per_step=1, name="attention"):
    lat = mode == "lat"
    hps = heads_per_step
    n_parts = len(q_srcs)
    tq = (window[0] if window is not None else DENSE_TQ) if lat else nctx
    nqb = (seq if lat else nctx) // tq
    ctx0 = t_lat // nctx
    kv_srcs = list(k_srcs) + [v_src]
    per_head = tuple(grp == 1 for _, _, grp in kv_srcs)
    assert heads % hps == 0 and all(first % hps == 0 for _, first in q_srcs)
    assert hps == 1 or all(grp == 0 or (grp == 1 and first % hps == 0) for _, first, grp in kv_srcs)

    def kv_spec(rows, row_blk, first, grp):
        if hps > 1 and grp == 1:
            return pl.BlockSpec((rows, hps * LANE), lambda bi, hg, i: (row_blk(bi), first // hps + hg))
        return pl.BlockSpec((rows, LANE), lambda bi, hg, i: (row_blk(bi), first + (hg // grp if grp else 0)))

    def q_row(bi, i):
        return bi * nqb + i if lat else t_lat // tq + bi

    in_specs, args = [], []
    for arr, first in q_srcs:
        in_specs.append(pl.BlockSpec((tq, hps * LANE), lambda bi, hg, i, f=first: (q_row(bi, i), f // hps + hg)))
        args.append(arr)
    if lat:
        for arr, first, grp in kv_srcs:
            in_specs.append(kv_spec(seq, lambda bi: bi, first, grp))
            args.append(arr)
    for arr, first, grp in kv_srcs:
        in_specs.append(kv_spec(nctx, lambda bi: ctx0 + bi, first, grp))
        args.append(arr)
    bias_per_head = bias is not None and bias.shape[1] != 1
    if bias is not None:
        in_specs.append(pl.BlockSpec(
            (1, hps if bias_per_head else 1, tq, window[2]),
            lambda bi, hg, i: (jnp.where(i == 0, 0, jnp.where(i == nqb - 1, 2, 1)), hg if bias_per_head else 0, 0, 0)))
        args.append(bias)
    if sink is not None:
        in_specs.append(pl.BlockSpec(memory_space=pltpu.SMEM))
        args.append(sink.astype(F32))
    aliases = {}
    if out_prev is not None:
        aliases = {len(args): 0}
        in_specs.append(pl.BlockSpec(memory_space=pl.ANY))
        args.append(out_prev)
        out_rows = out_prev.shape[0]
    if window is not None:
        assert seq % tq == 0 and nqb >= 2 and seq >= window[2]
    return pl.pallas_call(
        functools.partial(_attn_kernel, n_parts=n_parts, has_lat=lat, window=window, has_bias=bias is not None,
                          bias_per_head=bias_per_head, has_sink=sink is not None, scale=scale, n_lat=seq,
                          n_sub=DENSE_SUB if (lat and window is None) else 1, hps=hps, per_head=per_head),
        grid=(bsz, heads // hps, nqb),
        in_specs=in_specs,
        out_specs=pl.BlockSpec((tq, hps * LANE), lambda bi, hg, i: (q_row(bi, i), hg)),
        out_shape=jax.ShapeDtypeStruct((out_rows, heads * LANE), BF16),
        input_output_aliases=aliases,
        compiler_params=_cparams(("parallel", "parallel", "arbitrary")),
        name=name,
    )(*args)


def _window_q_offsets(geom):
    tq, back, wk = geom
    return (0, back, wk - tq)


def band_bias_table():
    tq, _, wk = (B_TQ, B_BACK, B_WK)
    kpos = np.arange(wk)[None, :]
    tabs = []
    for off in _window_q_offsets((B_TQ, B_BACK, B_WK)):
        qpos = off + np.arange(tq)[:, None]
        tabs.append(np.where(np.abs(qpos - kpos) <= B_WINDOW, 0.0, NEG_INF))
    return jnp.asarray(np.stack(tabs)[:, None], F32)


def neighbourhood_bias_table(rpb, rows):
    tq, back, wk = (C_TQ, C_BACK, C_WK)
    assert rows >= wk // GRID_W and rows >= NA_ROWS
    nqb = rows * GRID_W // tq
    nq_rows, nk_rows = tq // GRID_W, wk // GRID_W
    cols = np.arange(GRID_W)
    col_start = np.clip(cols - NA_COLS // 2, 0, GRID_W - NA_COLS)
    col_ok = (cols[None, :] >= col_start[:, None]) & (cols[None, :] < col_start[:, None] + NA_COLS)
    col_off = np.clip(cols[None, :] - cols[:, None] + (NA_COLS - 1), 0, 2 * NA_COLS - 2)
    onehot_c = jnp.asarray(col_off[..., None] == np.arange(2 * NA_COLS - 1), F32)
    tabs = []
    for cls, off in enumerate(_window_q_offsets((C_TQ, C_BACK, C_WK))):
        blk = (0, 1, nqb - 1)[cls]
        w0 = int(np.clip(blk * tq - back, 0, rows * GRID_W - wk)) // GRID_W
        assert w0 * GRID_W + off == blk * tq and off % GRID_W == 0
        q_row = w0 + off // GRID_W + np.arange(nq_rows)
        k_row = w0 + np.arange(nk_rows)
        row_start = np.clip(q_row - NA_ROWS // 2, 0, rows - NA_ROWS)
        row_ok = (k_row[None, :] >= row_start[:, None]) & (k_row[None, :] < row_start[:, None] + NA_ROWS)
        row_off = np.clip(k_row[None, :] - q_row[:, None] + (NA_ROWS - 1), 0, 2 * NA_ROWS - 2)
        by_row = rpb[:, row_off, :].astype(F32)
        vals = jnp.einsum("hrsc,qkc->hrqsk", by_row, onehot_c, precision=lax.Precision.HIGHEST)
        ok = row_ok[:, None, :, None] & col_ok[None, :, None, :]
        tabs.append(jnp.where(jnp.asarray(ok)[None], vals, NEG_INF).reshape(rpb.shape[0], tq, wk))
    return jnp.stack(tabs)


def _merge_out_ln_kernel(*refs, alpha, n_branch):
    o_refs = refs[:n_branch]
    wbr_refs = refs[n_branch:2 * n_branch]
    g_ref, wout_ref, x_ref, gate_ref, lng_ref, lnb_ref, out_ref = refs[2 * n_branch:]
    d = x_ref.shape[1]
    y = None
    for r, (o_ref, wbr_ref) in enumerate(zip(o_refs, wbr_refs)):
        t = g_ref[:, r * d:(r + 1) * d].astype(F32) * _dot(o_ref[...], wbr_ref[...])
        y = t if y is None else y + t
    mix = _dot(y.astype(BF16), wout_ref[...])
    z = alpha * x_ref[...] + gate_ref[0] * mix
    mu = jnp.mean(z, axis=-1, keepdims=True)
    zc = z - mu
    var = jnp.mean(zc * zc, axis=-1, keepdims=True)
    out_ref[...] = (zc * lax.rsqrt(var + LN_EPS)) * lng_ref[...] + lnb_ref[...]


def merge_out_ln(branches, w_branch, gates, w_out, x, mod, gate_slot, group_of, ln_g, ln_b, alpha):
    rows = gates.shape[0]
    nbr = len(branches)
    d = w_out.shape[0]
    tm = _pick(rows, (256,))
    tm_group = group_of(tm)
    resident = pl.Buffered(1)
    return pl.pallas_call(
        functools.partial(_merge_out_ln_kernel, alpha=alpha, n_branch=nbr),
        grid=(rows // tm,),
        in_specs=[pl.BlockSpec((tm, o.shape[1]), lambda i: (i, 0)) for o in branches] + [
                  pl.BlockSpec(w.shape, lambda i: (0, 0), pipeline_mode=resident) for w in w_branch] + [
                  pl.BlockSpec((tm, nbr * d), lambda i: (i, 0)),
                  pl.BlockSpec((d, d), lambda i: (0, 0), pipeline_mode=resident),
                  pl.BlockSpec((tm, d), lambda i: (i, 0)),
                  pl.BlockSpec((1, 1, d), lambda i: (tm_group(i) * 6 + gate_slot, 0, 0)),
                  pl.BlockSpec((1, d), lambda i: (0, 0)),
                  pl.BlockSpec((1, d), lambda i: (0, 0))],
        out_specs=pl.BlockSpec((tm, d), lambda i: (i, 0)),
        out_shape=jax.ShapeDtypeStruct((rows, d), F32),
        compiler_params=_cparams(("parallel",)),
        name="merge_out_ln",
    )(*branches, *w_branch, gates, w_out, x, mod, ln_g.reshape(1, d), ln_b.reshape(1, d))


def _peer_pairs():
    return [(a, b) for a in range(PEER_TOPK) for b in range(PEER_TOPK) if (a + 1) * (b + 1) <= PEER_TOPK]


def _peer_topk_kernel(q_ref, keys_ref, cnt_ref, e0_ref, rank_ref, e1_ref, s0_scr):
    tops = [[[None] * PEER_HEADS for _ in range(PEER_TOPK)] for _ in range(2)]
    for h in range(PEER_HEADS):
        for p in range(2):
            hp = 2 * h + p
            qs = q_ref[:, hp * PEER_KEYS:(hp + 1) * PEER_KEYS]
            s = _dot_nt(keys_ref[hp], qs)
            work = s
            rank = jnp.full(s.shape, float(PEER_TOPK), F32)
            for r in range(PEER_TOPK):
                m = jnp.max(work, axis=0, keepdims=True)
                tops[p][r][h] = m
                if p == 1:
                    hit = work == m
                    rank = jnp.where(hit, float(r), rank)
                    work = jnp.where(hit, -jnp.inf, work)
                elif r + 1 < PEER_TOPK:
                    work = jnp.where(work == m, -jnp.inf, work)
            if p == 0:
                s0_scr[h] = s
            else:
                rank_ref[h, 0] = rank.astype(BF16)
                e1_ref[h, 0] = jnp.exp(s - tops[1][0][h]).astype(BF16)
    top = [[jnp.concatenate(tops[p][r], axis=0) for r in range(PEER_TOPK)] for p in range(2)]
    cands = [top[0][a] + top[1][b] for a, b in _peer_pairs()]
    work = list(cands)
    for r in range(PEER_TOPK):
        m = functools.reduce(jnp.maximum, work)
        if r + 1 < PEER_TOPK:
            work = [jnp.where(w == m, -jnp.inf, w) for w in work]
    tau = m
    cmax = cands[0]
    z = functools.reduce(jnp.add, [jnp.where(c >= tau, jnp.exp(c - cmax), 0.0) for c in cands])
    rz = 1.0 / z
    for h in range(PEER_HEADS):
        s0 = s0_scr[h]
        tau_h = tau[h:h + 1]
        cnt = None
        for b in range(PEER_TOPK):
            inc = jnp.where(s0 + top[1][b][h:h + 1] >= tau_h, 1.0, 0.0)
            cnt = inc if cnt is None else cnt + inc
        cnt_ref[h, 0] = cnt
        e0_ref[h, 0] = jnp.exp(s0 - top[0][0][h:h + 1]) * rz[h:h + 1]


def peer_topk(q, keys):
    rows = q.shape[0]
    tt = PEER_TOPK_TILE
    per_chunk = PEER_CHUNK // tt
    nch = rows // PEER_CHUNK
    big = pl.BlockSpec((PEER_HEADS, 1, PEER_KEYS, tt), lambda i: (0, i // per_chunk, 0, i % per_chunk))
    shape_f32 = jax.ShapeDtypeStruct((PEER_HEADS, nch, PEER_KEYS, PEER_CHUNK), F32)
    shape_bf16 = jax.ShapeDtypeStruct((PEER_HEADS, nch, PEER_KEYS, PEER_CHUNK), BF16)
    return pl.pallas_call(
        _peer_topk_kernel,
        grid=(rows // tt,),
        in_specs=[pl.BlockSpec((tt, q.shape[1]), lambda i: (i, 0)),
                  pl.BlockSpec(keys.shape, lambda i: (0, 0, 0))],
        out_specs=[big, big, big, big],
        out_shape=[shape_f32, shape_f32, shape_bf16, shape_bf16],
        scratch_shapes=[pltpu.VMEM((PEER_HEADS, PEER_KEYS, tt), F32)],
        compiler_params=_cparams(("parallel",)),
        name="peer_topk",
    )(q, keys)


def _modulate_t_kernel(x_ref, sh_ref, sc_ref, o_ref):
    h = x_ref[...] * (1.0 + sc_ref[0]) + sh_ref[0]
    o_ref[0] = h.T.astype(BF16)


def modulate_transposed(x, mod, mod_idx, group_of):
    rows, d = x.shape
    tt = PEER_CHUNK
    tt_group = group_of(tt)
    mod_specs = [pl.BlockSpec((1, 1, d), lambda i, s=slot: (tt_group(i) * 6 + s, 0, 0)) for slot in mod_idx]
    return pl.pallas_call(
        _modulate_t_kernel,
        grid=(rows // tt,),
        in_specs=[pl.BlockSpec((tt, d), lambda i: (i, 0))] + mod_specs,
        out_specs=pl.BlockSpec((1, d, tt), lambda i: (i, 0, 0)),
        out_shape=jax.ShapeDtypeStruct((rows // tt, d, tt), BF16),
        compiler_params=_cparams(("parallel",)),
        name="modulate_transposed",
    )(x, mod, mod)


def _peer_dense_kernel(ht_ref, u_ref, vt_ref, cnt_ref, e0_ref, rank_ref, e1_ref, o_ref, a_scr, w_scr, p_scr, *, n_e,
                       n_pairs):
    s = pl.program_id(0)
    e = s % n_e
    cur = s % 2
    prev = 1 - cur

    @pl.when(jnp.logical_and(s > 0, (s - 1) % n_e == 0))
    def _():
        o_ref[...] = jnp.zeros_like(o_ref)
    i_per_eb = PEER_EB // PEER_KEYS
    tc = w_scr.shape[-1]

    def gate_and_down():
        for ii in range(i_per_eb):
            sl = slice(ii * PEER_KEYS, (ii + 1) * PEER_KEYS)
            a = a_scr[prev, sl, :]
            act = 0.5 * a * (1.0 + lax.erf(a * (1.0 / math.sqrt(2.0))))
            p_scr[sl, :] = act.astype(BF16) * w_scr[prev, sl, :]
        o_ref[0] += _dot(vt_ref[0], p_scr[...])

    def up_and_weights():
        a_scr[cur] = _dot(u_ref[...], ht_ref[0])
        for ii in range(i_per_eb):
            i = e * i_per_eb + ii
            cnt = [jnp.broadcast_to(cnt_ref[h, 0, pl.ds(i, 1), :], (BF16_ROWS, tc)).astype(BF16)
                   for h in range(PEER_HEADS)]
            e0 = [jnp.broadcast_to(e0_ref[h, 0, pl.ds(i, 1), :], (BF16_ROWS, tc)).astype(BF16)
                  for h in range(PEER_HEADS)]
            for g in range(PEER_KEYS // BF16_ROWS):
                rows = slice(g * BF16_ROWS, (g + 1) * BF16_ROWS)
                w = None
                for h in range(PEER_HEADS):
                    e1 = e1_ref[h, 0, rows, :]
                    t = jnp.where(rank_ref[h, 0, rows, :] < cnt[h], e1, jnp.zeros_like(e1)) * e0[h]
                    w = t if w is None else w + t
                w_scr[cur, ii * PEER_KEYS + g * BF16_ROWS:ii * PEER_KEYS + (g + 1) * BF16_ROWS, :] = w

    @pl.when(s == 0)
    def _():
        up_and_weights()

    @pl.when(jnp.logical_and(s > 0, s < n_pairs))
    def _():
        gate_and_down()
        up_and_weights()

    @pl.when(s == n_pairs)
    def _():
        gate_and_down()


def peer_dense(ht, u, vt, cnt, e0, rank, e1):
    nch, d, tc = ht.shape
    n_e = u.shape[0] // PEER_EB
    n_pairs = nch * n_e

    def cur_chunk(s):
        return jnp.minimum(s, n_pairs - 1) // n_e

    def prev_chunk(s):
        return jnp.maximum(s - 1, 0) // n_e

    sel = pl.BlockSpec((PEER_HEADS, 1, PEER_KEYS, tc), lambda s: (0, cur_chunk(s), 0, 0))
    return pl.pallas_call(
        functools.partial(_peer_dense_kernel, n_e=n_e, n_pairs=n_pairs),
        grid=(n_pairs + 1,),
        in_specs=[pl.BlockSpec((1, d, tc), lambda s: (cur_chunk(s), 0, 0)),
                  pl.BlockSpec((PEER_EB, d), lambda s: (jnp.minimum(s, n_pairs - 1) % n_e, 0)),
                  pl.BlockSpec((1, d, PEER_EB), lambda s: (jnp.maximum(s - 1, 0) % n_e, 0, 0)),
                  sel, sel, sel, sel],
        out_specs=pl.BlockSpec((1, d, tc), lambda s: (prev_chunk(s), 0, 0)),
        out_shape=jax.ShapeDtypeStruct((nch, d, tc), F32),
        scratch_shapes=[pltpu.VMEM((2, PEER_EB, tc), F32), pltpu.VMEM((2, PEER_EB, tc), BF16),
                        pltpu.VMEM((PEER_EB, tc), BF16)],
        compiler_params=_cparams(("arbitrary",)),
        name="peer_dense",
    )(ht, u, vt, cnt, e0, rank, e1)


def _axial_angles(n, rot_dim):
    t = jnp.arange(n)
    rows = (t // GRID_W).astype(F32)
    cols = (t % GRID_W).astype(F32)
    nf = rot_dim // 4
    inv = ROPE_THETA ** (-jnp.arange(nf, dtype=F32) / nf)
    ang = jnp.concatenate([rows[:, None] * inv, cols[:, None] * inv], axis=-1)
    return jnp.cos(ang), jnp.sin(ang)


def _rope_tables(seq):
    assert A_ROPE == B_DIM == LANE // 2 and D_DIM == LANE

    def pad(t, fill):
        return jnp.concatenate([t, jnp.full((ROPE_TM, LANE), fill, F32)], axis=0)

    cos, sin = _axial_angles(seq, B_DIM)
    z = jnp.zeros_like(sin)
    c32 = jnp.concatenate([cos, cos, cos, cos], axis=1)
    sa = jnp.concatenate([z, sin, z, sin], axis=1)
    sb = jnp.concatenate([-sin, z, -sin, z], axis=1)
    cos, sin = _axial_angles(seq, D_DIM)
    c64 = jnp.concatenate([cos, cos], axis=1)
    s64 = jnp.concatenate([-sin, sin], axis=1)
    return [pad(c32, 1.0), pad(sa, 0.0), pad(sb, 0.0), pad(c64, 1.0), pad(s64, 0.0)]


def _half_block(t, first_half, axis):
    z = jnp.zeros_like(t)
    return jnp.concatenate([t, z] if first_half else [z, t], axis=axis)


def _in_proj_weight(w_in):
    d = w_in.shape[0]
    o = np.cumsum((0,) + IN_SIZES)
    part = [w_in[:, o[i]:o[i + 1]] for i in range(len(IN_SIZES))]
    per_kv = B_HEADS // B_KV_HEADS
    bq = [_half_block(part[3][:, h * B_DIM:(h + 1) * B_DIM], h // per_kv == 0, 1) for h in range(B_HEADS)]
    cols = bq + [_half_block(part[2], True, 1), part[4], part[9], part[10], part[6], part[7], part[8], part[1],
                 part[5], part[0], part[11], jnp.zeros((d, LANE), w_in.dtype)]
    out = jnp.concatenate(cols, axis=1).astype(BF16)
    assert out.shape[1] == U_BLOCKS * LANE
    return out


def _mla_up_weights(w_uq, w_ukv):
    dq, dkv = A_NOPE + A_ROPE, A_NOPE + A_V
    q_nope = [w_uq[:, h * dq:h * dq + A_NOPE] for h in range(A_HEADS)]
    q_rope = [_half_block(w_uq[:, h * dq + A_NOPE:(h + 1) * dq], True, 1) for h in range(A_HEADS)]
    k_nope = [w_ukv[:, h * dkv:h * dkv + A_NOPE] for h in range(A_HEADS)]
    v = [w_ukv[:, h * dkv + A_NOPE:(h + 1) * dkv] for h in range(A_HEADS)]
    return (jnp.concatenate(q_nope + q_rope, axis=1).astype(BF16), jnp.concatenate(k_nope + v, axis=1).astype(BF16))


def _branch_weights(w_branch):
    per_kv = B_HEADS // B_KV_HEADS
    wb = jnp.concatenate([_half_block(w_branch[1][h * B_DIM:(h + 1) * B_DIM], h // per_kv == 0, 0)
                          for h in range(B_HEADS)], axis=0)
    return [w_branch[0].astype(BF16), wb.astype(BF16), w_branch[2].astype(BF16), w_branch[3].astype(BF16)]


def kernel(x, c, ctx, c_ctx, w_ada, b_ada, w_in, a_q_norm, a_kv_norm, a_w_uq, a_w_ukv, b_sink, c_rpb, d_q_norm, d_k_norm, w_branch, w_gate, b_gate, w_out, ln1_g, ln1_b, ln2_g, ln2_b, peer_w_q, peer_sub_keys, peer_u, peer_v):
    bsz, seq, d = x.shape
    nctx = ctx.shape[1]
    depth = w_in.shape[0]
    t_lat, t_ctx = bsz * seq, bsz * nctx
    t_all = t_lat + t_ctx
    rows_grid = seq // GRID_W
    alpha = (2 * depth) ** 0.25

    tables = _rope_tables(seq)
    band_bias = band_bias_table()
    dims = dict(bsz=bsz, seq=seq, nctx=nctx, t_lat=t_lat)

    n_groups = -(-(bsz + 1) // 8) * 8
    cc = jnp.zeros((n_groups, d), F32).at[:bsz].set(c).at[bsz].set(c_ctx)

    def group_of(tm):
        lat_blocks, per_batch = t_lat // tm, seq // tm
        assert t_lat % tm == 0 and seq % tm == 0
        return lambda i: jnp.where(i < lat_blocks, i // per_batch, bsz)

    xs = jnp.concatenate([x.reshape(t_lat, d), ctx.reshape(t_ctx, d)], axis=0)

    for l in range(depth):
        need_ctx = l < depth - 1
        rows_out = t_all if need_ctx else t_lat
        mod = ada_modulation(cc, w_ada, b_ada[l], l).reshape(n_groups * 6, 1, d)

        u = token_matmul(xs, _in_proj_weight(w_in[l]), prologue="modulate", mod=mod, mod_idx=(0, 1),
                         group_of=group_of, out_dtype=BF16, name="in_proj")
        w_gate_l = jnp.transpose(w_gate[l], (1, 0, 2)).reshape(d, N_BRANCH * d).astype(BF16)
        gates = token_matmul(xs, w_gate_l, rows=rows_out, prologue="modulate", mod=mod, mod_idx=(0, 1),
                             group_of=group_of, bias=b_gate[l].reshape(-1), act="sigmoid", out_dtype=BF16,
                             name="branch_gates")

        w_uq, w_ukv = _mla_up_weights(a_w_uq[l], a_w_ukv[l])
        qa = token_matmul(u, w_uq, x_cols=(A_Q_LORA, U_CQ * LANE // A_Q_LORA), prologue="rmsnorm", norm_g=a_q_norm[l], out_dtype=BF16,
                          name="mla_q_up")
        kva = token_matmul(u, w_ukv, x_cols=(A_KV_LORA, U_CKV), prologue="rmsnorm", norm_g=a_kv_norm[l],
                           out_dtype=BF16, name="mla_kv_up")
        ur = qk_prep(u, 0, U_PREP_N, U_PREP_R32, U_PREP_RMSQ, tables, d_q_norm[l], d_k_norm[l], t_lat, seq, "qk_prep")
        qr = qk_prep(qa, 1, A_HEADS, A_HEADS, 0, tables, d_q_norm[l], d_k_norm[l], t_lat, seq, "mla_q_rope")
        mixers = {
            "mla": dict(q_srcs=[(qa, 0), (qr, 0)], k_srcs=[(kva, 0, 1), (ur, R_KR, 0)], v_src=(kva, A_HEADS, 1),
                        heads=A_HEADS, scale=A_SCALE),
            "swa": dict(q_srcs=[(ur, R_BQ)], k_srcs=[(ur, R_BK, 0)], v_src=(u, U_BV, 0), heads=B_HEADS,
                        scale=B_SCALE, sink=b_sink[l], heads_per_step=B_HEADS),
            "na": dict(q_srcs=[(u, U_CQ_)], k_srcs=[(u, U_CK, 1)], v_src=(u, U_CV, 1), heads=C_HEADS,
                       scale=C_SCALE, heads_per_step=C_HEADS),
            "gqa": dict(q_srcs=[(ur, R_DQ)], k_srcs=[(ur, R_DK, D_HEADS // D_KV_HEADS)],
                        v_src=(u, U_DV, D_HEADS // D_KV_HEADS), heads=D_HEADS, scale=D_SCALE),
        }
        windows = {"swa": dict(window=(B_TQ, B_BACK, B_WK), bias=band_bias),
                   "na": dict(window=(C_TQ, C_BACK, C_WK), bias=neighbourhood_bias_table(c_rpb[l], rows_grid))}
        branches = []
        for mixer, cfg in mixers.items():
            o = attention(mode="lat", out_rows=rows_out, name=mixer + "_attention", **cfg, **windows.get(mixer, {}),
                          **dims)
            if need_ctx:
                o = attention(mode="ctx", out_prev=o, name=mixer + "_attention_ctx", **cfg, **dims)
            branches.append(o)

        xs = merge_out_ln(branches, _branch_weights(w_branch[l]), gates, w_out[l].astype(BF16), xs, mod, 2, group_of,
                          ln1_g[l], ln1_b[l], alpha)

        pq = token_matmul(xs, peer_w_q[l].astype(BF16), prologue="modulate", mod=mod, mod_idx=(3, 4),
                          group_of=group_of, out_dtype=BF16, name="peer_query")
        keys = peer_sub_keys[l].reshape(2 * PEER_HEADS, PEER_KEYS, PEER_QDIM // 2).astype(BF16)
        cnt, e0, rank, e1 = peer_topk(pq, keys)
        ht = modulate_transposed(xs, mod, (3, 4), group_of)
        n_exp = peer_u.shape[1]
        vt = jnp.transpose(peer_v[l].astype(BF16).reshape(n_exp // PEER_EB, PEER_EB, d), (0, 2, 1))
        ffn_t = peer_dense(ht, peer_u[l].astype(BF16), vt, cnt, e0, rank, e1)
        xs = residual_ln(xs, ffn_t, mod, 5, group_of, ln2_g[l], ln2_b[l], alpha)

    return xs[:t_lat].reshape(bsz, seq, d)
```

```python
import functools
import math

import jax
import jax.numpy as jnp
import numpy as np
from jax import lax
from jax.experimental import pallas as pl
from jax.experimental.pallas import tpu as pltpu

F32 = jnp.float32
BF16 = jnp.bfloat16

GRID_W = 64
ROPE_THETA = 10000.0
LN_EPS = 1e-5
RMS_EPS = 1e-6
NEG_INF = -1e30

A_HEADS, A_NOPE, A_ROPE, A_V, A_Q_LORA, A_KV_LORA = 4, 128, 64, 128, 384, 128
B_HEADS, B_KV_HEADS, B_DIM, B_WINDOW = 8, 2, 64, 128
C_HEADS, C_DIM, NA_ROWS, NA_COLS = 4, 128, 8, 16
D_HEADS, D_KV_HEADS, D_DIM = 4, 2, 128
N_BRANCH, BRANCH_W = 4, 512
PEER_HEADS, PEER_KEYS, PEER_QDIM, PEER_TOPK = 8, 128, 256, 16

A_SCALE = (A_NOPE + A_ROPE) ** -0.5
B_SCALE = B_DIM ** -0.5
C_SCALE = C_DIM ** -0.5
D_SCALE = D_DIM ** -0.5

IN_SIZES = (A_Q_LORA, A_KV_LORA, A_ROPE,
            B_HEADS * B_DIM, B_KV_HEADS * B_DIM, B_KV_HEADS * B_DIM,
            C_HEADS * C_DIM, C_HEADS * C_DIM, C_HEADS * C_DIM,
            D_HEADS * D_DIM, D_KV_HEADS * D_DIM, D_KV_HEADS * D_DIM)
LANE = 128
BF16_ROWS = 16

VMEM_LIMIT = 56 * 1024 * 1024

PEER_CHUNK = 512
PEER_EB = 1024
PEER_TOPK_TILE = 256

B_TQ, B_BACK, B_WK = 512, 128, 768
C_TQ, C_BACK, C_WK = 4 * GRID_W, 4 * GRID_W, 12 * GRID_W
DENSE_TQ = 1024
DENSE_SUB = 4
ROPE_TM = 512

R_BQ, R_KR, R_BK, R_DQ, R_DK = 0, 8, 9, 10, 14
U_PREP_N, U_PREP_R32, U_PREP_RMSQ = 16, 10, 4
U_CQ_, U_CK, U_CV, U_CKV, U_BV, U_CQ, U_DV, U_BLOCKS = 16, 20, 24, 28, 29, 30, 33, 36


def _cparams(sem):
    return pltpu.CompilerParams(dimension_semantics=sem, vmem_limit_bytes=VMEM_LIMIT)


def _pick(n, cands):
    for c in cands:
        if n % c == 0:
            return c
    raise ValueError(f"no tile in {cands} divides {n}")


def _dot(a, b):
    return jnp.dot(a, b, preferred_element_type=F32)


def _dot_nt(a, b):
    return lax.dot_general(a, b, (((1,), (1,)), ((), ())), preferred_element_type=F32)


def _ada_kernel(c_ref, w_ref, b_ref, o_ref):
    c = c_ref[...]
    s = c * (1.0 / (1.0 + jnp.exp(-c)))
    w = w_ref[0]
    s_hi = s.astype(BF16)
    s_lo = (s - s_hi.astype(F32)).astype(BF16)
    w_hi = w.astype(BF16)
    w_lo = (w - w_hi.astype(F32)).astype(BF16)
    acc = _dot(s_hi, w_hi) + _dot(s_lo, w_hi) + _dot(s_hi, w_lo)
    o_ref[...] = acc + b_ref[...]


def ada_modulation(cc, w_all, b, layer):
    r, k = cc.shape
    n = w_all.shape[2]
    tn = _pick(n, (512, 256, 128))
    return pl.pallas_call(
        _ada_kernel,
        grid=(n // tn,),
        in_specs=[pl.BlockSpec((r, k), lambda j: (0, 0)),
                  pl.BlockSpec((1, k, tn), lambda j: (layer, 0, j)),
                  pl.BlockSpec((1, tn), lambda j: (0, j))],
        out_specs=pl.BlockSpec((r, tn), lambda j: (0, j)),
        out_shape=jax.ShapeDtypeStruct((r, n), F32),
        compiler_params=_cparams(("arbitrary",)),
        name="ada_modulation",
    )(cc, w_all, b.reshape(1, n))


def _mm_kernel(*refs, prologue, has_bias, act):
    it = iter(refs)
    x_ref = next(it)
    if prologue == "modulate":
        sh_ref, sc_ref = next(it), next(it)
    elif prologue == "rmsnorm":
        g_ref = next(it)
    w_ref = next(it)
    b_ref = next(it) if has_bias else None
    o_ref = next(it)
    a_scr = next(it)

    @pl.when(pl.program_id(1) == 0)
    def _():
        x = x_ref[...].astype(F32)
        if prologue == "modulate":
            a = x * (1.0 + sc_ref[0]) + sh_ref[0]
        elif prologue == "rmsnorm":
            ms = jnp.mean(x * x, axis=-1, keepdims=True)
            a = (x * lax.rsqrt(ms + RMS_EPS)) * g_ref[...]
        else:
            a = x
        a_scr[...] = a.astype(BF16)

    acc = _dot(a_scr[...], w_ref[...])
    if has_bias:
        acc = acc + b_ref[...]
    if act == "sigmoid":
        acc = 1.0 / (1.0 + jnp.exp(-acc))
    o_ref[...] = acc.astype(o_ref.dtype)


def token_matmul(x, w, *, rows=None, x_cols=None, prologue=None, mod=None, mod_idx=None, group_of=None,
                 norm_g=None, bias=None, act=None, out_dtype=F32, name="token_matmul"):
    rows = x.shape[0] if rows is None else rows
    kdim, n = w.shape
    kw, kblk = (x.shape[1], 0) if x_cols is None else x_cols
    assert kw == kdim
    tm = _pick(rows, (1024, 512, 256))
    tn = _pick(n, (2048, 1536, 1024, 512, 256, 128))
    in_specs = [pl.BlockSpec((tm, kdim), lambda i, j: (i, kblk))]
    args = [x]
    if prologue == "modulate":
        tm_group = group_of(tm)
        for slot in mod_idx:
            in_specs.append(pl.BlockSpec((1, 1, kdim), lambda i, j, s=slot: (tm_group(i) * 6 + s, 0, 0)))
            args.append(mod)
    elif prologue == "rmsnorm":
        in_specs.append(pl.BlockSpec((1, kdim), lambda i, j: (0, 0)))
        args.append(norm_g.reshape(1, kdim))
    in_specs.append(pl.BlockSpec((kdim, tn), lambda i, j: (0, j)))
    args.append(w)
    if bias is not None:
        in_specs.append(pl.BlockSpec((1, tn), lambda i, j: (0, j)))
        args.append(bias.reshape(1, n))
    return pl.pallas_call(
        functools.partial(_mm_kernel, prologue=prologue, has_bias=bias is not None, act=act),
        grid=(rows // tm, n // tn),
        in_specs=in_specs,
        out_specs=pl.BlockSpec((tm, tn), lambda i, j: (i, j)),
        out_shape=jax.ShapeDtypeStruct((rows, n), out_dtype),
        scratch_shapes=[pltpu.VMEM((tm, kdim), BF16)],
        compiler_params=_cparams(("parallel", "arbitrary")),
        name=name,
    )(*args)


def _res_ln_kernel(x_ref, yt_ref, gate_ref, g_ref, b_ref, o_ref, *, alpha):
    z = alpha * x_ref[...] + gate_ref[0] * yt_ref[0].T
    mu = jnp.mean(z, axis=-1, keepdims=True)
    zc = z - mu
    var = jnp.mean(zc * zc, axis=-1, keepdims=True)
    o_ref[...] = (zc * lax.rsqrt(var + LN_EPS)) * g_ref[...] + b_ref[...]


def residual_ln(x, yt, mod, gate_slot, group_of, ln_g, ln_b, alpha):
    nch, d, tm = yt.shape
    rows = nch * tm
    tm_group = group_of(tm)
    return pl.pallas_call(
        functools.partial(_res_ln_kernel, alpha=alpha),
        grid=(nch,),
        in_specs=[pl.BlockSpec((tm, d), lambda i: (i, 0)),
                  pl.BlockSpec((1, d, tm), lambda i: (i, 0, 0)),
                  pl.BlockSpec((1, 1, d), lambda i: (tm_group(i) * 6 + gate_slot, 0, 0)),
                  pl.BlockSpec((1, d), lambda i: (0, 0)),
                  pl.BlockSpec((1, d), lambda i: (0, 0))],
        out_specs=pl.BlockSpec((tm, d), lambda i: (i, 0)),
        out_shape=jax.ShapeDtypeStruct((rows, d), F32),
        compiler_params=_cparams(("parallel",)),
        name="residual_ln",
    )(x, yt, mod, ln_g.reshape(1, d), ln_b.reshape(1, d))


def _qk_prep_kernel(x_ref, c32_ref, sa_ref, sb_ref, c64_ref, s64_ref, gq_ref, gk_ref, o_ref, *, n_r32, n_rms_q):
    for j in range(x_ref.shape[1] // LANE):
        cols = slice(j * LANE, (j + 1) * LANE)
        x = x_ref[:, cols].astype(F32)
        if j < n_r32:
            y = x * c32_ref[...] + pltpu.roll(x, 32, 1) * sa_ref[...] + pltpu.roll(x, 96, 1) * sb_ref[...]
        else:
            g = gq_ref[...] if j < n_r32 + n_rms_q else gk_ref[...]
            ms = jnp.mean(x * x, axis=-1, keepdims=True)
            y = (x * lax.rsqrt(ms + RMS_EPS)) * g
            y = y * c64_ref[...] + pltpu.roll(y, 64, 1) * s64_ref[...]
        o_ref[:, cols] = y.astype(o_ref.dtype)


def qk_prep(x, group_idx, n_blocks, n_r32, n_rms_q, tables, gq, gk, lat_rows, seq, name):
    rows = x.shape[0]
    tm = ROPE_TM
    width = n_blocks * LANE
    lat_blocks, per_seq = lat_rows // tm, seq // tm
    tab_spec = pl.BlockSpec((tm, LANE), lambda i: (jnp.where(i < lat_blocks, i % per_seq, per_seq), 0))
    vec_spec = pl.BlockSpec((1, LANE), lambda i: (0, 0))
    return pl.pallas_call(
        functools.partial(_qk_prep_kernel, n_r32=n_r32, n_rms_q=n_rms_q),
        grid=(rows // tm,),
        in_specs=[pl.BlockSpec((tm, width), lambda i: (i, group_idx))] + [tab_spec] * 5 + [vec_spec] * 2,
        out_specs=pl.BlockSpec((tm, width), lambda i: (i, 0)),
        out_shape=jax.ShapeDtypeStruct((rows, width), BF16),
        compiler_params=_cparams(("parallel",)),
        name=name,
    )(x, *tables, gq.reshape(1, LANE), gk.reshape(1, LANE))


def _attn_kernel(*refs, n_parts, has_lat, window, has_bias, bias_per_head, has_sink, scale, n_lat, n_sub, hps,
                 per_head):
    it = iter(refs)
    q_refs = [next(it) for _ in range(n_parts)]
    if has_lat:
        kl_refs = [next(it) for _ in range(n_parts)]
        vl_ref = next(it)
    kc_refs = [next(it) for _ in range(n_parts)]
    vc_ref = next(it)
    bias_ref = next(it) if has_bias else None
    sink_ref = next(it) if has_sink else None
    refs = list(it)
    o_ref = refs[-1]

    if has_lat and window is not None:
        tq, back, wk = window
        start = pl.multiple_of(jnp.clip(pl.program_id(2) * tq - back, 0, n_lat - wk), LANE)
        key_rows = pl.ds(start, wk)
    else:
        key_rows = slice(None)
    sub = o_ref.shape[0] // n_sub
    for hh in range(hps):
        lanes = slice(hh * LANE, (hh + 1) * LANE)

        def cols(j):
            return lanes if per_head[j] else slice(None)

        kcs = [k[:, cols(j)] for j, k in enumerate(kc_refs)]
        vc = vc_ref[:, cols(n_parts)]
        if has_lat:
            kls = [k[key_rows, cols(j)] for j, k in enumerate(kl_refs)]
            vl = vl_ref[key_rows, cols(n_parts)]
        if has_sink:
            sink = sink_ref[pl.program_id(1) * hps + hh]
        for sb in range(n_sub):
            rows = slice(sb * sub, (sb + 1) * sub)
            qs = [q[rows, lanes] for q in q_refs]
            s2 = functools.reduce(jnp.add, [_dot_nt(q, k) for q, k in zip(qs, kcs)]) * scale
            m = jnp.max(s2, axis=-1, keepdims=True)
            if has_lat:
                s1 = functools.reduce(jnp.add, [_dot_nt(q, k) for q, k in zip(qs, kls)]) * scale
                if has_bias:
                    s1 = s1 + bias_ref[0, hh if bias_per_head else 0, rows, :]
                m = jnp.maximum(m, jnp.max(s1, axis=-1, keepdims=True))
            if has_sink:
                m = jnp.maximum(m, sink)
            p2 = jnp.exp(s2 - m)
            l = jnp.sum(p2, axis=-1, keepdims=True)
            o = _dot(p2.astype(BF16), vc)
            if has_lat:
                p1 = jnp.exp(s1 - m)
                l = l + jnp.sum(p1, axis=-1, keepdims=True)
                o = o + _dot(p1.astype(BF16), vl)
            if has_sink:
                l = l + jnp.exp(sink - m)
            o_ref[rows, lanes] = (o / l).astype(o_ref.dtype)


def attention(q_srcs, k_srcs, v_src, *, heads, scale, bsz, seq, nctx, t_lat, mode, window=None, bias=None, sink=None,
              heads_per_step=1, name="attention"):
    lat = mode == "lat"
    hps = heads_per_step
    n_parts = len(q_srcs)
    tq = (window[0] if window is not None else DENSE_TQ) if lat else nctx
    nqb = (seq if lat else nctx) // tq
    ctx0 = t_lat // nctx
    kv_srcs = list(k_srcs) + [v_src]
    per_head = tuple(grp == 1 for _, _, grp in kv_srcs)
    assert heads % hps == 0 and all(first % hps == 0 for _, first in q_srcs)
    assert hps == 1 or all(grp == 0 or (grp == 1 and first % hps == 0) for _, first, grp in kv_srcs)

    def kv_spec(rows, row_blk, first, grp):
        if hps > 1 and grp == 1:
            return pl.BlockSpec((rows, hps * LANE), lambda bi, hg, i: (row_blk(bi), first // hps + hg))
        return pl.BlockSpec((rows, LANE), lambda bi, hg, i: (row_blk(bi), first + (hg // grp if grp else 0)))

    def q_row(bi, i):
        return bi * nqb + i if lat else t_lat // tq + bi

    in_specs, args = [], []
    for arr, first in q_srcs:
        in_specs.append(pl.BlockSpec((tq, hps * LANE), lambda bi, hg, i, f=first: (q_row(bi, i), f // hps + hg)))
        args.append(arr)
    if lat:
        for arr, first, grp in kv_srcs:
            in_specs.append(kv_spec(seq, lambda bi: bi, first, grp))
            args.append(arr)
    for arr, first, grp in kv_srcs:
        in_specs.append(kv_spec(nctx, lambda bi: ctx0 + bi, first, grp))
        args.append(arr)
    bias_per_head = bias is not None and bias.shape[1] != 1
    if bias is not None:
        in_specs.append(pl.BlockSpec(
            (1, hps if bias_per_head else 1, tq, window[2]),
            lambda bi, hg, i: (jnp.where(i == 0, 0, jnp.where(i == nqb - 1, 2, 1)), hg if bias_per_head else 0, 0, 0)))
        args.append(bias)
    if sink is not None:
        in_specs.append(pl.BlockSpec(memory_space=pltpu.SMEM))
        args.append(sink.astype(F32))
    if window is not None:
        assert seq % tq == 0 and nqb >= 2 and seq >= window[2]
    return pl.pallas_call(
        functools.partial(_attn_kernel, n_parts=n_parts, has_lat=lat, window=window, has_bias=bias is not None,
                          bias_per_head=bias_per_head, has_sink=sink is not None, scale=scale, n_lat=seq,
                          n_sub=DENSE_SUB if (lat and window is None) else 1, hps=hps, per_head=per_head),
        grid=(bsz, heads // hps, nqb),
        in_specs=in_specs,
        out_specs=pl.BlockSpec((tq, hps * LANE), lambda bi, hg, i: (bi * nqb + i, hg)),
        out_shape=jax.ShapeDtypeStruct((bsz * nqb * tq, heads * LANE), BF16),
        compiler_params=_cparams(("parallel", "parallel", "arbitrary")),
        name=name,
    )(*args)


def _window_q_offsets(geom):
    tq, back, wk = geom
    return (0, back, wk - tq)


def band_bias_table():
    tq, _, wk = (B_TQ, B_BACK, B_WK)
    kpos = np.arange(wk)[None, :]
    tabs = []
    for off in _window_q_offsets((B_TQ, B_BACK, B_WK)):
        qpos = off + np.arange(tq)[:, None]
        tabs.append(np.where(np.abs(qpos - kpos) <= B_WINDOW, 0.0, NEG_INF))
    return jnp.asarray(np.stack(tabs)[:, None], F32)


def neighbourhood_bias_table(rpb, rows):
    tq, back, wk = (C_TQ, C_BACK, C_WK)
    assert rows >= wk // GRID_W and rows >= NA_ROWS
    nqb = rows * GRID_W // tq
    nq_rows, nk_rows = tq // GRID_W, wk // GRID_W
    cols = np.arange(GRID_W)
    col_start = np.clip(cols - NA_COLS // 2, 0, GRID_W - NA_COLS)
    col_ok = (cols[None, :] >= col_start[:, None]) & (cols[None, :] < col_start[:, None] + NA_COLS)
    col_off = np.clip(cols[None, :] - cols[:, None] + (NA_COLS - 1), 0, 2 * NA_COLS - 2)
    onehot_c = jnp.asarray(col_off[..., None] == np.arange(2 * NA_COLS - 1), F32)
    tabs = []
    for cls, off in enumerate(_window_q_offsets((C_TQ, C_BACK, C_WK))):
        blk = (0, 1, nqb - 1)[cls]
        w0 = int(np.clip(blk * tq - back, 0, rows * GRID_W - wk)) // GRID_W
        assert w0 * GRID_W + off == blk * tq and off % GRID_W == 0
        q_row = w0 + off // GRID_W + np.arange(nq_rows)
        k_row = w0 + np.arange(nk_rows)
        row_start = np.clip(q_row - NA_ROWS // 2, 0, rows - NA_ROWS)
        row_ok = (k_row[None, :] >= row_start[:, None]) & (k_row[None, :] < row_start[:, None] + NA_ROWS)
        row_off = np.clip(k_row[None, :] - q_row[:, None] + (NA_ROWS - 1), 0, 2 * NA_ROWS - 2)
        by_row = rpb[:, row_off, :].astype(F32)
        vals = jnp.einsum("hrsc,qkc->hrqsk", by_row, onehot_c, precision=lax.Precision.HIGHEST)
        ok = row_ok[:, None, :, None] & col_ok[None, :, None, :]
        tabs.append(jnp.where(jnp.asarray(ok)[None], vals, NEG_INF).reshape(rpb.shape[0], tq, wk))
    return jnp.stack(tabs)


def _merge_out_ln_kernel(*refs, alpha, n_branch, lat_blocks):
    o_refs = refs[:n_branch]
    refs = refs[n_branch:]
    if lat_blocks is not None:
        oc_refs = refs[:n_branch]
        refs = refs[n_branch:]
        is_ctx = pl.program_id(0) >= lat_blocks
    wbr_refs = refs[:n_branch]
    g_ref, wout_ref, x_ref, gate_ref, lng_ref, lnb_ref, out_ref = refs[n_branch:]
    d = x_ref.shape[1]
    y = None
    for r, (o_ref, wbr_ref) in enumerate(zip(o_refs, wbr_refs)):
        o = o_ref[...]
        if lat_blocks is not None:
            o = jnp.where(is_ctx, oc_refs[r][...], o)
        t = g_ref[:, r * d:(r + 1) * d].astype(F32) * _dot(o, wbr_ref[...])
        y = t if y is None else y + t
    mix = _dot(y.astype(BF16), wout_ref[...])
    z = alpha * x_ref[...] + gate_ref[0] * mix
    mu = jnp.mean(z, axis=-1, keepdims=True)
    zc = z - mu
    var = jnp.mean(zc * zc, axis=-1, keepdims=True)
    out_ref[...] = (zc * lax.rsqrt(var + LN_EPS)) * lng_ref[...] + lnb_ref[...]


def merge_out_ln(branches, branches_ctx, w_branch, gates, w_out, x, mod, gate_slot, group_of, ln_g, ln_b, alpha):
    rows = gates.shape[0]
    nbr = len(branches)
    d = w_out.shape[0]
    tm = _pick(rows, (256,))
    tm_group = group_of(tm)
    resident = pl.Buffered(1)
    lat_blocks = branches[0].shape[0] // tm
    last = lat_blocks - 1
    branch_specs = [pl.BlockSpec((tm, o.shape[1]), lambda i: (jnp.minimum(i, last), 0)) for o in branches]
    if branches_ctx is not None:
        assert rows == branches[0].shape[0] + branches_ctx[0].shape[0]
        branch_specs += [pl.BlockSpec((tm, o.shape[1]), lambda i: (jnp.maximum(i - lat_blocks, 0), 0))
                         for o in branches_ctx]
    else:
        assert rows == branches[0].shape[0]
    return pl.pallas_call(
        functools.partial(_merge_out_ln_kernel, alpha=alpha, n_branch=nbr,
                          lat_blocks=None if branches_ctx is None else lat_blocks),
        grid=(rows // tm,),
        in_specs=branch_specs + [
                  pl.BlockSpec(w.shape, lambda i: (0, 0), pipeline_mode=resident) for w in w_branch] + [
                  pl.BlockSpec((tm, nbr * d), lambda i: (i, 0)),
                  pl.BlockSpec((d, d), lambda i: (0, 0), pipeline_mode=resident),
                  pl.BlockSpec((tm, d), lambda i: (i, 0)),
                  pl.BlockSpec((1, 1, d), lambda i: (tm_group(i) * 6 + gate_slot, 0, 0)),
                  pl.BlockSpec((1, d), lambda i: (0, 0)),
                  pl.BlockSpec((1, d), lambda i: (0, 0))],
        out_specs=pl.BlockSpec((tm, d), lambda i: (i, 0)),
        out_shape=jax.ShapeDtypeStruct((rows, d), F32),
        compiler_params=_cparams(("parallel",)),
        name="merge_out_ln",
    )(*branches, *(branches_ctx or []), *w_branch, gates, w_out, x, mod, ln_g.reshape(1, d), ln_b.reshape(1, d))


def _peer_pairs():
    return [(a, b) for a in range(PEER_TOPK) for b in range(PEER_TOPK) if (a + 1) * (b + 1) <= PEER_TOPK]


def _peer_topk_kernel(q_ref, keys_ref, cnt_ref, e0_ref, rank_ref, e1_ref, s0_scr):
    tops = [[[None] * PEER_HEADS for _ in range(PEER_TOPK)] for _ in range(2)]
    for h in range(PEER_HEADS):
        for p in range(2):
            hp = 2 * h + p
            qs = q_ref[:, hp * PEER_KEYS:(hp + 1) * PEER_KEYS]
            s = _dot_nt(keys_ref[hp], qs)
            work = s
            rank = jnp.full(s.shape, float(PEER_TOPK), F32)
            for r in range(PEER_TOPK):
                m = jnp.max(work, axis=0, keepdims=True)
                tops[p][r][h] = m
                if p == 1:
                    hit = work == m
                    rank = jnp.where(hit, float(r), rank)
                    work = jnp.where(hit, -jnp.inf, work)
                elif r + 1 < PEER_TOPK:
                    work = jnp.where(work == m, -jnp.inf, work)
            if p == 0:
                s0_scr[h] = s
            else:
                rank_ref[h, 0] = rank.astype(BF16)
                e1_ref[h, 0] = jnp.exp(s - tops[1][0][h]).astype(BF16)
    top = [[jnp.concatenate(tops[p][r], axis=0) for r in range(PEER_TOPK)] for p in range(2)]
    cands = [top[0][a] + top[1][b] for a, b in _peer_pairs()]
    work = list(cands)
    for r in range(PEER_TOPK):
        m = functools.reduce(jnp.maximum, work)
        if r + 1 < PEER_TOPK:
            work = [jnp.where(w == m, -jnp.inf, w) for w in work]
    tau = m
    cmax = cands[0]
    z = functools.reduce(jnp.add, [jnp.where(c >= tau, jnp.exp(c - cmax), 0.0) for c in cands])
    rz = 1.0 / z
    for h in range(PEER_HEADS):
        s0 = s0_scr[h]
        tau_h = tau[h:h + 1]
        cnt = None
        for b in range(PEER_TOPK):
            inc = jnp.where(s0 + top[1][b][h:h + 1] >= tau_h, 1.0, 0.0)
            cnt = inc if cnt is None else cnt + inc
        cnt_ref[h, 0] = cnt
        e0_ref[h, 0] = jnp.exp(s0 - top[0][0][h:h + 1]) * rz[h:h + 1]


def peer_topk(q, keys):
    rows = q.shape[0]
    tt = PEER_TOPK_TILE
    per_chunk = PEER_CHUNK // tt
    nch = rows // PEER_CHUNK
    big = pl.BlockSpec((PEER_HEADS, 1, PEER_KEYS, tt), lambda i: (0, i // per_chunk, 0, i % per_chunk))
    shape_f32 = jax.ShapeDtypeStruct((PEER_HEADS, nch, PEER_KEYS, PEER_CHUNK), F32)
    shape_bf16 = jax.ShapeDtypeStruct((PEER_HEADS, nch, PEER_KEYS, PEER_CHUNK), BF16)
    return pl.pallas_call(
        _peer_topk_kernel,
        grid=(rows // tt,),
        in_specs=[pl.BlockSpec((tt, q.shape[1]), lambda i: (i, 0)),
                  pl.BlockSpec(keys.shape, lambda i: (0, 0, 0))],
        out_specs=[big, big, big, big],
        out_shape=[shape_f32, shape_f32, shape_bf16, shape_bf16],
        scratch_shapes=[pltpu.VMEM((PEER_HEADS, PEER_KEYS, tt), F32)],
        compiler_params=_cparams(("parallel",)),
        name="peer_topk",
    )(q, keys)


def _modulate_t_kernel(x_ref, sh_ref, sc_ref, o_ref):
    h = x_ref[...] * (1.0 + sc_ref[0]) + sh_ref[0]
    o_ref[0] = h.T.astype(BF16)


def modulate_transposed(x, mod, mod_idx, group_of):
    rows, d = x.shape
    tt = PEER_CHUNK
    tt_group = group_of(tt)
    mod_specs = [pl.BlockSpec((1, 1, d), lambda i, s=slot: (tt_group(i) * 6 + s, 0, 0)) for slot in mod_idx]
    return pl.pallas_call(
        _modulate_t_kernel,
        grid=(rows // tt,),
        in_specs=[pl.BlockSpec((tt, d), lambda i: (i, 0))] + mod_specs,
        out_specs=pl.BlockSpec((1, d, tt), lambda i: (i, 0, 0)),
        out_shape=jax.ShapeDtypeStruct((rows // tt, d, tt), BF16),
        compiler_params=_cparams(("parallel",)),
        name="modulate_transposed",
    )(x, mod, mod)


def _peer_dense_kernel(ht_ref, u_ref, vt_ref, cnt_ref, e0_ref, rank_ref, e1_ref, o_ref, a_scr, w_scr, p_scr, *, n_e,
                       n_pairs):
    s = pl.program_id(0)
    e = s % n_e
    cur = s % 2
    prev = 1 - cur

    @pl.when(jnp.logical_and(s > 0, (s - 1) % n_e == 0))
    def _():
        o_ref[...] = jnp.zeros_like(o_ref)
    i_per_eb = PEER_EB // PEER_KEYS
    tc = w_scr.shape[-1]

    def gate_and_down():
        for ii in range(i_per_eb):
            sl = slice(ii * PEER_KEYS, (ii + 1) * PEER_KEYS)
            a = a_scr[prev, sl, :]
            act = 0.5 * a * (1.0 + lax.erf(a * (1.0 / math.sqrt(2.0))))
            p_scr[sl, :] = act.astype(BF16) * w_scr[prev, sl, :]
        o_ref[0] += _dot(vt_ref[0], p_scr[...])

    def up_and_weights():
        a_scr[cur] = _dot(u_ref[...], ht_ref[0])
        for ii in range(i_per_eb):
            i = e * i_per_eb + ii
            cnt = [jnp.broadcast_to(cnt_ref[h, 0, pl.ds(i, 1), :], (BF16_ROWS, tc)).astype(BF16)
                   for h in range(PEER_HEADS)]
            e0 = [jnp.broadcast_to(e0_ref[h, 0, pl.ds(i, 1), :], (BF16_ROWS, tc)).astype(BF16)
                  for h in range(PEER_HEADS)]
            for g in range(PEER_KEYS // BF16_ROWS):
                rows = slice(g * BF16_ROWS, (g + 1) * BF16_ROWS)
                w = None
                for h in range(PEER_HEADS):
                    e1 = e1_ref[h, 0, rows, :]
                    t = jnp.where(rank_ref[h, 0, rows, :] < cnt[h], e1, jnp.zeros_like(e1)) * e0[h]
                    w = t if w is None else w + t
                w_scr[cur, ii * PEER_KEYS + g * BF16_ROWS:ii * PEER_KEYS + (g + 1) * BF16_ROWS, :] = w

    @pl.when(s == 0)
    def _():
        up_and_weights()

    @pl.when(jnp.logical_and(s > 0, s < n_pairs))
    def _():
        gate_and_down()
        up_and_weights()

    @pl.when(s == n_pairs)
    def _():
        gate_and_down()


def peer_dense(ht, u, vt, cnt, e0, rank, e1):
    nch, d, tc = ht.shape
    n_e = u.shape[0] // PEER_EB
    n_pairs = nch * n_e

    def cur_chunk(s):
        return jnp.minimum(s, n_pairs - 1) // n_e

    def prev_chunk(s):
        return jnp.maximum(s - 1, 0) // n_e

    sel = pl.BlockSpec((PEER_HEADS, 1, PEER_KEYS, tc), lambda s: (0, cur_chunk(s), 0, 0))
    return pl.pallas_call(
        functools.partial(_peer_dense_kernel, n_e=n_e, n_pairs=n_pairs),
        grid=(n_pairs + 1,),
        in_specs=[pl.BlockSpec((1, d, tc), lambda s: (cur_chunk(s), 0, 0)),
                  pl.BlockSpec((PEER_EB, d), lambda s: (jnp.minimum(s, n_pairs - 1) % n_e, 0)),
                  pl.BlockSpec((1, d, PEER_EB), lambda s: (jnp.maximum(s - 1, 0) % n_e, 0, 0)),
                  sel, sel, sel, sel],
        out_specs=pl.BlockSpec((1, d, tc), lambda s: (prev_chunk(s), 0, 0)),
        out_shape=jax.ShapeDtypeStruct((nch, d, tc), F32),
        scratch_shapes=[pltpu.VMEM((2, PEER_EB, tc), F32), pltpu.VMEM((2, PEER_EB, tc), BF16),
                        pltpu.VMEM((PEER_EB, tc), BF16)],
        compiler_params=_cparams(("arbitrary",)),
        name="peer_dense",
    )(ht, u, vt, cnt, e0, rank, e1)


def _axial_angles(n, rot_dim):
    t = jnp.arange(n)
    rows = (t // GRID_W).astype(F32)
    cols = (t % GRID_W).astype(F32)
    nf = rot_dim // 4
    inv = ROPE_THETA ** (-jnp.arange(nf, dtype=F32) / nf)
    ang = jnp.concatenate([rows[:, None] * inv, cols[:, None] * inv], axis=-1)
    return jnp.cos(ang), jnp.sin(ang)


def _rope_tables(seq):
    assert A_ROPE == B_DIM == LANE // 2 and D_DIM == LANE

    def pad(t, fill):
        return jnp.concatenate([t, jnp.full((ROPE_TM, LANE), fill, F32)], axis=0)

    cos, sin = _axial_angles(seq, B_DIM)
    z = jnp.zeros_like(sin)
    c32 = jnp.concatenate([cos, cos, cos, cos], axis=1)
    sa = jnp.concatenate([z, sin, z, sin], axis=1)
    sb = jnp.concatenate([-sin, z, -sin, z], axis=1)
    cos, sin = _axial_angles(seq, D_DIM)
    c64 = jnp.concatenate([cos, cos], axis=1)
    s64 = jnp.concatenate([-sin, sin], axis=1)
    return [pad(c32, 1.0), pad(sa, 0.0), pad(sb, 0.0), pad(c64, 1.0), pad(s64, 0.0)]


def _half_block(t, first_half, axis):
    z = jnp.zeros_like(t)
    return jnp.concatenate([t, z] if first_half else [z, t], axis=axis)


def _in_proj_weight(w_in):
    d = w_in.shape[0]
    o = np.cumsum((0,) + IN_SIZES)
    part = [w_in[:, o[i]:o[i + 1]] for i in range(len(IN_SIZES))]
    per_kv = B_HEADS // B_KV_HEADS
    bq = [_half_block(part[3][:, h * B_DIM:(h + 1) * B_DIM], h // per_kv == 0, 1) for h in range(B_HEADS)]
    cols = bq + [_half_block(part[2], True, 1), part[4], part[9], part[10], part[6], part[7], part[8], part[1],
                 part[5], part[0], part[11], jnp.zeros((d, LANE), w_in.dtype)]
    out = jnp.concatenate(cols, axis=1).astype(BF16)
    assert out.shape[1] == U_BLOCKS * LANE
    return out


def _mla_up_weights(w_uq, w_ukv):
    dq, dkv = A_NOPE + A_ROPE, A_NOPE + A_V
    q_nope = [w_uq[:, h * dq:h * dq + A_NOPE] for h in range(A_HEADS)]
    q_rope = [_half_block(w_uq[:, h * dq + A_NOPE:(h + 1) * dq], True, 1) for h in range(A_HEADS)]
    k_nope = [w_ukv[:, h * dkv:h * dkv + A_NOPE] for h in range(A_HEADS)]
    v = [w_ukv[:, h * dkv + A_NOPE:(h + 1) * dkv] for h in range(A_HEADS)]
    return (jnp.concatenate(q_nope + q_rope, axis=1).astype(BF16), jnp.concatenate(k_nope + v, axis=1).astype(BF16))


def _branch_weights(w_branch):
    per_kv = B_HEADS // B_KV_HEADS
    wb = jnp.concatenate([_half_block(w_branch[1][h * B_DIM:(h + 1) * B_DIM], h // per_kv == 0, 0)
                          for h in range(B_HEADS)], axis=0)
    return [w_branch[0].astype(BF16), wb.astype(BF16), w_branch[2].astype(BF16), w_branch[3].astype(BF16)]


def kernel(x, c, ctx, c_ctx, w_ada, b_ada, w_in, a_q_norm, a_kv_norm, a_w_uq, a_w_ukv, b_sink, c_rpb, d_q_norm, d_k_norm, w_branch, w_gate, b_gate, w_out, ln1_g, ln1_b, ln2_g, ln2_b, peer_w_q, peer_sub_keys, peer_u, peer_v):
    bsz, seq, d = x.shape
    nctx = ctx.shape[1]
    depth = w_in.shape[0]
    t_lat, t_ctx = bsz * seq, bsz * nctx
    t_all = t_lat + t_ctx
    rows_grid = seq // GRID_W
    alpha = (2 * depth) ** 0.25

    tables = _rope_tables(seq)
    band_bias = band_bias_table()
    dims = dict(bsz=bsz, seq=seq, nctx=nctx, t_lat=t_lat)

    n_groups = -(-(bsz + 1) // 8) * 8
    cc = jnp.zeros((n_groups, d), F32).at[:bsz].set(c).at[bsz].set(c_ctx)

    def group_of(tm):
        lat_blocks, per_batch = t_lat // tm, seq // tm
        assert t_lat % tm == 0 and seq % tm == 0
        return lambda i: jnp.where(i < lat_blocks, i // per_batch, bsz)

    xs = jnp.concatenate([x.reshape(t_lat, d), ctx.reshape(t_ctx, d)], axis=0)

    for l in range(depth):
        need_ctx = l < depth - 1
        rows_out = t_all if need_ctx else t_lat
        mod = ada_modulation(cc, w_ada, b_ada[l], l).reshape(n_groups * 6, 1, d)

        u = token_matmul(xs, _in_proj_weight(w_in[l]), prologue="modulate", mod=mod, mod_idx=(0, 1),
                         group_of=group_of, out_dtype=BF16, name="in_proj")
        w_gate_l = jnp.transpose(w_gate[l], (1, 0, 2)).reshape(d, N_BRANCH * d).astype(BF16)
        gates = token_matmul(xs, w_gate_l, rows=rows_out, prologue="modulate", mod=mod, mod_idx=(0, 1),
                             group_of=group_of, bias=b_gate[l].reshape(-1), act="sigmoid", out_dtype=BF16,
                             name="branch_gates")

        w_uq, w_ukv = _mla_up_weights(a_w_uq[l], a_w_ukv[l])
        qa = token_matmul(u, w_uq, x_cols=(A_Q_LORA, U_CQ * LANE // A_Q_LORA), prologue="rmsnorm", norm_g=a_q_norm[l], out_dtype=BF16,
                          name="mla_q_up")
        kva = token_matmul(u, w_ukv, x_cols=(A_KV_LORA, U_CKV), prologue="rmsnorm", norm_g=a_kv_norm[l],
                           out_dtype=BF16, name="mla_kv_up")
        ur = qk_prep(u, 0, U_PREP_N, U_PREP_R32, U_PREP_RMSQ, tables, d_q_norm[l], d_k_norm[l], t_lat, seq, "qk_prep")
        qr = qk_prep(qa, 1, A_HEADS, A_HEADS, 0, tables, d_q_norm[l], d_k_norm[l], t_lat, seq, "mla_q_rope")
        mixers = {
            "mla": dict(q_srcs=[(qa, 0), (qr, 0)], k_srcs=[(kva, 0, 1), (ur, R_KR, 0)], v_src=(kva, A_HEADS, 1),
                        heads=A_HEADS, scale=A_SCALE),
            "swa": dict(q_srcs=[(ur, R_BQ)], k_srcs=[(ur, R_BK, 0)], v_src=(u, U_BV, 0), heads=B_HEADS,
                        scale=B_SCALE, sink=b_sink[l], heads_per_step=B_HEADS),
            "na": dict(q_srcs=[(u, U_CQ_)], k_srcs=[(u, U_CK, 1)], v_src=(u, U_CV, 1), heads=C_HEADS,
                       scale=C_SCALE, heads_per_step=C_HEADS),
            "gqa": dict(q_srcs=[(ur, R_DQ)], k_srcs=[(ur, R_DK, D_HEADS // D_KV_HEADS)],
                        v_src=(u, U_DV, D_HEADS // D_KV_HEADS), heads=D_HEADS, scale=D_SCALE),
        }
        windows = {"swa": dict(window=(B_TQ, B_BACK, B_WK), bias=band_bias),
                   "na": dict(window=(C_TQ, C_BACK, C_WK), bias=neighbourhood_bias_table(c_rpb[l], rows_grid))}
        branches = [attention(mode="lat", name=mixer + "_attention", **cfg, **windows.get(mixer, {}), **dims)
                    for mixer, cfg in mixers.items()]
        branches_ctx = [attention(mode="ctx", name=mixer + "_attention_ctx", **cfg, **dims)
                        for mixer, cfg in mixers.items()] if need_ctx else None

        xs = merge_out_ln(branches, branches_ctx, _branch_weights(w_branch[l]), gates, w_out[l].astype(BF16), xs, mod,
                          2, group_of, ln1_g[l], ln1_b[l], alpha)

        pq = token_matmul(xs, peer_w_q[l].astype(BF16), prologue="modulate", mod=mod, mod_idx=(3, 4),
                          group_of=group_of, out_dtype=BF16, name="peer_query")
        keys = peer_sub_keys[l].reshape(2 * PEER_HEADS, PEER_KEYS, PEER_QDIM // 2).astype(BF16)
        cnt, e0, rank, e1 = peer_topk(pq, keys)
        ht = modulate_transposed(xs, mod, (3, 4), group_of)
        n_exp = peer_u.shape[1]
        vt = jnp.transpose(peer_v[l].astype(BF16).reshape(n_exp // PEER_EB, PEER_EB, d), (0, 2, 1))
        ffn_t = peer_dense(ht, peer_u[l].astype(BF16), vt, cnt, e0, rank, e1)
        xs = residual_ln(xs, ffn_t, mod, 5, group_of, ln2_g[l], ln2_b[l], alpha)

    return xs[:t_lat].reshape(bsz, seq, d)
```

```python
import functools
import math

import jax
import jax.numpy as jnp
import numpy as np
from jax import lax
from jax.experimental import pallas as pl
from jax.experimental.pallas import tpu as pltpu

F32 = jnp.float32
BF16 = jnp.bfloat16

GRID_W = 64
ROPE_THETA = 10000.0
LN_EPS = 1e-5
RMS_EPS = 1e-6
NEG_INF = -1e30

A_HEADS, A_NOPE, A_ROPE, A_V, A_Q_LORA, A_KV_LORA = 4, 128, 64, 128, 384, 128
B_HEADS, B_KV_HEADS, B_DIM, B_WINDOW = 8, 2, 64, 128
C_HEADS, C_DIM, NA_ROWS, NA_COLS = 4, 128, 8, 16
D_HEADS, D_KV_HEADS, D_DIM = 4, 2, 128
N_BRANCH, BRANCH_W = 4, 512
PEER_HEADS, PEER_KEYS, PEER_QDIM, PEER_TOPK = 8, 128, 256, 16

A_SCALE = (A_NOPE + A_ROPE) ** -0.5
B_SCALE = B_DIM ** -0.5
C_SCALE = C_DIM ** -0.5
D_SCALE = D_DIM ** -0.5

IN_SIZES = (A_Q_LORA, A_KV_LORA, A_ROPE,
            B_HEADS * B_DIM, B_KV_HEADS * B_DIM, B_KV_HEADS * B_DIM,
            C_HEADS * C_DIM, C_HEADS * C_DIM, C_HEADS * C_DIM,
            D_HEADS * D_DIM, D_KV_HEADS * D_DIM, D_KV_HEADS * D_DIM)
LANE = 128
BF16_ROWS = 16

VMEM_LIMIT = 56 * 1024 * 1024

PEER_CHUNK = 512
PEER_EB = 1024
PEER_TOPK_TILE = 256

B_TQ, B_BACK, B_WK = 512, 128, 768
C_TQ, C_BACK, C_WK = 4 * GRID_W, 4 * GRID_W, 12 * GRID_W
DENSE_TQ = 1024
DENSE_SUB = 4
ROPE_TM = 512

R_BQ, R_KR, R_BK, R_DQ, R_DK = 0, 8, 9, 10, 14
U_PREP_N, U_PREP_R32, U_PREP_RMSQ = 16, 10, 4
U_CQ_, U_CK, U_CV, U_CKV, U_BV, U_CQ, U_DV, U_BLOCKS = 16, 20, 24, 28, 29, 30, 33, 36


def _cparams(sem):
    return pltpu.CompilerParams(dimension_semantics=sem, vmem_limit_bytes=VMEM_LIMIT)


def _pick(n, cands):
    for c in cands:
        if n % c == 0:
            return c
    raise ValueError(f"no tile in {cands} divides {n}")


def _dot(a, b):
    return jnp.dot(a, b, preferred_element_type=F32)


def _dot_nt(a, b):
    return lax.dot_general(a, b, (((1,), (1,)), ((), ())), preferred_element_type=F32)


def _ada_kernel(c_ref, w_ref, b_ref, o_ref):
    c = c_ref[...]
    s = c * (1.0 / (1.0 + jnp.exp(-c)))
    w = w_ref[0]
    s_hi = s.astype(BF16)
    s_lo = (s - s_hi.astype(F32)).astype(BF16)
    w_hi = w.astype(BF16)
    w_lo = (w - w_hi.astype(F32)).astype(BF16)
    acc = _dot(s_hi, w_hi) + _dot(s_lo, w_hi) + _dot(s_hi, w_lo)
    o_ref[...] = acc + b_ref[...]


def ada_modulation(cc, w_all, b, layer):
    r, k = cc.shape
    n = w_all.shape[2]
    tn = _pick(n, (512, 256, 128))
    return pl.pallas_call(
        _ada_kernel,
        grid=(n // tn,),
        in_specs=[pl.BlockSpec((r, k), lambda j: (0, 0)),
                  pl.BlockSpec((1, k, tn), lambda j: (layer, 0, j)),
                  pl.BlockSpec((1, tn), lambda j: (0, j))],
        out_specs=pl.BlockSpec((r, tn), lambda j: (0, j)),
        out_shape=jax.ShapeDtypeStruct((r, n), F32),
        compiler_params=_cparams(("arbitrary",)),
        name="ada_modulation",
    )(cc, w_all, b.reshape(1, n))


def _mm_kernel(*refs, prologue, has_bias, act):
    it = iter(refs)
    x_ref = next(it)
    if prologue == "modulate":
        sh_ref, sc_ref = next(it), next(it)
    elif prologue == "rmsnorm":
        g_ref = next(it)
    w_ref = next(it)
    b_ref = next(it) if has_bias else None
    o_ref = next(it)
    a_scr = next(it)

    @pl.when(pl.program_id(1) == 0)
    def _():
        x = x_ref[...].astype(F32)
        if prologue == "modulate":
            a = x * (1.0 + sc_ref[0]) + sh_ref[0]
        elif prologue == "rmsnorm":
            ms = jnp.mean(x * x, axis=-1, keepdims=True)
            a = (x * lax.rsqrt(ms + RMS_EPS)) * g_ref[...]
        else:
            a = x
        a_scr[...] = a.astype(BF16)

    acc = _dot(a_scr[...], w_ref[...])
    if has_bias:
        acc = acc + b_ref[...]
    if act == "sigmoid":
        acc = 1.0 / (1.0 + jnp.exp(-acc))
    o_ref[...] = acc.astype(o_ref.dtype)


def token_matmul(x, w, *, rows=None, x_cols=None, prologue=None, mod=None, mod_idx=None, group_of=None,
                 norm_g=None, bias=None, act=None, out_dtype=F32, name="token_matmul"):
    rows = x.shape[0] if rows is None else rows
    kdim, n = w.shape
    kw, kblk = (x.shape[1], 0) if x_cols is None else x_cols
    assert kw == kdim
    tm = _pick(rows, (1024, 512, 256))
    tn = _pick(n, (2048, 1536, 1024, 512, 256, 128))
    in_specs = [pl.BlockSpec((tm, kdim), lambda i, j: (i, kblk))]
    args = [x]
    if prologue == "modulate":
        tm_group = group_of(tm)
        for slot in mod_idx:
            in_specs.append(pl.BlockSpec((1, 1, kdim), lambda i, j, s=slot: (tm_group(i) * 6 + s, 0, 0)))
            args.append(mod)
    elif prologue == "rmsnorm":
        in_specs.append(pl.BlockSpec((1, kdim), lambda i, j: (0, 0)))
        args.append(norm_g.reshape(1, kdim))
    in_specs.append(pl.BlockSpec((kdim, tn), lambda i, j: (0, j)))
    args.append(w)
    if bias is not None:
        in_specs.append(pl.BlockSpec((1, tn), lambda i, j: (0, j)))
        args.append(bias.reshape(1, n))
    return pl.pallas_call(
        functools.partial(_mm_kernel, prologue=prologue, has_bias=bias is not None, act=act),
        grid=(rows // tm, n // tn),
        in_specs=in_specs,
        out_specs=pl.BlockSpec((tm, tn), lambda i, j: (i, j)),
        out_shape=jax.ShapeDtypeStruct((rows, n), out_dtype),
        scratch_shapes=[pltpu.VMEM((tm, kdim), BF16)],
        compiler_params=_cparams(("parallel", "arbitrary")),
        name=name,
    )(*args)


def _res_ln_kernel(x_ref, yt_ref, gate_ref, g_ref, b_ref, o_ref, *, alpha):
    z = alpha * x_ref[...] + gate_ref[0] * yt_ref[0].T
    mu = jnp.mean(z, axis=-1, keepdims=True)
    zc = z - mu
    var = jnp.mean(zc * zc, axis=-1, keepdims=True)
    o_ref[...] = (zc * lax.rsqrt(var + LN_EPS)) * g_ref[...] + b_ref[...]


def residual_ln(x, yt, mod, gate_slot, group_of, ln_g, ln_b, alpha):
    nch, d, tm = yt.shape
    rows = nch * tm
    tm_group = group_of(tm)
    return pl.pallas_call(
        functools.partial(_res_ln_kernel, alpha=alpha),
        grid=(nch,),
        in_specs=[pl.BlockSpec((tm, d), lambda i: (i, 0)),
                  pl.BlockSpec((1, d, tm), lambda i: (i, 0, 0)),
                  pl.BlockSpec((1, 1, d), lambda i: (tm_group(i) * 6 + gate_slot, 0, 0)),
                  pl.BlockSpec((1, d), lambda i: (0, 0)),
                  pl.BlockSpec((1, d), lambda i: (0, 0))],
        out_specs=pl.BlockSpec((tm, d), lambda i: (i, 0)),
        out_shape=jax.ShapeDtypeStruct((rows, d), F32),
        compiler_params=_cparams(("parallel",)),
        name="residual_ln",
    )(x, yt, mod, ln_g.reshape(1, d), ln_b.reshape(1, d))


def _qk_prep_kernel(x_ref, c32_ref, sa_ref, sb_ref, c64_ref, s64_ref, gq_ref, gk_ref, o_ref, *, n_r32, n_rms_q):
    for j in range(x_ref.shape[1] // LANE):
        cols = slice(j * LANE, (j + 1) * LANE)
        x = x_ref[:, cols].astype(F32)
        if j < n_r32:
            y = x * c32_ref[...] + pltpu.roll(x, 32, 1) * sa_ref[...] + pltpu.roll(x, 96, 1) * sb_ref[...]
        else:
            g = gq_ref[...] if j < n_r32 + n_rms_q else gk_ref[...]
            ms = jnp.mean(x * x, axis=-1, keepdims=True)
            y = (x * lax.rsqrt(ms + RMS_EPS)) * g
            y = y * c64_ref[...] + pltpu.roll(y, 64, 1) * s64_ref[...]
        o_ref[:, cols] = y.astype(o_ref.dtype)


def qk_prep(x, group_idx, n_blocks, n_r32, n_rms_q, tables, gq, gk, lat_rows, seq, name):
    rows = x.shape[0]
    tm = ROPE_TM
    width = n_blocks * LANE
    lat_blocks, per_seq = lat_rows // tm, seq // tm
    tab_spec = pl.BlockSpec((tm, LANE), lambda i: (jnp.where(i < lat_blocks, i % per_seq, per_seq), 0))
    vec_spec = pl.BlockSpec((1, LANE), lambda i: (0, 0))
    return pl.pallas_call(
        functools.partial(_qk_prep_kernel, n_r32=n_r32, n_rms_q=n_rms_q),
        grid=(rows // tm,),
        in_specs=[pl.BlockSpec((tm, width), lambda i: (i, group_idx))] + [tab_spec] * 5 + [vec_spec] * 2,
        out_specs=pl.BlockSpec((tm, width), lambda i: (i, 0)),
        out_shape=jax.ShapeDtypeStruct((rows, width), BF16),
        compiler_params=_cparams(("parallel",)),
        name=name,
    )(x, *tables, gq.reshape(1, LANE), gk.reshape(1, LANE))


def _attn_kernel(*refs, n_parts, has_lat, window, has_bias, bias_per_head, has_sink, scale, n_lat, n_sub, hps,
                 per_head):
    it = iter(refs)
    q_refs = [next(it) for _ in range(n_parts)]
    if has_lat:
        kl_refs = [next(it) for _ in range(n_parts)]
        vl_ref = next(it)
    kc_refs = [next(it) for _ in range(n_parts)]
    vc_ref = next(it)
    bias_ref = next(it) if has_bias else None
    sink_ref = next(it) if has_sink else None
    refs = list(it)
    o_ref = refs[-1]

    if has_lat and window is not None:
        tq, back, wk = window
        start = pl.multiple_of(jnp.clip(pl.program_id(2) * tq - back, 0, n_lat - wk), LANE)
        key_rows = pl.ds(start, wk)
    else:
        key_rows = slice(None)
    sub = o_ref.shape[0] // n_sub
    for hh in range(hps):
        lanes = slice(hh * LANE, (hh + 1) * LANE)

        def cols(j):
            return lanes if per_head[j] else slice(None)

        kcs = [jnp.concatenate([k[:, cols(j)] for j, k in enumerate(kc_refs)], axis=1)]
        vc = vc_ref[:, cols(n_parts)]
        if has_lat:
            kls = [jnp.concatenate([k[key_rows, cols(j)] for j, k in enumerate(kl_refs)], axis=1)]
            vl = vl_ref[key_rows, cols(n_parts)]
        if has_sink:
            sink = sink_ref[pl.program_id(1) * hps + hh]
        for sb in range(n_sub):
            rows = slice(sb * sub, (sb + 1) * sub)
            qs = [jnp.concatenate([q[rows, lanes] for q in q_refs], axis=1)]
            s2 = functools.reduce(jnp.add, [_dot_nt(q, k) for q, k in zip(qs, kcs)]) * scale
            m = jnp.max(s2, axis=-1, keepdims=True)
            if has_lat:
                s1 = functools.reduce(jnp.add, [_dot_nt(q, k) for q, k in zip(qs, kls)]) * scale
                if has_bias:
                    s1 = s1 + bias_ref[0, hh if bias_per_head else 0, rows, :]
                m = jnp.maximum(m, jnp.max(s1, axis=-1, keepdims=True))
            if has_sink:
                m = jnp.maximum(m, sink)
            p2 = jnp.exp(s2 - m)
            l = jnp.sum(p2, axis=-1, keepdims=True)
            o = _dot(p2.astype(BF16), vc)
            if has_lat:
                p1 = jnp.exp(s1 - m)
                l = l + jnp.sum(p1, axis=-1, keepdims=True)
                o = o + _dot(p1.astype(BF16), vl)
            if has_sink:
                l = l + jnp.exp(sink - m)
            o_ref[rows, lanes] = (o / l).astype(o_ref.dtype)


def attention(q_srcs, k_srcs, v_src, *, heads, scale, bsz, seq, nctx, t_lat, mode, window=None, bias=None, sink=None,
              heads_per_step=1, name="attention"):
    lat = mode == "lat"
    hps = heads_per_step
    n_parts = len(q_srcs)
    tq = (window[0] if window is not None else DENSE_TQ) if lat else nctx
    nqb = (seq if lat else nctx) // tq
    ctx0 = t_lat // nctx
    kv_srcs = list(k_srcs) + [v_src]
    per_head = tuple(grp == 1 for _, _, grp in kv_srcs)
    assert heads % hps == 0 and all(first % hps == 0 for _, first in q_srcs)
    assert hps == 1 or all(grp == 0 or (grp == 1 and first % hps == 0) for _, first, grp in kv_srcs)

    def kv_spec(rows, row_blk, first, grp):
        if hps > 1 and grp == 1:
            return pl.BlockSpec((rows, hps * LANE), lambda bi, hg, i: (row_blk(bi), first // hps + hg))
        return pl.BlockSpec((rows, LANE), lambda bi, hg, i: (row_blk(bi), first + (hg // grp if grp else 0)))

    def q_row(bi, i):
        return bi * nqb + i if lat else t_lat // tq + bi

    in_specs, args = [], []
    for arr, first in q_srcs:
        in_specs.append(pl.BlockSpec((tq, hps * LANE), lambda bi, hg, i, f=first: (q_row(bi, i), f // hps + hg)))
        args.append(arr)
    if lat:
        for arr, first, grp in kv_srcs:
            in_specs.append(kv_spec(seq, lambda bi: bi, first, grp))
            args.append(arr)
    for arr, first, grp in kv_srcs:
        in_specs.append(kv_spec(nctx, lambda bi: ctx0 + bi, first, grp))
        args.append(arr)
    bias_per_head = bias is not None and bias.shape[1] != 1
    if bias is not None:
        in_specs.append(pl.BlockSpec(
            (1, hps if bias_per_head else 1, tq, window[2]),
            lambda bi, hg, i: (jnp.where(i == 0, 0, jnp.where(i == nqb - 1, 2, 1)), hg if bias_per_head else 0, 0, 0)))
        args.append(bias)
    if sink is not None:
        in_specs.append(pl.BlockSpec(memory_space=pltpu.SMEM))
        args.append(sink.astype(F32))
    if window is not None:
        assert seq % tq == 0 and nqb >= 2 and seq >= window[2]
    return pl.pallas_call(
        functools.partial(_attn_kernel, n_parts=n_parts, has_lat=lat, window=window, has_bias=bias is not None,
                          bias_per_head=bias_per_head, has_sink=sink is not None, scale=scale, n_lat=seq,
                          n_sub=DENSE_SUB if (lat and window is None) else 1, hps=hps, per_head=per_head),
        grid=(bsz, heads // hps, nqb),
        in_specs=in_specs,
        out_specs=pl.BlockSpec((tq, hps * LANE), lambda bi, hg, i: (bi * nqb + i, hg)),
        out_shape=jax.ShapeDtypeStruct((bsz * nqb * tq, heads * LANE), BF16),
        compiler_params=_cparams(("parallel", "parallel", "arbitrary")),
        name=name,
    )(*args)


def _window_q_offsets(geom):
    tq, back, wk = geom
    return (0, back, wk - tq)


def band_bias_table():
    tq, _, wk = (B_TQ, B_BACK, B_WK)
    kpos = np.arange(wk)[None, :]
    tabs = []
    for off in _window_q_offsets((B_TQ, B_BACK, B_WK)):
        qpos = off + np.arange(tq)[:, None]
        tabs.append(np.where(np.abs(qpos - kpos) <= B_WINDOW, 0.0, NEG_INF))
    return jnp.asarray(np.stack(tabs)[:, None], F32)


def neighbourhood_bias_table(rpb, rows):
    tq, back, wk = (C_TQ, C_BACK, C_WK)
    assert rows >= wk // GRID_W and rows >= NA_ROWS
    nqb = rows * GRID_W // tq
    nq_rows, nk_rows = tq // GRID_W, wk // GRID_W
    cols = np.arange(GRID_W)
    col_start = np.clip(cols - NA_COLS // 2, 0, GRID_W - NA_COLS)
    col_ok = (cols[None, :] >= col_start[:, None]) & (cols[None, :] < col_start[:, None] + NA_COLS)
    col_off = np.clip(cols[None, :] - cols[:, None] + (NA_COLS - 1), 0, 2 * NA_COLS - 2)
    onehot_c = jnp.asarray(col_off[..., None] == np.arange(2 * NA_COLS - 1), F32)
    tabs = []
    for cls, off in enumerate(_window_q_offsets((C_TQ, C_BACK, C_WK))):
        blk = (0, 1, nqb - 1)[cls]
        w0 = int(np.clip(blk * tq - back, 0, rows * GRID_W - wk)) // GRID_W
        assert w0 * GRID_W + off == blk * tq and off % GRID_W == 0
        q_row = w0 + off // GRID_W + np.arange(nq_rows)
        k_row = w0 + np.arange(nk_rows)
        row_start = np.clip(q_row - NA_ROWS // 2, 0, rows - NA_ROWS)
        row_ok = (k_row[None, :] >= row_start[:, None]) & (k_row[None, :] < row_start[:, None] + NA_ROWS)
        row_off = np.clip(k_row[None, :] - q_row[:, None] + (NA_ROWS - 1), 0, 2 * NA_ROWS - 2)
        by_row = rpb[:, row_off, :].astype(F32)
        vals = jnp.einsum("hrsc,qkc->hrqsk", by_row, onehot_c, precision=lax.Precision.HIGHEST)
        ok = row_ok[:, None, :, None] & col_ok[None, :, None, :]
        tabs.append(jnp.where(jnp.asarray(ok)[None], vals, NEG_INF).reshape(rpb.shape[0], tq, wk))
    return jnp.stack(tabs)


def _merge_out_ln_kernel(*refs, alpha, n_branch, lat_blocks):
    o_refs = refs[:n_branch]
    refs = refs[n_branch:]
    if lat_blocks is not None:
        oc_refs = refs[:n_branch]
        refs = refs[n_branch:]
        is_ctx = pl.program_id(0) >= lat_blocks
    wbr_refs = refs[:n_branch]
    g_ref, wout_ref, x_ref, gate_ref, lng_ref, lnb_ref, out_ref = refs[n_branch:]
    d = x_ref.shape[1]
    y = None
    for r, (o_ref, wbr_ref) in enumerate(zip(o_refs, wbr_refs)):
        o = o_ref[...]
        if lat_blocks is not None:
            o = jnp.where(is_ctx, oc_refs[r][...], o)
        t = g_ref[:, r * d:(r + 1) * d].astype(F32) * _dot(o, wbr_ref[...])
        y = t if y is None else y + t
    mix = _dot(y.astype(BF16), wout_ref[...])
    z = alpha * x_ref[...] + gate_ref[0] * mix
    mu = jnp.mean(z, axis=-1, keepdims=True)
    zc = z - mu
    var = jnp.mean(zc * zc, axis=-1, keepdims=True)
    out_ref[...] = (zc * lax.rsqrt(var + LN_EPS)) * lng_ref[...] + lnb_ref[...]


def merge_out_ln(branches, branches_ctx, w_branch, gates, w_out, x, mod, gate_slot, group_of, ln_g, ln_b, alpha):
    rows = gates.shape[0]
    nbr = len(branches)
    d = w_out.shape[0]
    tm = _pick(rows, (256,))
    tm_group = group_of(tm)
    resident = pl.Buffered(1)
    lat_blocks = branches[0].shape[0] // tm
    last = lat_blocks - 1
    branch_specs = [pl.BlockSpec((tm, o.shape[1]), lambda i: (jnp.minimum(i, last), 0)) for o in branches]
    if branches_ctx is not None:
        assert rows == branches[0].shape[0] + branches_ctx[0].shape[0]
        branch_specs += [pl.BlockSpec((tm, o.shape[1]), lambda i: (jnp.maximum(i - lat_blocks, 0), 0))
                         for o in branches_ctx]
    else:
        assert rows == branches[0].shape[0]
    return pl.pallas_call(
        functools.partial(_merge_out_ln_kernel, alpha=alpha, n_branch=nbr,
                          lat_blocks=None if branches_ctx is None else lat_blocks),
        grid=(rows // tm,),
        in_specs=branch_specs + [
                  pl.BlockSpec(w.shape, lambda i: (0, 0), pipeline_mode=resident) for w in w_branch] + [
                  pl.BlockSpec((tm, nbr * d), lambda i: (i, 0)),
                  pl.BlockSpec((d, d), lambda i: (0, 0), pipeline_mode=resident),
                  pl.BlockSpec((tm, d), lambda i: (i, 0)),
                  pl.BlockSpec((1, 1, d), lambda i: (tm_group(i) * 6 + gate_slot, 0, 0)),
                  pl.BlockSpec((1, d), lambda i: (0, 0)),
                  pl.BlockSpec((1, d), lambda i: (0, 0))],
        out_specs=pl.BlockSpec((tm, d), lambda i: (i, 0)),
        out_shape=jax.ShapeDtypeStruct((rows, d), F32),
        compiler_params=_cparams(("parallel",)),
        name="merge_out_ln",
    )(*branches, *(branches_ctx or []), *w_branch, gates, w_out, x, mod, ln_g.reshape(1, d), ln_b.reshape(1, d))


def _peer_pairs():
    return [(a, b) for a in range(PEER_TOPK) for b in range(PEER_TOPK) if (a + 1) * (b + 1) <= PEER_TOPK]


def _peer_topk_kernel(q_ref, keys_ref, cnt_ref, e0_ref, rank_ref, e1_ref, s0_scr):
    tops = [[[None] * PEER_HEADS for _ in range(PEER_TOPK)] for _ in range(2)]
    for h in range(PEER_HEADS):
        for p in range(2):
            hp = 2 * h + p
            qs = q_ref[:, hp * PEER_KEYS:(hp + 1) * PEER_KEYS]
            s = _dot_nt(keys_ref[hp], qs)
            work = s
            rank = jnp.full(s.shape, float(PEER_TOPK), F32)
            for r in range(PEER_TOPK):
                m = jnp.max(work, axis=0, keepdims=True)
                tops[p][r][h] = m
                if p == 1:
                    hit = work == m
                    rank = jnp.where(hit, float(r), rank)
                    work = jnp.where(hit, -jnp.inf, work)
                elif r + 1 < PEER_TOPK:
                    work = jnp.where(work == m, -jnp.inf, work)
            if p == 0:
                s0_scr[h] = s
            else:
                rank_ref[h, 0] = rank.astype(BF16)
                e1_ref[h, 0] = jnp.exp(s - tops[1][0][h]).astype(BF16)
    top = [[jnp.concatenate(tops[p][r], axis=0) for r in range(PEER_TOPK)] for p in range(2)]
    cands = [top[0][a] + top[1][b] for a, b in _peer_pairs()]
    work = list(cands)
    for r in range(PEER_TOPK):
        m = functools.reduce(jnp.maximum, work)
        if r + 1 < PEER_TOPK:
            work = [jnp.where(w == m, -jnp.inf, w) for w in work]
    tau = m
    cmax = cands[0]
    z = functools.reduce(jnp.add, [jnp.where(c >= tau, jnp.exp(c - cmax), 0.0) for c in cands])
    rz = 1.0 / z
    for h in range(PEER_HEADS):
        s0 = s0_scr[h]
        tau_h = tau[h:h + 1]
        cnt = None
        for b in range(PEER_TOPK):
            inc = jnp.where(s0 + top[1][b][h:h + 1] >= tau_h, 1.0, 0.0)
            cnt = inc if cnt is None else cnt + inc
        cnt_ref[h, 0] = cnt
        e0_ref[h, 0] = jnp.exp(s0 - top[0][0][h:h + 1]) * rz[h:h + 1]


def peer_topk(q, keys):
    rows = q.shape[0]
    tt = PEER_TOPK_TILE
    per_chunk = PEER_CHUNK // tt
    nch = rows // PEER_CHUNK
    big = pl.BlockSpec((PEER_HEADS, 1, PEER_KEYS, tt), lambda i: (0, i // per_chunk, 0, i % per_chunk))
    shape_f32 = jax.ShapeDtypeStruct((PEER_HEADS, nch, PEER_KEYS, PEER_CHUNK), F32)
    shape_bf16 = jax.ShapeDtypeStruct((PEER_HEADS, nch, PEER_KEYS, PEER_CHUNK), BF16)
    return pl.pallas_call(
        _peer_topk_kernel,
        grid=(rows // tt,),
        in_specs=[pl.BlockSpec((tt, q.shape[1]), lambda i: (i, 0)),
                  pl.BlockSpec(keys.shape, lambda i: (0, 0, 0))],
        out_specs=[big, big, big, big],
        out_shape=[shape_f32, shape_f32, shape_bf16, shape_bf16],
        scratch_shapes=[pltpu.VMEM((PEER_HEADS, PEER_KEYS, tt), F32)],
        compiler_params=_cparams(("parallel",)),
        name="peer_topk",
    )(q, keys)


def _modulate_t_kernel(x_ref, sh_ref, sc_ref, o_ref):
    h = x_ref[...] * (1.0 + sc_ref[0]) + sh_ref[0]
    o_ref[0] = h.T.astype(BF16)


def modulate_transposed(x, mod, mod_idx, group_of):
    rows, d = x.shape
    tt = PEER_CHUNK
    tt_group = group_of(tt)
    mod_specs = [pl.BlockSpec((1, 1, d), lambda i, s=slot: (tt_group(i) * 6 + s, 0, 0)) for slot in mod_idx]
    return pl.pallas_call(
        _modulate_t_kernel,
        grid=(rows // tt,),
        in_specs=[pl.BlockSpec((tt, d), lambda i: (i, 0))] + mod_specs,
        out_specs=pl.BlockSpec((1, d, tt), lambda i: (i, 0, 0)),
        out_shape=jax.ShapeDtypeStruct((rows // tt, d, tt), BF16),
        compiler_params=_cparams(("parallel",)),
        name="modulate_transposed",
    )(x, mod, mod)


def _peer_dense_kernel(ht_ref, u_ref, vt_ref, cnt_ref, e0_ref, rank_ref, e1_ref, o_ref, a_scr, w_scr, p_scr, *, n_e,
                       n_pairs):
    s = pl.program_id(0)
    e = s % n_e
    cur = s % 2
    prev = 1 - cur

    @pl.when(jnp.logical_and(s > 0, (s - 1) % n_e == 0))
    def _():
        o_ref[...] = jnp.zeros_like(o_ref)
    i_per_eb = PEER_EB // PEER_KEYS
    tc = w_scr.shape[-1]

    def gate_and_down():
        for ii in range(i_per_eb):
            sl = slice(ii * PEER_KEYS, (ii + 1) * PEER_KEYS)
            a = a_scr[prev, sl, :]
            act = 0.5 * a * (1.0 + lax.erf(a * (1.0 / math.sqrt(2.0))))
            p_scr[sl, :] = act.astype(BF16) * w_scr[prev, sl, :]
        o_ref[0] += _dot(vt_ref[0], p_scr[...])

    def up_and_weights():
        a_scr[cur] = _dot(u_ref[...], ht_ref[0])
        for ii in range(i_per_eb):
            i = e * i_per_eb + ii
            cnt = [jnp.broadcast_to(cnt_ref[h, 0, pl.ds(i, 1), :], (BF16_ROWS, tc)).astype(BF16)
                   for h in range(PEER_HEADS)]
            e0 = [jnp.broadcast_to(e0_ref[h, 0, pl.ds(i, 1), :], (BF16_ROWS, tc)).astype(BF16)
                  for h in range(PEER_HEADS)]
            for g in range(PEER_KEYS // BF16_ROWS):
                rows = slice(g * BF16_ROWS, (g + 1) * BF16_ROWS)
                w = None
                for h in range(PEER_HEADS):
                    e1 = e1_ref[h, 0, rows, :]
                    t = jnp.where(rank_ref[h, 0, rows, :] < cnt[h], e1, jnp.zeros_like(e1)) * e0[h]
                    w = t if w is None else w + t
                w_scr[cur, ii * PEER_KEYS + g * BF16_ROWS:ii * PEER_KEYS + (g + 1) * BF16_ROWS, :] = w

    @pl.when(s == 0)
    def _():
        up_and_weights()

    @pl.when(jnp.logical_and(s > 0, s < n_pairs))
    def _():
        gate_and_down()
        up_and_weights()

    @pl.when(s == n_pairs)
    def _():
        gate_and_down()


def peer_dense(ht, u, vt, cnt, e0, rank, e1):
    nch, d, tc = ht.shape
    n_e = u.shape[0] // PEER_EB
    n_pairs = nch * n_e

    def cur_chunk(s):
        return jnp.minimum(s, n_pairs - 1) // n_e

    def prev_chunk(s):
        return jnp.maximum(s - 1, 0) // n_e

    sel = pl.BlockSpec((PEER_HEADS, 1, PEER_KEYS, tc), lambda s: (0, cur_chunk(s), 0, 0))
    return pl.pallas_call(
        functools.partial(_peer_dense_kernel, n_e=n_e, n_pairs=n_pairs),
        grid=(n_pairs + 1,),
        in_specs=[pl.BlockSpec((1, d, tc), lambda s: (cur_chunk(s), 0, 0)),
                  pl.BlockSpec((PEER_EB, d), lambda s: (jnp.minimum(s, n_pairs - 1) % n_e, 0)),
                  pl.BlockSpec((1, d, PEER_EB), lambda s: (jnp.maximum(s - 1, 0) % n_e, 0, 0)),
                  sel, sel, sel, sel],
        out_specs=pl.BlockSpec((1, d, tc), lambda s: (prev_chunk(s), 0, 0)),
        out_shape=jax.ShapeDtypeStruct((nch, d, tc), F32),
        scratch_shapes=[pltpu.VMEM((2, PEER_EB, tc), F32), pltpu.VMEM((2, PEER_EB, tc), BF16),
                        pltpu.VMEM((PEER_EB, tc), BF16)],
        compiler_params=_cparams(("arbitrary",)),
        name="peer_dense",
    )(ht, u, vt, cnt, e0, rank, e1)


def _axial_angles(n, rot_dim):
    t = jnp.arange(n)
    rows = (t // GRID_W).astype(F32)
    cols = (t % GRID_W).astype(F32)
    nf = rot_dim // 4
    inv = ROPE_THETA ** (-jnp.arange(nf, dtype=F32) / nf)
    ang = jnp.concatenate([rows[:, None] * inv, cols[:, None] * inv], axis=-1)
    return jnp.cos(ang), jnp.sin(ang)


def _rope_tables(seq):
    assert A_ROPE == B_DIM == LANE // 2 and D_DIM == LANE

    def pad(t, fill):
        return jnp.concatenate([t, jnp.full((ROPE_TM, LANE), fill, F32)], axis=0)

    cos, sin = _axial_angles(seq, B_DIM)
    z = jnp.zeros_like(sin)
    c32 = jnp.concatenate([cos, cos, cos, cos], axis=1)
    sa = jnp.concatenate([z, sin, z, sin], axis=1)
    sb = jnp.concatenate([-sin, z, -sin, z], axis=1)
    cos, sin = _axial_angles(seq, D_DIM)
    c64 = jnp.concatenate([cos, cos], axis=1)
    s64 = jnp.concatenate([-sin, sin], axis=1)
    return [pad(c32, 1.0), pad(sa, 0.0), pad(sb, 0.0), pad(c64, 1.0), pad(s64, 0.0)]


def _half_block(t, first_half, axis):
    z = jnp.zeros_like(t)
    return jnp.concatenate([t, z] if first_half else [z, t], axis=axis)


def _in_proj_weight(w_in):
    d = w_in.shape[0]
    o = np.cumsum((0,) + IN_SIZES)
    part = [w_in[:, o[i]:o[i + 1]] for i in range(len(IN_SIZES))]
    per_kv = B_HEADS // B_KV_HEADS
    bq = [_half_block(part[3][:, h * B_DIM:(h + 1) * B_DIM], h // per_kv == 0, 1) for h in range(B_HEADS)]
    cols = bq + [_half_block(part[2], True, 1), part[4], part[9], part[10], part[6], part[7], part[8], part[1],
                 part[5], part[0], part[11], jnp.zeros((d, LANE), w_in.dtype)]
    out = jnp.concatenate(cols, axis=1).astype(BF16)
    assert out.shape[1] == U_BLOCKS * LANE
    return out


def _mla_up_weights(w_uq, w_ukv):
    dq, dkv = A_NOPE + A_ROPE, A_NOPE + A_V
    q_nope = [w_uq[:, h * dq:h * dq + A_NOPE] for h in range(A_HEADS)]
    q_rope = [_half_block(w_uq[:, h * dq + A_NOPE:(h + 1) * dq], True, 1) for h in range(A_HEADS)]
    k_nope = [w_ukv[:, h * dkv:h * dkv + A_NOPE] for h in range(A_HEADS)]
    v = [w_ukv[:, h * dkv + A_NOPE:(h + 1) * dkv] for h in range(A_HEADS)]
    return (jnp.concatenate(q_nope + q_rope, axis=1).astype(BF16), jnp.concatenate(k_nope + v, axis=1).astype(BF16))


def _branch_weights(w_branch):
    per_kv = B_HEADS // B_KV_HEADS
    wb = jnp.concatenate([_half_block(w_branch[1][h * B_DIM:(h + 1) * B_DIM], h // per_kv == 0, 0)
                          for h in range(B_HEADS)], axis=0)
    return [w_branch[0].astype(BF16), wb.astype(BF16), w_branch[2].astype(BF16), w_branch[3].astype(BF16)]


def kernel(x, c, ctx, c_ctx, w_ada, b_ada, w_in, a_q_norm, a_kv_norm, a_w_uq, a_w_ukv, b_sink, c_rpb, d_q_norm, d_k_norm, w_branch, w_gate, b_gate, w_out, ln1_g, ln1_b, ln2_g, ln2_b, peer_w_q, peer_sub_keys, peer_u, peer_v):
    bsz, seq, d = x.shape
    nctx = ctx.shape[1]
    depth = w_in.shape[0]
    t_lat, t_ctx = bsz * seq, bsz * nctx
    t_all = t_lat + t_ctx
    rows_grid = seq // GRID_W
    alpha = (2 * depth) ** 0.25

    tables = _rope_tables(seq)
    band_bias = band_bias_table()
    dims = dict(bsz=bsz, seq=seq, nctx=nctx, t_lat=t_lat)

    n_groups = -(-(bsz + 1) // 8) * 8
    cc = jnp.zeros((n_groups, d), F32).at[:bsz].set(c).at[bsz].set(c_ctx)

    def group_of(tm):
        lat_blocks, per_batch = t_lat // tm, seq // tm
        assert t_lat % tm == 0 and seq % tm == 0
        return lambda i: jnp.where(i < lat_blocks, i // per_batch, bsz)

    xs = jnp.concatenate([x.reshape(t_lat, d), ctx.reshape(t_ctx, d)], axis=0)

    for l in range(depth):
        need_ctx = l < depth - 1
        rows_out = t_all if need_ctx else t_lat
        mod = ada_modulation(cc, w_ada, b_ada[l], l).reshape(n_groups * 6, 1, d)

        u = token_matmul(xs, _in_proj_weight(w_in[l]), prologue="modulate", mod=mod, mod_idx=(0, 1),
                         group_of=group_of, out_dtype=BF16, name="in_proj")
        w_gate_l = jnp.transpose(w_gate[l], (1, 0, 2)).reshape(d, N_BRANCH * d).astype(BF16)
        gates = token_matmul(xs, w_gate_l, rows=rows_out, prologue="modulate", mod=mod, mod_idx=(0, 1),
                             group_of=group_of, bias=b_gate[l].reshape(-1), act="sigmoid", out_dtype=BF16,
                             name="branch_gates")

        w_uq, w_ukv = _mla_up_weights(a_w_uq[l], a_w_ukv[l])
        qa = token_matmul(u, w_uq, x_cols=(A_Q_LORA, U_CQ * LANE // A_Q_LORA), prologue="rmsnorm", norm_g=a_q_norm[l], out_dtype=BF16,
                          name="mla_q_up")
        kva = token_matmul(u, w_ukv, x_cols=(A_KV_LORA, U_CKV), prologue="rmsnorm", norm_g=a_kv_norm[l],
                           out_dtype=BF16, name="mla_kv_up")
        ur = qk_prep(u, 0, U_PREP_N, U_PREP_R32, U_PREP_RMSQ, tables, d_q_norm[l], d_k_norm[l], t_lat, seq, "qk_prep")
        qr = qk_prep(qa, 1, A_HEADS, A_HEADS, 0, tables, d_q_norm[l], d_k_norm[l], t_lat, seq, "mla_q_rope")
        mixers = {
            "mla": dict(q_srcs=[(qa, 0), (qr, 0)], k_srcs=[(kva, 0, 1), (ur, R_KR, 0)], v_src=(kva, A_HEADS, 1),
                        heads=A_HEADS, scale=A_SCALE),
            "swa": dict(q_srcs=[(ur, R_BQ)], k_srcs=[(ur, R_BK, 0)], v_src=(u, U_BV, 0), heads=B_HEADS,
                        scale=B_SCALE, sink=b_sink[l], heads_per_step=B_HEADS),
            "na": dict(q_srcs=[(u, U_CQ_)], k_srcs=[(u, U_CK, 1)], v_src=(u, U_CV, 1), heads=C_HEADS,
                       scale=C_SCALE, heads_per_step=C_HEADS),
            "gqa": dict(q_srcs=[(ur, R_DQ)], k_srcs=[(ur, R_DK, D_HEADS // D_KV_HEADS)],
                        v_src=(u, U_DV, D_HEADS // D_KV_HEADS), heads=D_HEADS, scale=D_SCALE),
        }
        windows = {"swa": dict(window=(B_TQ, B_BACK, B_WK), bias=band_bias),
                   "na": dict(window=(C_TQ, C_BACK, C_WK), bias=neighbourhood_bias_table(c_rpb[l], rows_grid))}
        branches = [attention(mode="lat", name=mixer + "_attention", **cfg, **windows.get(mixer, {}), **dims)
                    for mixer, cfg in mixers.items()]
        branches_ctx = [attention(mode="ctx", name=mixer + "_attention_ctx", **cfg, **dims)
                        for mixer, cfg in mixers.items()] if need_ctx else None

        xs = merge_out_ln(branches, branches_ctx, _branch_weights(w_branch[l]), gates, w_out[l].astype(BF16), xs, mod,
                          2, group_of, ln1_g[l], ln1_b[l], alpha)

        pq = token_matmul(xs, peer_w_q[l].astype(BF16), prologue="modulate", mod=mod, mod_idx=(3, 4),
                          group_of=group_of, out_dtype=BF16, name="peer_query")
        keys = peer_sub_keys[l].reshape(2 * PEER_HEADS, PEER_KEYS, PEER_QDIM // 2).astype(BF16)
        cnt, e0, rank, e1 = peer_topk(pq, keys)
        ht = modulate_transposed(xs, mod, (3, 4), group_of)
        n_exp = peer_u.shape[1]
        vt = jnp.transpose(peer_v[l].astype(BF16).reshape(n_exp // PEER_EB, PEER_EB, d), (0, 2, 1))
        ffn_t = peer_dense(ht, peer_u[l].astype(BF16), vt, cnt, e0, rank, e1)
        xs = residual_ln(xs, ffn_t, mod, 5, group_of, ln2_g[l], ln2_b[l], alpha)

    return xs[:t_lat].reshape(bsz, seq, d)
```

```python
import functools
import math

import jax
import jax.numpy as jnp
import numpy as np
from jax import lax
from jax.experimental import pallas as pl
from jax.experimental.pallas import tpu as pltpu

F32 = jnp.float32
BF16 = jnp.bfloat16

GRID_W = 64
ROPE_THETA = 10000.0
LN_EPS = 1e-5
RMS_EPS = 1e-6
NEG_INF = -1e30

A_HEADS, A_NOPE, A_ROPE, A_V, A_Q_LORA, A_KV_LORA = 4, 128, 64, 128, 384, 128
B_HEADS, B_KV_HEADS, B_DIM, B_WINDOW = 8, 2, 64, 128
C_HEADS, C_DIM, NA_ROWS, NA_COLS = 4, 128, 8, 16
D_HEADS, D_KV_HEADS, D_DIM = 4, 2, 128
N_BRANCH, BRANCH_W = 4, 512
PEER_HEADS, PEER_KEYS, PEER_QDIM, PEER_TOPK = 8, 128, 256, 16

A_SCALE = (A_NOPE + A_ROPE) ** -0.5
B_SCALE = B_DIM ** -0.5
C_SCALE = C_DIM ** -0.5
D_SCALE = D_DIM ** -0.5

IN_SIZES = (A_Q_LORA, A_KV_LORA, A_ROPE,
            B_HEADS * B_DIM, B_KV_HEADS * B_DIM, B_KV_HEADS * B_DIM,
            C_HEADS * C_DIM, C_HEADS * C_DIM, C_HEADS * C_DIM,
            D_HEADS * D_DIM, D_KV_HEADS * D_DIM, D_KV_HEADS * D_DIM)
LANE = 128
BF16_ROWS = 16

VMEM_LIMIT = 56 * 1024 * 1024

PEER_CHUNK = 512
PEER_EB = 1024
PEER_TOPK_TILE = 256

B_TQ, B_BACK, B_WK = 512, 128, 768
C_TQ, C_BACK, C_WK = 4 * GRID_W, 4 * GRID_W, 12 * GRID_W
DENSE_TQ = 1024
DENSE_SUB = 4
ROPE_TM = 512

R_BQ, R_KR, R_BK, R_DQ, R_DK = 0, 8, 9, 10, 14
U_PREP_N, U_PREP_R32, U_PREP_RMSQ = 16, 10, 4
U_CQ_, U_CK, U_CV, U_CKV, U_BV, U_CQ, U_DV, U_BLOCKS = 16, 20, 24, 28, 29, 30, 33, 36


def _cparams(sem):
    return pltpu.CompilerParams(dimension_semantics=sem, vmem_limit_bytes=VMEM_LIMIT)


def _pick(n, cands):
    for c in cands:
        if n % c == 0:
            return c
    raise ValueError(f"no tile in {cands} divides {n}")


def _dot(a, b):
    return jnp.dot(a, b, preferred_element_type=F32)


def _dot_nt(a, b):
    return lax.dot_general(a, b, (((1,), (1,)), ((), ())), preferred_element_type=F32)


def _ada_kernel(c_ref, w_ref, b_ref, o_ref):
    c = c_ref[...]
    s = c * (1.0 / (1.0 + jnp.exp(-c)))
    w = w_ref[0]
    s_hi = s.astype(BF16)
    s_lo = (s - s_hi.astype(F32)).astype(BF16)
    w_hi = w.astype(BF16)
    w_lo = (w - w_hi.astype(F32)).astype(BF16)
    acc = _dot(s_hi, w_hi) + _dot(s_lo, w_hi) + _dot(s_hi, w_lo)
    o_ref[...] = acc + b_ref[...]


def ada_modulation(cc, w_all, b, layer):
    r, k = cc.shape
    n = w_all.shape[2]
    tn = _pick(n, (512, 256, 128))
    return pl.pallas_call(
        _ada_kernel,
        grid=(n // tn,),
        in_specs=[pl.BlockSpec((r, k), lambda j: (0, 0)),
                  pl.BlockSpec((1, k, tn), lambda j: (layer, 0, j)),
                  pl.BlockSpec((1, tn), lambda j: (0, j))],
        out_specs=pl.BlockSpec((r, tn), lambda j: (0, j)),
        out_shape=jax.ShapeDtypeStruct((r, n), F32),
        compiler_params=_cparams(("arbitrary",)),
        name="ada_modulation",
    )(cc, w_all, b.reshape(1, n))


def _mm_kernel(*refs, prologue, has_bias, act):
    it = iter(refs)
    x_ref = next(it)
    if prologue == "modulate":
        sh_ref, sc_ref = next(it), next(it)
    elif prologue == "rmsnorm":
        g_ref = next(it)
    w_ref = next(it)
    b_ref = next(it) if has_bias else None
    o_ref = next(it)
    a_scr = next(it)

    @pl.when(pl.program_id(1) == 0)
    def _():
        x = x_ref[...].astype(F32)
        if prologue == "modulate":
            a = x * (1.0 + sc_ref[0]) + sh_ref[0]
        elif prologue == "rmsnorm":
            ms = jnp.mean(x * x, axis=-1, keepdims=True)
            a = (x * lax.rsqrt(ms + RMS_EPS)) * g_ref[...]
        else:
            a = x
        a_scr[...] = a.astype(BF16)

    acc = _dot(a_scr[...], w_ref[...])
    if has_bias:
        acc = acc + b_ref[...]
    if act == "sigmoid":
        acc = 1.0 / (1.0 + jnp.exp(-acc))
    o_ref[...] = acc.astype(o_ref.dtype)


def token_matmul(x, w, *, rows=None, x_cols=None, prologue=None, mod=None, mod_idx=None, group_of=None,
                 norm_g=None, bias=None, act=None, out_dtype=F32, name="token_matmul"):
    rows = x.shape[0] if rows is None else rows
    kdim, n = w.shape
    kw, kblk = (x.shape[1], 0) if x_cols is None else x_cols
    assert kw == kdim
    tm = _pick(rows, (1024, 512, 256))
    tn = _pick(n, (2048, 1536, 1024, 512, 256, 128))
    in_specs = [pl.BlockSpec((tm, kdim), lambda i, j: (i, kblk))]
    args = [x]
    if prologue == "modulate":
        tm_group = group_of(tm)
        for slot in mod_idx:
            in_specs.append(pl.BlockSpec((1, 1, kdim), lambda i, j, s=slot: (tm_group(i) * 6 + s, 0, 0)))
            args.append(mod)
    elif prologue == "rmsnorm":
        in_specs.append(pl.BlockSpec((1, kdim), lambda i, j: (0, 0)))
        args.append(norm_g.reshape(1, kdim))
    in_specs.append(pl.BlockSpec((kdim, tn), lambda i, j: (0, j)))
    args.append(w)
    if bias is not None:
        in_specs.append(pl.BlockSpec((1, tn), lambda i, j: (0, j)))
        args.append(bias.reshape(1, n))
    return pl.pallas_call(
        functools.partial(_mm_kernel, prologue=prologue, has_bias=bias is not None, act=act),
        grid=(rows // tm, n // tn),
        in_specs=in_specs,
        out_specs=pl.BlockSpec((tm, tn), lambda i, j: (i, j)),
        out_shape=jax.ShapeDtypeStruct((rows, n), out_dtype),
        scratch_shapes=[pltpu.VMEM((tm, kdim), BF16)],
        compiler_params=_cparams(("parallel", "arbitrary")),
        name=name,
    )(*args)


def _res_ln_kernel(x_ref, yt_ref, gate_ref, g_ref, b_ref, o_ref, *, alpha):
    z = alpha * x_ref[...] + gate_ref[0] * yt_ref[0].T
    mu = jnp.mean(z, axis=-1, keepdims=True)
    zc = z - mu
    var = jnp.mean(zc * zc, axis=-1, keepdims=True)
    o_ref[...] = (zc * lax.rsqrt(var + LN_EPS)) * g_ref[...] + b_ref[...]


def residual_ln(x, yt, mod, gate_slot, group_of, ln_g, ln_b, alpha):
    nch, d, tm = yt.shape
    rows = nch * tm
    tm_group = group_of(tm)
    return pl.pallas_call(
        functools.partial(_res_ln_kernel, alpha=alpha),
        grid=(nch,),
        in_specs=[pl.BlockSpec((tm, d), lambda i: (i, 0)),
                  pl.BlockSpec((1, d, tm), lambda i: (i, 0, 0)),
                  pl.BlockSpec((1, 1, d), lambda i: (tm_group(i) * 6 + gate_slot, 0, 0)),
                  pl.BlockSpec((1, d), lambda i: (0, 0)),
                  pl.BlockSpec((1, d), lambda i: (0, 0))],
        out_specs=pl.BlockSpec((tm, d), lambda i: (i, 0)),
        out_shape=jax.ShapeDtypeStruct((rows, d), F32),
        compiler_params=_cparams(("parallel",)),
        name="residual_ln",
    )(x, yt, mod, ln_g.reshape(1, d), ln_b.reshape(1, d))


def _qk_prep_kernel(x_ref, c32_ref, sa_ref, sb_ref, c64_ref, s64_ref, gq_ref, gk_ref, o_ref, *, n_r32, n_rms_q):
    for j in range(x_ref.shape[1] // LANE):
        cols = slice(j * LANE, (j + 1) * LANE)
        x = x_ref[:, cols].astype(F32)
        if j < n_r32:
            y = x * c32_ref[...] + pltpu.roll(x, 32, 1) * sa_ref[...] + pltpu.roll(x, 96, 1) * sb_ref[...]
        else:
            g = gq_ref[...] if j < n_r32 + n_rms_q else gk_ref[...]
            ms = jnp.mean(x * x, axis=-1, keepdims=True)
            y = (x * lax.rsqrt(ms + RMS_EPS)) * g
            y = y * c64_ref[...] + pltpu.roll(y, 64, 1) * s64_ref[...]
        o_ref[:, cols] = y.astype(o_ref.dtype)


def qk_prep(x, group_idx, n_blocks, n_r32, n_rms_q, tables, gq, gk, lat_rows, seq, name):
    rows = x.shape[0]
    tm = ROPE_TM
    width = n_blocks * LANE
    lat_blocks, per_seq = lat_rows // tm, seq // tm
    tab_spec = pl.BlockSpec((tm, LANE), lambda i: (jnp.where(i < lat_blocks, i % per_seq, per_seq), 0))
    vec_spec = pl.BlockSpec((1, LANE), lambda i: (0, 0))
    return pl.pallas_call(
        functools.partial(_qk_prep_kernel, n_r32=n_r32, n_rms_q=n_rms_q),
        grid=(rows // tm,),
        in_specs=[pl.BlockSpec((tm, width), lambda i: (i, group_idx))] + [tab_spec] * 5 + [vec_spec] * 2,
        out_specs=pl.BlockSpec((tm, width), lambda i: (i, 0)),
        out_shape=jax.ShapeDtypeStruct((rows, width), BF16),
        compiler_params=_cparams(("parallel",)),
        name=name,
    )(x, *tables, gq.reshape(1, LANE), gk.reshape(1, LANE))


def _attn_kernel(*refs, n_parts, has_lat, window, has_bias, bias_per_head, has_sink, scale, n_lat, n_sub, hps,
                 per_head):
    it = iter(refs)
    q_refs = [next(it) for _ in range(n_parts)]
    if has_lat:
        kl_refs = [next(it) for _ in range(n_parts)]
        vl_ref = next(it)
    kc_refs = [next(it) for _ in range(n_parts)]
    vc_ref = next(it)
    bias_ref = next(it) if has_bias else None
    sink_ref = next(it) if has_sink else None
    refs = list(it)
    o_ref = refs[-1]

    if has_lat and window is not None:
        tq, back, wk = window
        start = pl.multiple_of(jnp.clip(pl.program_id(2) * tq - back, 0, n_lat - wk), LANE)
        key_rows = pl.ds(start, wk)
    else:
        key_rows = slice(None)
    sub = o_ref.shape[0] // n_sub
    for hh in range(hps):
        lanes = slice(hh * LANE, (hh + 1) * LANE)

        def cols(j):
            return lanes if per_head[j] else slice(None)

        kcs = [jnp.concatenate([k[:, cols(j)] for j, k in enumerate(kc_refs)], axis=1)]
        vc = vc_ref[:, cols(n_parts)]
        if has_lat:
            kls = [jnp.concatenate([k[key_rows, cols(j)] for j, k in enumerate(kl_refs)], axis=1)]
            vl = vl_ref[key_rows, cols(n_parts)]
        if has_sink:
            sink = sink_ref[pl.program_id(1) * hps + hh]
        for sb in range(n_sub):
            rows = slice(sb * sub, (sb + 1) * sub)
            qs = [jnp.concatenate([q[rows, lanes] for q in q_refs], axis=1)]
            s2 = functools.reduce(jnp.add, [_dot_nt(q, k) for q, k in zip(qs, kcs)]) * scale
            m = jnp.max(s2, axis=-1, keepdims=True)
            if has_lat:
                s1 = functools.reduce(jnp.add, [_dot_nt(q, k) for q, k in zip(qs, kls)]) * scale
                if has_bias:
                    s1 = s1 + bias_ref[0, hh if bias_per_head else 0, rows, :]
                m = jnp.maximum(m, jnp.max(s1, axis=-1, keepdims=True))
            if has_sink:
                m = jnp.maximum(m, sink)
            p2 = jnp.exp(s2 - m)
            l = jnp.sum(p2, axis=-1, keepdims=True)
            o = _dot(p2.astype(BF16), vc)
            if has_lat:
                p1 = jnp.exp(s1 - m)
                l = l + jnp.sum(p1, axis=-1, keepdims=True)
                o = o + _dot(p1.astype(BF16), vl)
            if has_sink:
                l = l + jnp.exp(sink - m)
            o_ref[rows, lanes] = (o / l).astype(o_ref.dtype)


def attention(q_srcs, k_srcs, v_src, *, heads, scale, bsz, seq, nctx, t_lat, mode, window=None, bias=None, sink=None,
              heads_per_step=1, name="attention"):
    lat = mode == "lat"
    hps = heads_per_step
    n_parts = len(q_srcs)
    tq = (window[0] if window is not None else DENSE_TQ) if lat else nctx
    nqb = (seq if lat else nctx) // tq
    ctx0 = t_lat // nctx
    kv_srcs = list(k_srcs) + [v_src]
    per_head = tuple(grp == 1 for _, _, grp in kv_srcs)
    assert heads % hps == 0 and all(first % hps == 0 for _, first in q_srcs)
    assert hps == 1 or all(grp == 0 or (grp == 1 and first % hps == 0) for _, first, grp in kv_srcs)

    def kv_spec(rows, row_blk, first, grp):
        if hps > 1 and grp == 1:
            return pl.BlockSpec((rows, hps * LANE), lambda bi, hg, i: (row_blk(bi), first // hps + hg))
        return pl.BlockSpec((rows, LANE), lambda bi, hg, i: (row_blk(bi), first + (hg // grp if grp else 0)))

    def q_row(bi, i):
        return bi * nqb + i if lat else t_lat // tq + bi

    in_specs, args = [], []
    for arr, first in q_srcs:
        in_specs.append(pl.BlockSpec((tq, hps * LANE), lambda bi, hg, i, f=first: (q_row(bi, i), f // hps + hg)))
        args.append(arr)
    if lat:
        for arr, first, grp in kv_srcs:
            in_specs.append(kv_spec(seq, lambda bi: bi, first, grp))
            args.append(arr)
    for arr, first, grp in kv_srcs:
        in_specs.append(kv_spec(nctx, lambda bi: ctx0 + bi, first, grp))
        args.append(arr)
    bias_per_head = bias is not None and bias.shape[1] != 1
    if bias is not None:
        in_specs.append(pl.BlockSpec(
            (1, hps if bias_per_head else 1, tq, window[2]),
            lambda bi, hg, i: (jnp.where(i == 0, 0, jnp.where(i == nqb - 1, 2, 1)), hg if bias_per_head else 0, 0, 0)))
        args.append(bias)
    if sink is not None:
        in_specs.append(pl.BlockSpec(memory_space=pltpu.SMEM))
        args.append(sink.astype(F32))
    if window is not None:
        assert seq % tq == 0 and nqb >= 2 and seq >= window[2]
    return pl.pallas_call(
        functools.partial(_attn_kernel, n_parts=n_parts, has_lat=lat, window=window, has_bias=bias is not None,
                          bias_per_head=bias_per_head, has_sink=sink is not None, scale=scale, n_lat=seq,
                          n_sub=DENSE_SUB if (lat and window is None) else 1, hps=hps, per_head=per_head),
        grid=(bsz, heads // hps, nqb),
        in_specs=in_specs,
        out_specs=pl.BlockSpec((tq, hps * LANE), lambda bi, hg, i: (bi * nqb + i, hg)),
        out_shape=jax.ShapeDtypeStruct((bsz * nqb * tq, heads * LANE), BF16),
        compiler_params=_cparams(("parallel", "parallel", "arbitrary")),
        name=name,
    )(*args)


def _window_q_offsets(geom):
    tq, back, wk = geom
    return (0, back, wk - tq)


def band_bias_table():
    tq, _, wk = (B_TQ, B_BACK, B_WK)
    kpos = np.arange(wk)[None, :]
    tabs = []
    for off in _window_q_offsets((B_TQ, B_BACK, B_WK)):
        qpos = off + np.arange(tq)[:, None]
        tabs.append(np.where(np.abs(qpos - kpos) <= B_WINDOW, 0.0, NEG_INF))
    return jnp.asarray(np.stack(tabs)[:, None], F32)


def neighbourhood_bias_table(rpb, rows):
    tq, back, wk = (C_TQ, C_BACK, C_WK)
    assert rows >= wk // GRID_W and rows >= NA_ROWS
    nqb = rows * GRID_W // tq
    nq_rows, nk_rows = tq // GRID_W, wk // GRID_W
    cols = np.arange(GRID_W)
    col_start = np.clip(cols - NA_COLS // 2, 0, GRID_W - NA_COLS)
    col_ok = (cols[None, :] >= col_start[:, None]) & (cols[None, :] < col_start[:, None] + NA_COLS)
    col_off = np.clip(cols[None, :] - cols[:, None] + (NA_COLS - 1), 0, 2 * NA_COLS - 2)
    onehot_c = jnp.asarray(col_off[..., None] == np.arange(2 * NA_COLS - 1), F32)
    tabs = []
    for cls, off in enumerate(_window_q_offsets((C_TQ, C_BACK, C_WK))):
        blk = (0, 1, nqb - 1)[cls]
        w0 = int(np.clip(blk * tq - back, 0, rows * GRID_W - wk)) // GRID_W
        assert w0 * GRID_W + off == blk * tq and off % GRID_W == 0
        q_row = w0 + off // GRID_W + np.arange(nq_rows)
        k_row = w0 + np.arange(nk_rows)
        row_start = np.clip(q_row - NA_ROWS // 2, 0, rows - NA_ROWS)
        row_ok = (k_row[None, :] >= row_start[:, None]) & (k_row[None, :] < row_start[:, None] + NA_ROWS)
        row_off = np.clip(k_row[None, :] - q_row[:, None] + (NA_ROWS - 1), 0, 2 * NA_ROWS - 2)
        by_row = rpb[:, row_off, :].astype(F32)
        vals = jnp.einsum("hrsc,qkc->hrqsk", by_row, onehot_c, precision=lax.Precision.HIGHEST)
        ok = row_ok[:, None, :, None] & col_ok[None, :, None, :]
        tabs.append(jnp.where(jnp.asarray(ok)[None], vals, NEG_INF).reshape(rpb.shape[0], tq, wk))
    return jnp.stack(tabs)


def _merge_out_ln_kernel(*refs, alpha, n_branch, lat_blocks):
    o_refs = refs[:n_branch]
    refs = refs[n_branch:]
    if lat_blocks is not None:
        oc_refs = refs[:n_branch]
        refs = refs[n_branch:]
        is_ctx = pl.program_id(0) >= lat_blocks
    wbr_refs = refs[:n_branch]
    g_ref, wout_ref, x_ref, gate_ref, lng_ref, lnb_ref, out_ref = refs[n_branch:]
    d = x_ref.shape[1]
    y = None
    for r, (o_ref, wbr_ref) in enumerate(zip(o_refs, wbr_refs)):
        o = o_ref[...]
        if lat_blocks is not None:
            o = jnp.where(is_ctx, oc_refs[r][...], o)
        t = g_ref[:, r * d:(r + 1) * d].astype(F32) * _dot(o, wbr_ref[...])
        y = t if y is None else y + t
    mix = _dot(y.astype(BF16), wout_ref[...])
    z = alpha * x_ref[...] + gate_ref[0] * mix
    mu = jnp.mean(z, axis=-1, keepdims=True)
    zc = z - mu
    var = jnp.mean(zc * zc, axis=-1, keepdims=True)
    out_ref[...] = (zc * lax.rsqrt(var + LN_EPS)) * lng_ref[...] + lnb_ref[...]


def merge_out_ln(branches, branches_ctx, w_branch, gates, w_out, x, mod, gate_slot, group_of, ln_g, ln_b, alpha):
    rows = gates.shape[0]
    nbr = len(branches)
    d = w_out.shape[0]
    tm = _pick(rows, (256,))
    tm_group = group_of(tm)
    resident = pl.Buffered(1)
    lat_blocks = branches[0].shape[0] // tm
    last = lat_blocks - 1
    branch_specs = [pl.BlockSpec((tm, o.shape[1]), lambda i: (jnp.minimum(i, last), 0)) for o in branches]
    if branches_ctx is not None:
        assert rows == branches[0].shape[0] + branches_ctx[0].shape[0]
        branch_specs += [pl.BlockSpec((tm, o.shape[1]), lambda i: (jnp.maximum(i - lat_blocks, 0), 0))
                         for o in branches_ctx]
    else:
        assert rows == branches[0].shape[0]
    return pl.pallas_call(
        functools.partial(_merge_out_ln_kernel, alpha=alpha, n_branch=nbr,
                          lat_blocks=None if branches_ctx is None else lat_blocks),
        grid=(rows // tm,),
        in_specs=branch_specs + [
                  pl.BlockSpec(w.shape, lambda i: (0, 0), pipeline_mode=resident) for w in w_branch] + [
                  pl.BlockSpec((tm, nbr * d), lambda i: (i, 0)),
                  pl.BlockSpec((d, d), lambda i: (0, 0), pipeline_mode=resident),
                  pl.BlockSpec((tm, d), lambda i: (i, 0)),
                  pl.BlockSpec((1, 1, d), lambda i: (tm_group(i) * 6 + gate_slot, 0, 0)),
                  pl.BlockSpec((1, d), lambda i: (0, 0)),
                  pl.BlockSpec((1, d), lambda i: (0, 0))],
        out_specs=pl.BlockSpec((tm, d), lambda i: (i, 0)),
        out_shape=jax.ShapeDtypeStruct((rows, d), F32),
        compiler_params=_cparams(("parallel",)),
        name="merge_out_ln",
    )(*branches, *(branches_ctx or []), *w_branch, gates, w_out, x, mod, ln_g.reshape(1, d), ln_b.reshape(1, d))


def _peer_pairs():
    return [(a, b) for a in range(PEER_TOPK) for b in range(PEER_TOPK) if (a + 1) * (b + 1) <= PEER_TOPK]


def _peer_topk_kernel(q_ref, keys_ref, cnt_ref, e0_ref, rank_ref, e1_ref, s0_scr):
    tops = [[[None] * PEER_HEADS for _ in range(PEER_TOPK)] for _ in range(2)]
    for h in range(PEER_HEADS):
        for p in range(2):
            hp = 2 * h + p
            qs = q_ref[:, hp * PEER_KEYS:(hp + 1) * PEER_KEYS]
            s = _dot_nt(keys_ref[hp], qs)
            work = s
            rank = jnp.full(s.shape, float(PEER_TOPK), F32)
            for r in range(PEER_TOPK):
                m = jnp.max(work, axis=0, keepdims=True)
                tops[p][r][h] = m
                if p == 1:
                    hit = work == m
                    rank = jnp.where(hit, float(r), rank)
                    work = jnp.where(hit, -jnp.inf, work)
                elif r + 1 < PEER_TOPK:
                    work = jnp.where(work == m, -jnp.inf, work)
            if p == 0:
                s0_scr[h] = s
            else:
                rank_ref[h, 0] = rank.astype(BF16)
                e1_ref[h, 0] = jnp.exp(s - tops[1][0][h]).astype(BF16)
    top = [[jnp.concatenate(tops[p][r], axis=0) for r in range(PEER_TOPK)] for p in range(2)]
    cands = [top[0][a] + top[1][b] for a, b in _peer_pairs()]
    work = list(cands)
    for r in range(PEER_TOPK):
        m = functools.reduce(jnp.maximum, work)
        if r + 1 < PEER_TOPK:
            work = [jnp.where(w == m, -jnp.inf, w) for w in work]
    tau = m
    cmax = cands[0]
    z = functools.reduce(jnp.add, [jnp.where(c >= tau, jnp.exp(c - cmax), 0.0) for c in cands])
    rz = 1.0 / z
    b_dense = 4
    a_sparse = PEER_TOPK // (b_dense + 1)
    cnt_hi = [functools.reduce(jnp.add, [jnp.where(top[0][a] + top[1][b] >= tau, 1.0, 0.0)
                                          for b in range(b_dense, PEER_TOPK)]) for a in range(a_sparse)]
    for h in range(PEER_HEADS):
        s0 = s0_scr[h]
        tau_h = tau[h:h + 1]
        cnt = None
        for b in range(b_dense):
            inc = jnp.where(s0 + top[1][b][h:h + 1] >= tau_h, 1.0, 0.0)
            cnt = inc if cnt is None else cnt + inc
        for a in range(a_sparse):
            cnt = cnt + jnp.where(s0 == top[0][a][h:h + 1], cnt_hi[a][h:h + 1], 0.0)
        cnt_ref[h, 0] = cnt
        e0_ref[h, 0] = jnp.exp(s0 - top[0][0][h:h + 1]) * rz[h:h + 1]


def peer_topk(q, keys):
    rows = q.shape[0]
    tt = PEER_TOPK_TILE
    per_chunk = PEER_CHUNK // tt
    nch = rows // PEER_CHUNK
    big = pl.BlockSpec((PEER_HEADS, 1, PEER_KEYS, tt), lambda i: (0, i // per_chunk, 0, i % per_chunk))
    shape_f32 = jax.ShapeDtypeStruct((PEER_HEADS, nch, PEER_KEYS, PEER_CHUNK), F32)
    shape_bf16 = jax.ShapeDtypeStruct((PEER_HEADS, nch, PEER_KEYS, PEER_CHUNK), BF16)
    return pl.pallas_call(
        _peer_topk_kernel,
        grid=(rows // tt,),
        in_specs=[pl.BlockSpec((tt, q.shape[1]), lambda i: (i, 0)),
                  pl.BlockSpec(keys.shape, lambda i: (0, 0, 0))],
        out_specs=[big, big, big, big],
        out_shape=[shape_f32, shape_f32, shape_bf16, shape_bf16],
        scratch_shapes=[pltpu.VMEM((PEER_HEADS, PEER_KEYS, tt), F32)],
        compiler_params=_cparams(("parallel",)),
        name="peer_topk",
    )(q, keys)


def _modulate_t_kernel(x_ref, sh_ref, sc_ref, o_ref):
    h = x_ref[...] * (1.0 + sc_ref[0]) + sh_ref[0]
    o_ref[0] = h.T.astype(BF16)


def modulate_transposed(x, mod, mod_idx, group_of):
    rows, d = x.shape
    tt = PEER_CHUNK
    tt_group = group_of(tt)
    mod_specs = [pl.BlockSpec((1, 1, d), lambda i, s=slot: (tt_group(i) * 6 + s, 0, 0)) for slot in mod_idx]
    return pl.pallas_call(
        _modulate_t_kernel,
        grid=(rows // tt,),
        in_specs=[pl.BlockSpec((tt, d), lambda i: (i, 0))] + mod_specs,
        out_specs=pl.BlockSpec((1, d, tt), lambda i: (i, 0, 0)),
        out_shape=jax.ShapeDtypeStruct((rows // tt, d, tt), BF16),
        compiler_params=_cparams(("parallel",)),
        name="modulate_transposed",
    )(x, mod, mod)


def _peer_dense_kernel(ht_ref, u_ref, vt_ref, cnt_ref, e0_ref, rank_ref, e1_ref, o_ref, a_scr, w_scr, p_scr, *, n_e,
                       n_pairs):
    s = pl.program_id(0)
    e = s % n_e
    cur = s % 2
    prev = 1 - cur

    @pl.when(jnp.logical_and(s > 0, (s - 1) % n_e == 0))
    def _():
        o_ref[...] = jnp.zeros_like(o_ref)
    i_per_eb = PEER_EB // PEER_KEYS
    tc = w_scr.shape[-1]

    def gate_and_down():
        for ii in range(i_per_eb):
            sl = slice(ii * PEER_KEYS, (ii + 1) * PEER_KEYS)
            a = a_scr[prev, sl, :]
            act = 0.5 * a * (1.0 + lax.erf(a * (1.0 / math.sqrt(2.0))))
            p_scr[sl, :] = act.astype(BF16) * w_scr[prev, sl, :]
        o_ref[0] += _dot(vt_ref[0], p_scr[...])

    def up_and_weights():
        a_scr[cur] = _dot(u_ref[...], ht_ref[0])
        for ii in range(i_per_eb):
            i = e * i_per_eb + ii
            cnt = [jnp.broadcast_to(cnt_ref[h, 0, pl.ds(i, 1), :], (BF16_ROWS, tc)).astype(BF16)
                   for h in range(PEER_HEADS)]
            e0 = [jnp.broadcast_to(e0_ref[h, 0, pl.ds(i, 1), :], (BF16_ROWS, tc)).astype(BF16)
                  for h in range(PEER_HEADS)]
            for g in range(PEER_KEYS // BF16_ROWS):
                rows = slice(g * BF16_ROWS, (g + 1) * BF16_ROWS)
                w = None
                for h in range(PEER_HEADS):
                    e1 = e1_ref[h, 0, rows, :]
                    t = jnp.where(rank_ref[h, 0, rows, :] < cnt[h], e1, jnp.zeros_like(e1)) * e0[h]
                    w = t if w is None else w + t
                w_scr[cur, ii * PEER_KEYS + g * BF16_ROWS:ii * PEER_KEYS + (g + 1) * BF16_ROWS, :] = w

    @pl.when(s == 0)
    def _():
        up_and_weights()

    @pl.when(jnp.logical_and(s > 0, s < n_pairs))
    def _():
        gate_and_down()
        up_and_weights()

    @pl.when(s == n_pairs)
    def _():
        gate_and_down()


def peer_dense(ht, u, vt, cnt, e0, rank, e1):
    nch, d, tc = ht.shape
    n_e = u.shape[0] // PEER_EB
    n_pairs = nch * n_e

    def cur_chunk(s):
        return jnp.minimum(s, n_pairs - 1) // n_e

    def prev_chunk(s):
        return jnp.maximum(s - 1, 0) // n_e

    sel = pl.BlockSpec((PEER_HEADS, 1, PEER_KEYS, tc), lambda s: (0, cur_chunk(s), 0, 0))
    return pl.pallas_call(
        functools.partial(_peer_dense_kernel, n_e=n_e, n_pairs=n_pairs),
        grid=(n_pairs + 1,),
        in_specs=[pl.BlockSpec((1, d, tc), lambda s: (cur_chunk(s), 0, 0)),
                  pl.BlockSpec((PEER_EB, d), lambda s: (jnp.minimum(s, n_pairs - 1) % n_e, 0)),
                  pl.BlockSpec((1, d, PEER_EB), lambda s: (jnp.maximum(s - 1, 0) % n_e, 0, 0)),
                  sel, sel, sel, sel],
        out_specs=pl.BlockSpec((1, d, tc), lambda s: (prev_chunk(s), 0, 0)),
        out_shape=jax.ShapeDtypeStruct((nch, d, tc), F32),
        scratch_shapes=[pltpu.VMEM((2, PEER_EB, tc), F32), pltpu.VMEM((2, PEER_EB, tc), BF16),
                        pltpu.VMEM((PEER_EB, tc), BF16)],
        compiler_params=_cparams(("arbitrary",)),
        name="peer_dense",
    )(ht, u, vt, cnt, e0, rank, e1)


def _axial_angles(n, rot_dim):
    t = jnp.arange(n)
    rows = (t // GRID_W).astype(F32)
    cols = (t % GRID_W).astype(F32)
    nf = rot_dim // 4
    inv = ROPE_THETA ** (-jnp.arange(nf, dtype=F32) / nf)
    ang = jnp.concatenate([rows[:, None] * inv, cols[:, None] * inv], axis=-1)
    return jnp.cos(ang), jnp.sin(ang)


def _rope_tables(seq):
    assert A_ROPE == B_DIM == LANE // 2 and D_DIM == LANE

    def pad(t, fill):
        return jnp.concatenate([t, jnp.full((ROPE_TM, LANE), fill, F32)], axis=0)

    cos, sin = _axial_angles(seq, B_DIM)
    z = jnp.zeros_like(sin)
    c32 = jnp.concatenate([cos, cos, cos, cos], axis=1)
    sa = jnp.concatenate([z, sin, z, sin], axis=1)
    sb = jnp.concatenate([-sin, z, -sin, z], axis=1)
    cos, sin = _axial_angles(seq, D_DIM)
    c64 = jnp.concatenate([cos, cos], axis=1)
    s64 = jnp.concatenate([-sin, sin], axis=1)
    return [pad(c32, 1.0), pad(sa, 0.0), pad(sb, 0.0), pad(c64, 1.0), pad(s64, 0.0)]


def _half_block(t, first_half, axis):
    z = jnp.zeros_like(t)
    return jnp.concatenate([t, z] if first_half else [z, t], axis=axis)


def _in_proj_weight(w_in):
    d = w_in.shape[0]
    o = np.cumsum((0,) + IN_SIZES)
    part = [w_in[:, o[i]:o[i + 1]] for i in range(len(IN_SIZES))]
    per_kv = B_HEADS // B_KV_HEADS
    bq = [_half_block(part[3][:, h * B_DIM:(h + 1) * B_DIM], h // per_kv == 0, 1) for h in range(B_HEADS)]
    cols = bq + [_half_block(part[2], True, 1), part[4], part[9], part[10], part[6], part[7], part[8], part[1],
                 part[5], part[0], part[11], jnp.zeros((d, LANE), w_in.dtype)]
    out = jnp.concatenate(cols, axis=1).astype(BF16)
    assert out.shape[1] == U_BLOCKS * LANE
    return out


def _mla_up_weights(w_uq, w_ukv):
    dq, dkv = A_NOPE + A_ROPE, A_NOPE + A_V
    q_nope = [w_uq[:, h * dq:h * dq + A_NOPE] for h in range(A_HEADS)]
    q_rope = [_half_block(w_uq[:, h * dq + A_NOPE:(h + 1) * dq], True, 1) for h in range(A_HEADS)]
    k_nope = [w_ukv[:, h * dkv:h * dkv + A_NOPE] for h in range(A_HEADS)]
    v = [w_ukv[:, h * dkv + A_NOPE:(h + 1) * dkv] for h in range(A_HEADS)]
    return (jnp.concatenate(q_nope + q_rope, axis=1).astype(BF16), jnp.concatenate(k_nope + v, axis=1).astype(BF16))


def _branch_weights(w_branch):
    per_kv = B_HEADS // B_KV_HEADS
    wb = jnp.concatenate([_half_block(w_branch[1][h * B_DIM:(h + 1) * B_DIM], h // per_kv == 0, 0)
                          for h in range(B_HEADS)], axis=0)
    return [w_branch[0].astype(BF16), wb.astype(BF16), w_branch[2].astype(BF16), w_branch[3].astype(BF16)]


def kernel(x, c, ctx, c_ctx, w_ada, b_ada, w_in, a_q_norm, a_kv_norm, a_w_uq, a_w_ukv, b_sink, c_rpb, d_q_norm, d_k_norm, w_branch, w_gate, b_gate, w_out, ln1_g, ln1_b, ln2_g, ln2_b, peer_w_q, peer_sub_keys, peer_u, peer_v):
    bsz, seq, d = x.shape
    nctx = ctx.shape[1]
    depth = w_in.shape[0]
    t_lat, t_ctx = bsz * seq, bsz * nctx
    t_all = t_lat + t_ctx
    rows_grid = seq // GRID_W
    alpha = (2 * depth) ** 0.25

    tables = _rope_tables(seq)
    band_bias = band_bias_table()
    dims = dict(bsz=bsz, seq=seq, nctx=nctx, t_lat=t_lat)

    n_groups = -(-(bsz + 1) // 8) * 8
    cc = jnp.zeros((n_groups, d), F32).at[:bsz].set(c).at[bsz].set(c_ctx)

    def group_of(tm):
        lat_blocks, per_batch = t_lat // tm, seq // tm
        assert t_lat % tm == 0 and seq % tm == 0
        return lambda i: jnp.where(i < lat_blocks, i // per_batch, bsz)

    xs = jnp.concatenate([x.reshape(t_lat, d), ctx.reshape(t_ctx, d)], axis=0)

    for l in range(depth):
        need_ctx = l < depth - 1
        rows_out = t_all if need_ctx else t_lat
        mod = ada_modulation(cc, w_ada, b_ada[l], l).reshape(n_groups * 6, 1, d)

        u = token_matmul(xs, _in_proj_weight(w_in[l]), prologue="modulate", mod=mod, mod_idx=(0, 1),
                         group_of=group_of, out_dtype=BF16, name="in_proj")
        w_gate_l = jnp.transpose(w_gate[l], (1, 0, 2)).reshape(d, N_BRANCH * d).astype(BF16)
        gates = token_matmul(xs, w_gate_l, rows=rows_out, prologue="modulate", mod=mod, mod_idx=(0, 1),
                             group_of=group_of, bias=b_gate[l].reshape(-1), act="sigmoid", out_dtype=BF16,
                             name="branch_gates")

        w_uq, w_ukv = _mla_up_weights(a_w_uq[l], a_w_ukv[l])
        qa = token_matmul(u, w_uq, x_cols=(A_Q_LORA, U_CQ * LANE // A_Q_LORA), prologue="rmsnorm", norm_g=a_q_norm[l], out_dtype=BF16,
                          name="mla_q_up")
        kva = token_matmul(u, w_ukv, x_cols=(A_KV_LORA, U_CKV), prologue="rmsnorm", norm_g=a_kv_norm[l],
                           out_dtype=BF16, name="mla_kv_up")
        ur = qk_prep(u, 0, U_PREP_N, U_PREP_R32, U_PREP_RMSQ, tables, d_q_norm[l], d_k_norm[l], t_lat, seq, "qk_prep")
        qr = qk_prep(qa, 1, A_HEADS, A_HEADS, 0, tables, d_q_norm[l], d_k_norm[l], t_lat, seq, "mla_q_rope")
        mixers = {
            "mla": dict(q_srcs=[(qa, 0), (qr, 0)], k_srcs=[(kva, 0, 1), (ur, R_KR, 0)], v_src=(kva, A_HEADS, 1),
                        heads=A_HEADS, scale=A_SCALE),
            "swa": dict(q_srcs=[(ur, R_BQ)], k_srcs=[(ur, R_BK, 0)], v_src=(u, U_BV, 0), heads=B_HEADS,
                        scale=B_SCALE, sink=b_sink[l], heads_per_step=B_HEADS),
            "na": dict(q_srcs=[(u, U_CQ_)], k_srcs=[(u, U_CK, 1)], v_src=(u, U_CV, 1), heads=C_HEADS,
                       scale=C_SCALE, heads_per_step=C_HEADS),
            "gqa": dict(q_srcs=[(ur, R_DQ)], k_srcs=[(ur, R_DK, D_HEADS // D_KV_HEADS)],
                        v_src=(u, U_DV, D_HEADS // D_KV_HEADS), heads=D_HEADS, scale=D_SCALE),
        }
        windows = {"swa": dict(window=(B_TQ, B_BACK, B_WK), bias=band_bias),
                   "na": dict(window=(C_TQ, C_BACK, C_WK), bias=neighbourhood_bias_table(c_rpb[l], rows_grid))}
        branches = [attention(mode="lat", name=mixer + "_attention", **cfg, **windows.get(mixer, {}), **dims)
                    for mixer, cfg in mixers.items()]
        branches_ctx = [attention(mode="ctx", name=mixer + "_attention_ctx", **cfg, **dims)
                        for mixer, cfg in mixers.items()] if need_ctx else None

        xs = merge_out_ln(branches, branches_ctx, _branch_weights(w_branch[l]), gates, w_out[l].astype(BF16), xs, mod,
                          2, group_of, ln1_g[l], ln1_b[l], alpha)

        pq = token_matmul(xs, peer_w_q[l].astype(BF16), prologue="modulate", mod=mod, mod_idx=(3, 4),
                          group_of=group_of, out_dtype=BF16, name="peer_query")
        keys = peer_sub_keys[l].reshape(2 * PEER_HEADS, PEER_KEYS, PEER_QDIM // 2).astype(BF16)
        cnt, e0, rank, e1 = peer_topk(pq, keys)
        ht = modulate_transposed(xs, mod, (3, 4), group_of)
        n_exp = peer_u.shape[1]
        vt = jnp.transpose(peer_v[l].astype(BF16).reshape(n_exp // PEER_EB, PEER_EB, d), (0, 2, 1))
        ffn_t = peer_dense(ht, peer_u[l].astype(BF16), vt, cnt, e0, rank, e1)
        xs = residual_ln(xs, ffn_t, mod, 5, group_of, ln2_g[l], ln2_b[l], alpha)

    return xs[:t_lat].reshape(bsz, seq, d)
```

```python
import functools
import math

import jax
import jax.numpy as jnp
import numpy as np
from jax import lax
from jax.experimental import pallas as pl
from jax.experimental.pallas import tpu as pltpu

F32 = jnp.float32
BF16 = jnp.bfloat16

GRID_W = 64
ROPE_THETA = 10000.0
LN_EPS = 1e-5
RMS_EPS = 1e-6
NEG_INF = -1e30

A_HEADS, A_NOPE, A_ROPE, A_V, A_Q_LORA, A_KV_LORA = 4, 128, 64, 128, 384, 128
B_HEADS, B_KV_HEADS, B_DIM, B_WINDOW = 8, 2, 64, 128
C_HEADS, C_DIM, NA_ROWS, NA_COLS = 4, 128, 8, 16
D_HEADS, D_KV_HEADS, D_DIM = 4, 2, 128
N_BRANCH, BRANCH_W = 4, 512
PEER_HEADS, PEER_KEYS, PEER_QDIM, PEER_TOPK = 8, 128, 256, 16

A_SCALE = (A_NOPE + A_ROPE) ** -0.5
B_SCALE = B_DIM ** -0.5
C_SCALE = C_DIM ** -0.5
D_SCALE = D_DIM ** -0.5

IN_SIZES = (A_Q_LORA, A_KV_LORA, A_ROPE,
            B_HEADS * B_DIM, B_KV_HEADS * B_DIM, B_KV_HEADS * B_DIM,
            C_HEADS * C_DIM, C_HEADS * C_DIM, C_HEADS * C_DIM,
            D_HEADS * D_DIM, D_KV_HEADS * D_DIM, D_KV_HEADS * D_DIM)
LANE = 128
BF16_ROWS = 16

VMEM_LIMIT = 56 * 1024 * 1024

PEER_CHUNK = 512
PEER_EB = 1024
PEER_TOPK_TILE = 256

B_TQ, B_BACK, B_WK = 512, 128, 768
C_TQ, C_BACK, C_WK = 4 * GRID_W, 4 * GRID_W, 12 * GRID_W
DENSE_TQ = 1024
DENSE_SUB = 4
ROPE_TM = 512

R_BQ, R_KR, R_BK, R_DQ, R_DK = 0, 8, 9, 10, 14
U_PREP_N, U_PREP_R32, U_PREP_RMSQ = 16, 10, 4
U_CQ_, U_CK, U_CV, U_CKV, U_BV, U_CQ, U_DV, U_BLOCKS = 16, 20, 24, 28, 29, 30, 33, 36


def _cparams(sem):
    return pltpu.CompilerParams(dimension_semantics=sem, vmem_limit_bytes=VMEM_LIMIT)


def _pick(n, cands):
    for c in cands:
        if n % c == 0:
            return c
    raise ValueError(f"no tile in {cands} divides {n}")


def _dot(a, b):
    return jnp.dot(a, b, preferred_element_type=F32)


def _dot_nt(a, b):
    return lax.dot_general(a, b, (((1,), (1,)), ((), ())), preferred_element_type=F32)


def _ada_kernel(c_ref, w_ref, b_ref, o_ref):
    c = c_ref[...]
    s = c * (1.0 / (1.0 + jnp.exp(-c)))
    w = w_ref[0]
    s_hi = s.astype(BF16)
    s_lo = (s - s_hi.astype(F32)).astype(BF16)
    w_hi = w.astype(BF16)
    w_lo = (w - w_hi.astype(F32)).astype(BF16)
    acc = _dot(s_hi, w_hi) + _dot(s_lo, w_hi) + _dot(s_hi, w_lo)
    o_ref[...] = acc + b_ref[...]


def ada_modulation(cc, w_all, b, layer):
    r, k = cc.shape
    n = w_all.shape[2]
    tn = _pick(n, (512, 256, 128))
    return pl.pallas_call(
        _ada_kernel,
        grid=(n // tn,),
        in_specs=[pl.BlockSpec((r, k), lambda j: (0, 0)),
                  pl.BlockSpec((1, k, tn), lambda j: (layer, 0, j)),
                  pl.BlockSpec((1, tn), lambda j: (0, j))],
        out_specs=pl.BlockSpec((r, tn), lambda j: (0, j)),
        out_shape=jax.ShapeDtypeStruct((r, n), F32),
        compiler_params=_cparams(("arbitrary",)),
        name="ada_modulation",
    )(cc, w_all, b.reshape(1, n))


def _mm_kernel(*refs, prologue, has_bias, act):
    it = iter(refs)
    x_ref = next(it)
    if prologue == "modulate":
        sh_ref, sc_ref = next(it), next(it)
    elif prologue == "rmsnorm":
        g_ref = next(it)
    w_ref = next(it)
    b_ref = next(it) if has_bias else None
    o_ref = next(it)
    a_scr = next(it)

    @pl.when(pl.program_id(1) == 0)
    def _():
        x = x_ref[...].astype(F32)
        if prologue == "modulate":
            a = x * (1.0 + sc_ref[0]) + sh_ref[0]
        elif prologue == "rmsnorm":
            ms = jnp.mean(x * x, axis=-1, keepdims=True)
            a = (x * lax.rsqrt(ms + RMS_EPS)) * g_ref[...]
        else:
            a = x
        a_scr[...] = a.astype(BF16)

    acc = _dot(a_scr[...], w_ref[...])
    if has_bias:
        acc = acc + b_ref[...]
    if act == "sigmoid":
        acc = 1.0 / (1.0 + jnp.exp(-acc))
    o_ref[...] = acc.astype(o_ref.dtype)


def token_matmul(x, w, *, rows=None, x_cols=None, prologue=None, mod=None, mod_idx=None, group_of=None,
                 norm_g=None, bias=None, act=None, out_dtype=F32, name="token_matmul"):
    rows = x.shape[0] if rows is None else rows
    kdim, n = w.shape
    kw, kblk = (x.shape[1], 0) if x_cols is None else x_cols
    assert kw == kdim
    tm = _pick(rows, (1024, 512, 256))
    tn = _pick(n, (2048, 1536, 1024, 512, 256, 128))
    in_specs = [pl.BlockSpec((tm, kdim), lambda i, j: (i, kblk))]
    args = [x]
    if prologue == "modulate":
        tm_group = group_of(tm)
        for slot in mod_idx:
            in_specs.append(pl.BlockSpec((1, 1, kdim), lambda i, j, s=slot: (tm_group(i) * 6 + s, 0, 0)))
            args.append(mod)
    elif prologue == "rmsnorm":
        in_specs.append(pl.BlockSpec((1, kdim), lambda i, j: (0, 0)))
        args.append(norm_g.reshape(1, kdim))
    in_specs.append(pl.BlockSpec((kdim, tn), lambda i, j: (0, j)))
    args.append(w)
    if bias is not None:
        in_specs.append(pl.BlockSpec((1, tn), lambda i, j: (0, j)))
        args.append(bias.reshape(1, n))
    return pl.pallas_call(
        functools.partial(_mm_kernel, prologue=prologue, has_bias=bias is not None, act=act),
        grid=(rows // tm, n // tn),
        in_specs=in_specs,
        out_specs=pl.BlockSpec((tm, tn), lambda i, j: (i, j)),
        out_shape=jax.ShapeDtypeStruct((rows, n), out_dtype),
        scratch_shapes=[pltpu.VMEM((tm, kdim), BF16)],
        compiler_params=_cparams(("parallel", "arbitrary")),
        name=name,
    )(*args)


def _res_ln_kernel(x_ref, yt_ref, gate_ref, g_ref, b_ref, o_ref, *, alpha):
    z = alpha * x_ref[...] + gate_ref[0] * yt_ref[0].T
    mu = jnp.mean(z, axis=-1, keepdims=True)
    zc = z - mu
    var = jnp.mean(zc * zc, axis=-1, keepdims=True)
    o_ref[...] = (zc * lax.rsqrt(var + LN_EPS)) * g_ref[...] + b_ref[...]


def residual_ln(x, yt, mod, gate_slot, group_of, ln_g, ln_b, alpha):
    nch, d, tm = yt.shape
    rows = nch * tm
    tm_group = group_of(tm)
    return pl.pallas_call(
        functools.partial(_res_ln_kernel, alpha=alpha),
        grid=(nch,),
        in_specs=[pl.BlockSpec((tm, d), lambda i: (i, 0)),
                  pl.BlockSpec((1, d, tm), lambda i: (i, 0, 0)),
                  pl.BlockSpec((1, 1, d), lambda i: (tm_group(i) * 6 + gate_slot, 0, 0)),
                  pl.BlockSpec((1, d), lambda i: (0, 0)),
                  pl.BlockSpec((1, d), lambda i: (0, 0))],
        out_specs=pl.BlockSpec((tm, d), lambda i: (i, 0)),
        out_shape=jax.ShapeDtypeStruct((rows, d), F32),
        compiler_params=_cparams(("parallel",)),
        name="residual_ln",
    )(x, yt, mod, ln_g.reshape(1, d), ln_b.reshape(1, d))


def _qk_prep_kernel(x_ref, c32_ref, sa_ref, sb_ref, c64_ref, s64_ref, gq_ref, gk_ref, o_ref, *, n_r32, n_rms_q):
    for j in range(x_ref.shape[1] // LANE):
        cols = slice(j * LANE, (j + 1) * LANE)
        x = x_ref[:, cols].astype(F32)
        if j < n_r32:
            y = x * c32_ref[...] + pltpu.roll(x, 32, 1) * sa_ref[...] + pltpu.roll(x, 96, 1) * sb_ref[...]
        else:
            g = gq_ref[...] if j < n_r32 + n_rms_q else gk_ref[...]
            ms = jnp.mean(x * x, axis=-1, keepdims=True)
            y = (x * lax.rsqrt(ms + RMS_EPS)) * g
            y = y * c64_ref[...] + pltpu.roll(y, 64, 1) * s64_ref[...]
        o_ref[:, cols] = y.astype(o_ref.dtype)


def qk_prep(x, group_idx, n_blocks, n_r32, n_rms_q, tables, gq, gk, lat_rows, seq, name):
    rows = x.shape[0]
    tm = ROPE_TM
    width = n_blocks * LANE
    lat_blocks, per_seq = lat_rows // tm, seq // tm
    tab_spec = pl.BlockSpec((tm, LANE), lambda i: (jnp.where(i < lat_blocks, i % per_seq, per_seq), 0))
    vec_spec = pl.BlockSpec((1, LANE), lambda i: (0, 0))
    return pl.pallas_call(
        functools.partial(_qk_prep_kernel, n_r32=n_r32, n_rms_q=n_rms_q),
        grid=(rows // tm,),
        in_specs=[pl.BlockSpec((tm, width), lambda i: (i, group_idx))] + [tab_spec] * 5 + [vec_spec] * 2,
        out_specs=pl.BlockSpec((tm, width), lambda i: (i, 0)),
        out_shape=jax.ShapeDtypeStruct((rows, width), BF16),
        compiler_params=_cparams(("parallel",)),
        name=name,
    )(x, *tables, gq.reshape(1, LANE), gk.reshape(1, LANE))


def _attn_kernel(*refs, n_parts, has_lat, window, has_bias, bias_per_head, has_sink, scale, n_lat, n_sub, hps,
                 per_head):
    it = iter(refs)
    q_refs = [next(it) for _ in range(n_parts)]
    if has_lat:
        kl_refs = [next(it) for _ in range(n_parts)]
        vl_ref = next(it)
    kc_refs = [next(it) for _ in range(n_parts)]
    vc_ref = next(it)
    bias_ref = next(it) if has_bias else None
    sink_ref = next(it) if has_sink else None
    refs = list(it)
    o_ref = refs[-1]

    if has_lat and window is not None:
        tq, back, wk = window
        start = pl.multiple_of(jnp.clip(pl.program_id(2) * tq - back, 0, n_lat - wk), LANE)
        key_rows = pl.ds(start, wk)
    else:
        key_rows = slice(None)
    sub = o_ref.shape[0] // n_sub
    for hh in range(hps):
        lanes = slice(hh * LANE, (hh + 1) * LANE)

        def cols(j):
            return lanes if per_head[j] else slice(None)

        kcs = [jnp.concatenate([k[:, cols(j)] for j, k in enumerate(kc_refs)], axis=1)]
        vc = vc_ref[:, cols(n_parts)]
        if has_lat:
            kls = [jnp.concatenate([k[key_rows, cols(j)] for j, k in enumerate(kl_refs)], axis=1)]
            vl = vl_ref[key_rows, cols(n_parts)]
        if has_sink:
            sink = sink_ref[pl.program_id(1) * hps + hh]
        for sb in range(n_sub):
            rows = slice(sb * sub, (sb + 1) * sub)
            qs = [jnp.concatenate([q[rows, lanes] for q in q_refs], axis=1)]
            fold = not has_bias and not has_sink
            post = (lambda t: t) if fold else (lambda t: t * scale)
            expo = (lambda t: jnp.exp2(t * (scale * math.log2(math.e)))) if fold else jnp.exp
            s2 = post(functools.reduce(jnp.add, [_dot_nt(q, k) for q, k in zip(qs, kcs)]))
            m = jnp.max(s2, axis=-1, keepdims=True)
            if has_lat:
                s1 = post(functools.reduce(jnp.add, [_dot_nt(q, k) for q, k in zip(qs, kls)]))
                if has_bias:
                    s1 = s1 + bias_ref[0, hh if bias_per_head else 0, rows, :]
                m = jnp.maximum(m, jnp.max(s1, axis=-1, keepdims=True))
            if has_sink:
                m = jnp.maximum(m, sink)
            p2 = expo(s2 - m)
            l = jnp.sum(p2, axis=-1, keepdims=True)
            o = _dot(p2.astype(BF16), vc)
            if has_lat:
                p1 = expo(s1 - m)
                l = l + jnp.sum(p1, axis=-1, keepdims=True)
                o = o + _dot(p1.astype(BF16), vl)
            if has_sink:
                l = l + jnp.exp(sink - m)
            o_ref[rows, lanes] = (o / l).astype(o_ref.dtype)


def attention(q_srcs, k_srcs, v_src, *, heads, scale, bsz, seq, nctx, t_lat, mode, window=None, bias=None, sink=None,
              heads_per_step=1, name="attention"):
    lat = mode == "lat"
    hps = heads_per_step
    n_parts = len(q_srcs)
    tq = (window[0] if window is not None else DENSE_TQ) if lat else nctx
    nqb = (seq if lat else nctx) // tq
    ctx0 = t_lat // nctx
    kv_srcs = list(k_srcs) + [v_src]
    per_head = tuple(grp == 1 for _, _, grp in kv_srcs)
    assert heads % hps == 0 and all(first % hps == 0 for _, first in q_srcs)
    assert hps == 1 or all(grp == 0 or (grp == 1 and first % hps == 0) for _, first, grp in kv_srcs)

    def kv_spec(rows, row_blk, first, grp):
        if hps > 1 and grp == 1:
            return pl.BlockSpec((rows, hps * LANE), lambda bi, hg, i: (row_blk(bi), first // hps + hg))
        return pl.BlockSpec((rows, LANE), lambda bi, hg, i: (row_blk(bi), first + (hg // grp if grp else 0)))

    def q_row(bi, i):
        return bi * nqb + i if lat else t_lat // tq + bi

    in_specs, args = [], []
    for arr, first in q_srcs:
        in_specs.append(pl.BlockSpec((tq, hps * LANE), lambda bi, hg, i, f=first: (q_row(bi, i), f // hps + hg)))
        args.append(arr)
    if lat:
        for arr, first, grp in kv_srcs:
            in_specs.append(kv_spec(seq, lambda bi: bi, first, grp))
            args.append(arr)
    for arr, first, grp in kv_srcs:
        in_specs.append(kv_spec(nctx, lambda bi: ctx0 + bi, first, grp))
        args.append(arr)
    bias_per_head = bias is not None and bias.shape[1] != 1
    if bias is not None:
        in_specs.append(pl.BlockSpec(
            (1, hps if bias_per_head else 1, tq, window[2]),
            lambda bi, hg, i: (jnp.where(i == 0, 0, jnp.where(i == nqb - 1, 2, 1)), hg if bias_per_head else 0, 0, 0)))
        args.append(bias)
    if sink is not None:
        in_specs.append(pl.BlockSpec(memory_space=pltpu.SMEM))
        args.append(sink.astype(F32))
    if window is not None:
        assert seq % tq == 0 and nqb >= 2 and seq >= window[2]
    return pl.pallas_call(
        functools.partial(_attn_kernel, n_parts=n_parts, has_lat=lat, window=window, has_bias=bias is not None,
                          bias_per_head=bias_per_head, has_sink=sink is not None, scale=scale, n_lat=seq,
                          n_sub=DENSE_SUB if (lat and window is None) else 1, hps=hps, per_head=per_head),
        grid=(bsz, heads // hps, nqb),
        in_specs=in_specs,
        out_specs=pl.BlockSpec((tq, hps * LANE), lambda bi, hg, i: (bi * nqb + i, hg)),
        out_shape=jax.ShapeDtypeStruct((bsz * nqb * tq, heads * LANE), BF16),
        compiler_params=_cparams(("parallel", "parallel", "arbitrary")),
        name=name,
    )(*args)


def _window_q_offsets(geom):
    tq, back, wk = geom
    return (0, back, wk - tq)


def band_bias_table():
    tq, _, wk = (B_TQ, B_BACK, B_WK)
    kpos = np.arange(wk)[None, :]
    tabs = []
    for off in _window_q_offsets((B_TQ, B_BACK, B_WK)):
        qpos = off + np.arange(tq)[:, None]
        tabs.append(np.where(np.abs(qpos - kpos) <= B_WINDOW, 0.0, NEG_INF))
    return jnp.asarray(np.stack(tabs)[:, None], F32)


def neighbourhood_bias_table(rpb, rows):
    tq, back, wk = (C_TQ, C_BACK, C_WK)
    assert rows >= wk // GRID_W and rows >= NA_ROWS
    nqb = rows * GRID_W // tq
    nq_rows, nk_rows = tq // GRID_W, wk // GRID_W
    cols = np.arange(GRID_W)
    col_start = np.clip(cols - NA_COLS // 2, 0, GRID_W - NA_COLS)
    col_ok = (cols[None, :] >= col_start[:, None]) & (cols[None, :] < col_start[:, None] + NA_COLS)
    col_off = np.clip(cols[None, :] - cols[:, None] + (NA_COLS - 1), 0, 2 * NA_COLS - 2)
    onehot_c = jnp.asarray(col_off[..., None] == np.arange(2 * NA_COLS - 1), F32)
    tabs = []
    for cls, off in enumerate(_window_q_offsets((C_TQ, C_BACK, C_WK))):
        blk = (0, 1, nqb - 1)[cls]
        w0 = int(np.clip(blk * tq - back, 0, rows * GRID_W - wk)) // GRID_W
        assert w0 * GRID_W + off == blk * tq and off % GRID_W == 0
        q_row = w0 + off // GRID_W + np.arange(nq_rows)
        k_row = w0 + np.arange(nk_rows)
        row_start = np.clip(q_row - NA_ROWS // 2, 0, rows - NA_ROWS)
        row_ok = (k_row[None, :] >= row_start[:, None]) & (k_row[None, :] < row_start[:, None] + NA_ROWS)
        row_off = np.clip(k_row[None, :] - q_row[:, None] + (NA_ROWS - 1), 0, 2 * NA_ROWS - 2)
        by_row = rpb[:, row_off, :].astype(F32)
        vals = jnp.einsum("hrsc,qkc->hrqsk", by_row, onehot_c, precision=lax.Precision.HIGHEST)
        ok = row_ok[:, None, :, None] & col_ok[None, :, None, :]
        tabs.append(jnp.where(jnp.asarray(ok)[None], vals, NEG_INF).reshape(rpb.shape[0], tq, wk))
    return jnp.stack(tabs)


def _merge_out_ln_kernel(*refs, alpha, n_branch, lat_blocks):
    o_refs = refs[:n_branch]
    refs = refs[n_branch:]
    if lat_blocks is not None:
        oc_refs = refs[:n_branch]
        refs = refs[n_branch:]
        is_ctx = pl.program_id(0) >= lat_blocks
    wbr_refs = refs[:n_branch]
    g_ref, wout_ref, x_ref, gate_ref, lng_ref, lnb_ref, out_ref = refs[n_branch:]
    d = x_ref.shape[1]
    y = None
    for r, (o_ref, wbr_ref) in enumerate(zip(o_refs, wbr_refs)):
        o = o_ref[...]
        if lat_blocks is not None:
            o = jnp.where(is_ctx, oc_refs[r][...], o)
        t = g_ref[:, r * d:(r + 1) * d].astype(F32) * _dot(o, wbr_ref[...])
        y = t if y is None else y + t
    mix = _dot(y.astype(BF16), wout_ref[...])
    z = alpha * x_ref[...] + gate_ref[0] * mix
    mu = jnp.mean(z, axis=-1, keepdims=True)
    zc = z - mu
    var = jnp.mean(zc * zc, axis=-1, keepdims=True)
    out_ref[...] = (zc * lax.rsqrt(var + LN_EPS)) * lng_ref[...] + lnb_ref[...]


def merge_out_ln(branches, branches_ctx, w_branch, gates, w_out, x, mod, gate_slot, group_of, ln_g, ln_b, alpha):
    rows = gates.shape[0]
    nbr = len(branches)
    d = w_out.shape[0]
    tm = _pick(rows, (256,))
    tm_group = group_of(tm)
    resident = pl.Buffered(1)
    lat_blocks = branches[0].shape[0] // tm
    last = lat_blocks - 1
    branch_specs = [pl.BlockSpec((tm, o.shape[1]), lambda i: (jnp.minimum(i, last), 0)) for o in branches]
    if branches_ctx is not None:
        assert rows == branches[0].shape[0] + branches_ctx[0].shape[0]
        branch_specs += [pl.BlockSpec((tm, o.shape[1]), lambda i: (jnp.maximum(i - lat_blocks, 0), 0))
                         for o in branches_ctx]
    else:
        assert rows == branches[0].shape[0]
    return pl.pallas_call(
        functools.partial(_merge_out_ln_kernel, alpha=alpha, n_branch=nbr,
                          lat_blocks=None if branches_ctx is None else lat_blocks),
        grid=(rows // tm,),
        in_specs=branch_specs + [
                  pl.BlockSpec(w.shape, lambda i: (0, 0), pipeline_mode=resident) for w in w_branch] + [
                  pl.BlockSpec((tm, nbr * d), lambda i: (i, 0)),
                  pl.BlockSpec((d, d), lambda i: (0, 0), pipeline_mode=resident),
                  pl.BlockSpec((tm, d), lambda i: (i, 0)),
                  pl.BlockSpec((1, 1, d), lambda i: (tm_group(i) * 6 + gate_slot, 0, 0)),
                  pl.BlockSpec((1, d), lambda i: (0, 0)),
                  pl.BlockSpec((1, d), lambda i: (0, 0))],
        out_specs=pl.BlockSpec((tm, d), lambda i: (i, 0)),
        out_shape=jax.ShapeDtypeStruct((rows, d), F32),
        compiler_params=_cparams(("parallel",)),
        name="merge_out_ln",
    )(*branches, *(branches_ctx or []), *w_branch, gates, w_out, x, mod, ln_g.reshape(1, d), ln_b.reshape(1, d))


def _peer_pairs():
    return [(a, b) for a in range(PEER_TOPK) for b in range(PEER_TOPK) if (a + 1) * (b + 1) <= PEER_TOPK]


def _peer_topk_kernel(q_ref, keys_ref, cnt_ref, e0_ref, rank_ref, e1_ref, s0_scr):
    tops = [[[None] * PEER_HEADS for _ in range(PEER_TOPK)] for _ in range(2)]
    for h in range(PEER_HEADS):
        for p in range(2):
            hp = 2 * h + p
            qs = q_ref[:, hp * PEER_KEYS:(hp + 1) * PEER_KEYS]
            s = _dot_nt(keys_ref[hp], qs)
            work = s
            rank = jnp.full(s.shape, float(PEER_TOPK), F32)
            for r in range(PEER_TOPK):
                m = jnp.max(work, axis=0, keepdims=True)
                tops[p][r][h] = m
                if p == 1:
                    hit = work == m
                    rank = jnp.where(hit, float(r), rank)
                    work = jnp.where(hit, -jnp.inf, work)
                elif r + 1 < PEER_TOPK:
                    work = jnp.where(work == m, -jnp.inf, work)
            if p == 0:
                s0_scr[h] = s
            else:
                rank_ref[h, 0] = rank.astype(BF16)
                e1_ref[h, 0] = jnp.exp(s - tops[1][0][h]).astype(BF16)
    top = [[jnp.concatenate(tops[p][r], axis=0) for r in range(PEER_TOPK)] for p in range(2)]
    cands = [top[0][a] + top[1][b] for a, b in _peer_pairs()]
    work = list(cands)
    for r in range(PEER_TOPK):
        m = functools.reduce(jnp.maximum, work)
        if r + 1 < PEER_TOPK:
            work = [jnp.where(w == m, -jnp.inf, w) for w in work]
    tau = m
    cmax = cands[0]
    z = functools.reduce(jnp.add, [jnp.where(c >= tau, jnp.exp(c - cmax), 0.0) for c in cands])
    rz = 1.0 / z
    b_dense = 4
    a_sparse = PEER_TOPK // (b_dense + 1)
    cnt_hi = [functools.reduce(jnp.add, [jnp.where(top[0][a] + top[1][b] >= tau, 1.0, 0.0)
                                          for b in range(b_dense, PEER_TOPK)]) for a in range(a_sparse)]
    for h in range(PEER_HEADS):
        s0 = s0_scr[h]
        tau_h = tau[h:h + 1]
        cnt = None
        for b in range(b_dense):
            inc = jnp.where(s0 + top[1][b][h:h + 1] >= tau_h, 1.0, 0.0)
            cnt = inc if cnt is None else cnt + inc
        for a in range(a_sparse):
            cnt = cnt + jnp.where(s0 == top[0][a][h:h + 1], cnt_hi[a][h:h + 1], 0.0)
        cnt_ref[h, 0] = cnt
        e0_ref[h, 0] = jnp.exp(s0 - top[0][0][h:h + 1]) * rz[h:h + 1]


def peer_topk(q, keys):
    rows = q.shape[0]
    tt = PEER_TOPK_TILE
    per_chunk = PEER_CHUNK // tt
    nch = rows // PEER_CHUNK
    big = pl.BlockSpec((PEER_HEADS, 1, PEER_KEYS, tt), lambda i: (0, i // per_chunk, 0, i % per_chunk))
    shape_f32 = jax.ShapeDtypeStruct((PEER_HEADS, nch, PEER_KEYS, PEER_CHUNK), F32)
    shape_bf16 = jax.ShapeDtypeStruct((PEER_HEADS, nch, PEER_KEYS, PEER_CHUNK), BF16)
    return pl.pallas_call(
        _peer_topk_kernel,
        grid=(rows // tt,),
        in_specs=[pl.BlockSpec((tt, q.shape[1]), lambda i: (i, 0)),
                  pl.BlockSpec(keys.shape, lambda i: (0, 0, 0))],
        out_specs=[big, big, big, big],
        out_shape=[shape_f32, shape_f32, shape_bf16, shape_bf16],
        scratch_shapes=[pltpu.VMEM((PEER_HEADS, PEER_KEYS, tt), F32)],
        compiler_params=_cparams(("parallel",)),
        name="peer_topk",
    )(q, keys)


def _modulate_t_kernel(x_ref, sh_ref, sc_ref, o_ref):
    h = x_ref[...] * (1.0 + sc_ref[0]) + sh_ref[0]
    o_ref[0] = h.T.astype(BF16)


def modulate_transposed(x, mod, mod_idx, group_of):
    rows, d = x.shape
    tt = PEER_CHUNK
    tt_group = group_of(tt)
    mod_specs = [pl.BlockSpec((1, 1, d), lambda i, s=slot: (tt_group(i) * 6 + s, 0, 0)) for slot in mod_idx]
    return pl.pallas_call(
        _modulate_t_kernel,
        grid=(rows // tt,),
        in_specs=[pl.BlockSpec((tt, d), lambda i: (i, 0))] + mod_specs,
        out_specs=pl.BlockSpec((1, d, tt), lambda i: (i, 0, 0)),
        out_shape=jax.ShapeDtypeStruct((rows // tt, d, tt), BF16),
        compiler_params=_cparams(("parallel",)),
        name="modulate_transposed",
    )(x, mod, mod)


def _peer_dense_kernel(ht_ref, u_ref, vt_ref, cnt_ref, e0_ref, rank_ref, e1_ref, o_ref, a_scr, w_scr, p_scr, *, n_e,
                       n_pairs):
    s = pl.program_id(0)
    e = s % n_e
    cur = s % 2
    prev = 1 - cur

    @pl.when(jnp.logical_and(s > 0, (s - 1) % n_e == 0))
    def _():
        o_ref[...] = jnp.zeros_like(o_ref)
    i_per_eb = PEER_EB // PEER_KEYS
    tc = w_scr.shape[-1]

    def gate_and_down():
        for ii in range(i_per_eb):
            sl = slice(ii * PEER_KEYS, (ii + 1) * PEER_KEYS)
            a = a_scr[prev, sl, :]
            act = 0.5 * a * (1.0 + lax.erf(a * (1.0 / math.sqrt(2.0))))
            p_scr[sl, :] = act.astype(BF16) * w_scr[prev, sl, :]
        o_ref[0] += _dot(vt_ref[0], p_scr[...])

    def up_and_weights():
        a_scr[cur] = _dot(u_ref[...], ht_ref[0])
        for ii in range(i_per_eb):
            i = e * i_per_eb + ii
            cnt = [jnp.broadcast_to(cnt_ref[h, 0, pl.ds(i, 1), :], (BF16_ROWS, tc)).astype(BF16)
                   for h in range(PEER_HEADS)]
            e0 = [jnp.broadcast_to(e0_ref[h, 0, pl.ds(i, 1), :], (BF16_ROWS, tc)).astype(BF16)
                  for h in range(PEER_HEADS)]
            for g in range(PEER_KEYS // BF16_ROWS):
                rows = slice(g * BF16_ROWS, (g + 1) * BF16_ROWS)
                w = None
                for h in range(PEER_HEADS):
                    e1 = e1_ref[h, 0, rows, :]
                    t = jnp.where(rank_ref[h, 0, rows, :] < cnt[h], e1, jnp.zeros_like(e1)) * e0[h]
                    w = t if w is None else w + t
                w_scr[cur, ii * PEER_KEYS + g * BF16_ROWS:ii * PEER_KEYS + (g + 1) * BF16_ROWS, :] = w

    @pl.when(s == 0)
    def _():
        up_and_weights()

    @pl.when(jnp.logical_and(s > 0, s < n_pairs))
    def _():
        gate_and_down()
        up_and_weights()

    @pl.when(s == n_pairs)
    def _():
        gate_and_down()


def peer_dense(ht, u, vt, cnt, e0, rank, e1):
    nch, d, tc = ht.shape
    n_e = u.shape[0] // PEER_EB
    n_pairs = nch * n_e

    def cur_chunk(s):
        return jnp.minimum(s, n_pairs - 1) // n_e

    def prev_chunk(s):
        return jnp.maximum(s - 1, 0) // n_e

    sel = pl.BlockSpec((PEER_HEADS, 1, PEER_KEYS, tc), lambda s: (0, cur_chunk(s), 0, 0))
    return pl.pallas_call(
        functools.partial(_peer_dense_kernel, n_e=n_e, n_pairs=n_pairs),
        grid=(n_pairs + 1,),
        in_specs=[pl.BlockSpec((1, d, tc), lambda s: (cur_chunk(s), 0, 0)),
                  pl.BlockSpec((PEER_EB, d), lambda s: (jnp.minimum(s, n_pairs - 1) % n_e, 0)),
                  pl.BlockSpec((1, d, PEER_EB), lambda s: (jnp.maximum(s - 1, 0) % n_e, 0, 0)),
                  sel, sel, sel, sel],
        out_specs=pl.BlockSpec((1, d, tc), lambda s: (prev_chunk(s), 0, 0)),
        out_shape=jax.ShapeDtypeStruct((nch, d, tc), F32),
        scratch_shapes=[pltpu.VMEM((2, PEER_EB, tc), F32), pltpu.VMEM((2, PEER_EB, tc), BF16),
                        pltpu.VMEM((PEER_EB, tc), BF16)],
        compiler_params=_cparams(("arbitrary",)),
        name="peer_dense",
    )(ht, u, vt, cnt, e0, rank, e1)


def _axial_angles(n, rot_dim):
    t = jnp.arange(n)
    rows = (t // GRID_W).astype(F32)
    cols = (t % GRID_W).astype(F32)
    nf = rot_dim // 4
    inv = ROPE_THETA ** (-jnp.arange(nf, dtype=F32) / nf)
    ang = jnp.concatenate([rows[:, None] * inv, cols[:, None] * inv], axis=-1)
    return jnp.cos(ang), jnp.sin(ang)


def _rope_tables(seq):
    assert A_ROPE == B_DIM == LANE // 2 and D_DIM == LANE

    def pad(t, fill):
        return jnp.concatenate([t, jnp.full((ROPE_TM, LANE), fill, F32)], axis=0)

    cos, sin = _axial_angles(seq, B_DIM)
    z = jnp.zeros_like(sin)
    c32 = jnp.concatenate([cos, cos, cos, cos], axis=1)
    sa = jnp.concatenate([z, sin, z, sin], axis=1)
    sb = jnp.concatenate([-sin, z, -sin, z], axis=1)
    cos, sin = _axial_angles(seq, D_DIM)
    c64 = jnp.concatenate([cos, cos], axis=1)
    s64 = jnp.concatenate([-sin, sin], axis=1)
    return [pad(c32, 1.0), pad(sa, 0.0), pad(sb, 0.0), pad(c64, 1.0), pad(s64, 0.0)]


def _half_block(t, first_half, axis):
    z = jnp.zeros_like(t)
    return jnp.concatenate([t, z] if first_half else [z, t], axis=axis)


def _in_proj_weight(w_in):
    d = w_in.shape[0]
    o = np.cumsum((0,) + IN_SIZES)
    part = [w_in[:, o[i]:o[i + 1]] for i in range(len(IN_SIZES))]
    per_kv = B_HEADS // B_KV_HEADS
    bq = [_half_block(part[3][:, h * B_DIM:(h + 1) * B_DIM], h // per_kv == 0, 1) for h in range(B_HEADS)]
    cols = bq + [_half_block(part[2], True, 1), part[4], part[9], part[10], part[6], part[7], part[8], part[1],
                 part[5], part[0], part[11], jnp.zeros((d, LANE), w_in.dtype)]
    out = jnp.concatenate(cols, axis=1).astype(BF16)
    assert out.shape[1] == U_BLOCKS * LANE
    return out


def _mla_up_weights(w_uq, w_ukv):
    dq, dkv = A_NOPE + A_ROPE, A_NOPE + A_V
    q_nope = [w_uq[:, h * dq:h * dq + A_NOPE] for h in range(A_HEADS)]
    q_rope = [_half_block(w_uq[:, h * dq + A_NOPE:(h + 1) * dq], True, 1) for h in range(A_HEADS)]
    k_nope = [w_ukv[:, h * dkv:h * dkv + A_NOPE] for h in range(A_HEADS)]
    v = [w_ukv[:, h * dkv + A_NOPE:(h + 1) * dkv] for h in range(A_HEADS)]
    return (jnp.concatenate(q_nope + q_rope, axis=1).astype(BF16), jnp.concatenate(k_nope + v, axis=1).astype(BF16))


def _branch_weights(w_branch):
    per_kv = B_HEADS // B_KV_HEADS
    wb = jnp.concatenate([_half_block(w_branch[1][h * B_DIM:(h + 1) * B_DIM], h // per_kv == 0, 0)
                          for h in range(B_HEADS)], axis=0)
    return [w_branch[0].astype(BF16), wb.astype(BF16), w_branch[2].astype(BF16), w_branch[3].astype(BF16)]


def kernel(x, c, ctx, c_ctx, w_ada, b_ada, w_in, a_q_norm, a_kv_norm, a_w_uq, a_w_ukv, b_sink, c_rpb, d_q_norm, d_k_norm, w_branch, w_gate, b_gate, w_out, ln1_g, ln1_b, ln2_g, ln2_b, peer_w_q, peer_sub_keys, peer_u, peer_v):
    bsz, seq, d = x.shape
    nctx = ctx.shape[1]
    depth = w_in.shape[0]
    t_lat, t_ctx = bsz * seq, bsz * nctx
    t_all = t_lat + t_ctx
    rows_grid = seq // GRID_W
    alpha = (2 * depth) ** 0.25

    tables = _rope_tables(seq)
    band_bias = band_bias_table()
    dims = dict(bsz=bsz, seq=seq, nctx=nctx, t_lat=t_lat)

    n_groups = -(-(bsz + 1) // 8) * 8
    cc = jnp.zeros((n_groups, d), F32).at[:bsz].set(c).at[bsz].set(c_ctx)

    def group_of(tm):
        lat_blocks, per_batch = t_lat // tm, seq // tm
        assert t_lat % tm == 0 and seq % tm == 0
        return lambda i: jnp.where(i < lat_blocks, i // per_batch, bsz)

    xs = jnp.concatenate([x.reshape(t_lat, d), ctx.reshape(t_ctx, d)], axis=0)

    for l in range(depth):
        need_ctx = l < depth - 1
        rows_out = t_all if need_ctx else t_lat
        mod = ada_modulation(cc, w_ada, b_ada[l], l).reshape(n_groups * 6, 1, d)

        u = token_matmul(xs, _in_proj_weight(w_in[l]), prologue="modulate", mod=mod, mod_idx=(0, 1),
                         group_of=group_of, out_dtype=BF16, name="in_proj")
        w_gate_l = jnp.transpose(w_gate[l], (1, 0, 2)).reshape(d, N_BRANCH * d).astype(BF16)
        gates = token_matmul(xs, w_gate_l, rows=rows_out, prologue="modulate", mod=mod, mod_idx=(0, 1),
                             group_of=group_of, bias=b_gate[l].reshape(-1), act="sigmoid", out_dtype=BF16,
                             name="branch_gates")

        w_uq, w_ukv = _mla_up_weights(a_w_uq[l], a_w_ukv[l])
        qa = token_matmul(u, w_uq, x_cols=(A_Q_LORA, U_CQ * LANE // A_Q_LORA), prologue="rmsnorm", norm_g=a_q_norm[l], out_dtype=BF16,
                          name="mla_q_up")
        kva = token_matmul(u, w_ukv, x_cols=(A_KV_LORA, U_CKV), prologue="rmsnorm", norm_g=a_kv_norm[l],
                           out_dtype=BF16, name="mla_kv_up")
        ur = qk_prep(u, 0, U_PREP_N, U_PREP_R32, U_PREP_RMSQ, tables, d_q_norm[l], d_k_norm[l], t_lat, seq, "qk_prep")
        qr = qk_prep(qa, 1, A_HEADS, A_HEADS, 0, tables, d_q_norm[l], d_k_norm[l], t_lat, seq, "mla_q_rope")
        mixers = {
            "mla": dict(q_srcs=[(qa, 0), (qr, 0)], k_srcs=[(kva, 0, 1), (ur, R_KR, 0)], v_src=(kva, A_HEADS, 1),
                        heads=A_HEADS, scale=A_SCALE),
            "swa": dict(q_srcs=[(ur, R_BQ)], k_srcs=[(ur, R_BK, 0)], v_src=(u, U_BV, 0), heads=B_HEADS,
                        scale=B_SCALE, sink=b_sink[l], heads_per_step=B_HEADS),
            "na": dict(q_srcs=[(u, U_CQ_)], k_srcs=[(u, U_CK, 1)], v_src=(u, U_CV, 1), heads=C_HEADS,
                       scale=C_SCALE, heads_per_step=C_HEADS),
            "gqa": dict(q_srcs=[(ur, R_DQ)], k_srcs=[(ur, R_DK, D_HEADS // D_KV_HEADS)],
                        v_src=(u, U_DV, D_HEADS // D_KV_HEADS), heads=D_HEADS, scale=D_SCALE),
        }
        windows = {"swa": dict(window=(B_TQ, B_BACK, B_WK), bias=band_bias),
                   "na": dict(window=(C_TQ, C_BACK, C_WK), bias=neighbourhood_bias_table(c_rpb[l], rows_grid))}
        branches = [attention(mode="lat", name=mixer + "_attention", **cfg, **windows.get(mixer, {}), **dims)
                    for mixer, cfg in mixers.items()]
        branches_ctx = [attention(mode="ctx", name=mixer + "_attention_ctx", **cfg, **dims)
                        for mixer, cfg in mixers.items()] if need_ctx else None

        xs = merge_out_ln(branches, branches_ctx, _branch_weights(w_branch[l]), gates, w_out[l].astype(BF16), xs, mod,
                          2, group_of, ln1_g[l], ln1_b[l], alpha)

        pq = token_matmul(xs, peer_w_q[l].astype(BF16), prologue="modulate", mod=mod, mod_idx=(3, 4),
                          group_of=group_of, out_dtype=BF16, name="peer_query")
        keys = peer_sub_keys[l].reshape(2 * PEER_HEADS, PEER_KEYS, PEER_QDIM // 2).astype(BF16)
        cnt, e0, rank, e1 = peer_topk(pq, keys)
        ht = modulate_transposed(xs, mod, (3, 4), group_of)
        n_exp = peer_u.shape[1]
        vt = jnp.transpose(peer_v[l].astype(BF16).reshape(n_exp // PEER_EB, PEER_EB, d), (0, 2, 1))
        ffn_t = peer_dense(ht, peer_u[l].astype(BF16), vt, cnt, e0, rank, e1)
        xs = residual_ln(xs, ffn_t, mod, 5, group_of, ln2_g[l], ln2_b[l], alpha)

    return xs[:t_lat].reshape(bsz, seq, d)
```

```python
import functools
import math

import jax
import jax.numpy as jnp
import numpy as np
from jax import lax
from jax.experimental import pallas as pl
from jax.experimental.pallas import tpu as pltpu

F32 = jnp.float32
BF16 = jnp.bfloat16

GRID_W = 64
ROPE_THETA = 10000.0
LN_EPS = 1e-5
RMS_EPS = 1e-6
NEG_INF = -1e30

A_HEADS, A_NOPE, A_ROPE, A_V, A_Q_LORA, A_KV_LORA = 4, 128, 64, 128, 384, 128
B_HEADS, B_KV_HEADS, B_DIM, B_WINDOW = 8, 2, 64, 128
C_HEADS, C_DIM, NA_ROWS, NA_COLS = 4, 128, 8, 16
D_HEADS, D_KV_HEADS, D_DIM = 4, 2, 128
N_BRANCH, BRANCH_W = 4, 512
PEER_HEADS, PEER_KEYS, PEER_QDIM, PEER_TOPK = 8, 128, 256, 16

A_SCALE = (A_NOPE + A_ROPE) ** -0.5
B_SCALE = B_DIM ** -0.5
C_SCALE = C_DIM ** -0.5
D_SCALE = D_DIM ** -0.5

IN_SIZES = (A_Q_LORA, A_KV_LORA, A_ROPE,
            B_HEADS * B_DIM, B_KV_HEADS * B_DIM, B_KV_HEADS * B_DIM,
            C_HEADS * C_DIM, C_HEADS * C_DIM, C_HEADS * C_DIM,
            D_HEADS * D_DIM, D_KV_HEADS * D_DIM, D_KV_HEADS * D_DIM)
LANE = 128
BF16_ROWS = 16

VMEM_LIMIT = 56 * 1024 * 1024

PEER_CHUNK = 512
PEER_EB = 1024
PEER_TOPK_TILE = 256

B_TQ, B_BACK, B_WK = 512, 128, 768
C_TQ, C_BACK, C_WK = 4 * GRID_W, 4 * GRID_W, 12 * GRID_W
DENSE_TQ = 1024
DENSE_SUB = 4
ROPE_TM = 512

R_BQ, R_KR, R_BK, R_DQ, R_DK = 0, 8, 9, 10, 14
U_PREP_N, U_PREP_R32, U_PREP_RMSQ = 16, 10, 4
U_CQ_, U_CK, U_CV, U_CKV, U_BV, U_CQ, U_DV, U_BLOCKS = 16, 20, 24, 28, 29, 30, 33, 36


def _cparams(sem):
    return pltpu.CompilerParams(dimension_semantics=sem, vmem_limit_bytes=VMEM_LIMIT)


def _pick(n, cands):
    for c in cands:
        if n % c == 0:
            return c
    raise ValueError(f"no tile in {cands} divides {n}")


def _dot(a, b):
    return jnp.dot(a, b, preferred_element_type=F32)


def _dot_nt(a, b):
    return lax.dot_general(a, b, (((1,), (1,)), ((), ())), preferred_element_type=F32)


def _ada_kernel(c_ref, w_ref, b_ref, o_ref):
    c = c_ref[...]
    s = c * (1.0 / (1.0 + jnp.exp(-c)))
    w = w_ref[0]
    s_hi = s.astype(BF16)
    s_lo = (s - s_hi.astype(F32)).astype(BF16)
    w_hi = w.astype(BF16)
    w_lo = (w - w_hi.astype(F32)).astype(BF16)
    acc = _dot(s_hi, w_hi) + _dot(s_lo, w_hi) + _dot(s_hi, w_lo)
    o_ref[...] = acc + b_ref[...]


def ada_modulation(cc, w_all, b, layer):
    r, k = cc.shape
    n = w_all.shape[2]
    tn = _pick(n, (512, 256, 128))
    return pl.pallas_call(
        _ada_kernel,
        grid=(n // tn,),
        in_specs=[pl.BlockSpec((r, k), lambda j: (0, 0)),
                  pl.BlockSpec((1, k, tn), lambda j: (layer, 0, j)),
                  pl.BlockSpec((1, tn), lambda j: (0, j))],
        out_specs=pl.BlockSpec((r, tn), lambda j: (0, j)),
        out_shape=jax.ShapeDtypeStruct((r, n), F32),
        compiler_params=_cparams(("arbitrary",)),
        name="ada_modulation",
    )(cc, w_all, b.reshape(1, n))


def _mm_kernel(*refs, prologue, has_bias, act):
    it = iter(refs)
    x_ref = next(it)
    if prologue == "modulate":
        sh_ref, sc_ref = next(it), next(it)
    elif prologue == "rmsnorm":
        g_ref = next(it)
    w_ref = next(it)
    b_ref = next(it) if has_bias else None
    o_ref = next(it)
    a_scr = next(it)

    @pl.when(pl.program_id(1) == 0)
    def _():
        x = x_ref[...].astype(F32)
        if prologue == "modulate":
            a = x * (1.0 + sc_ref[0]) + sh_ref[0]
        elif prologue == "rmsnorm":
            ms = jnp.mean(x * x, axis=-1, keepdims=True)
            a = (x * lax.rsqrt(ms + RMS_EPS)) * g_ref[...]
        else:
            a = x
        a_scr[...] = a.astype(BF16)

    acc = _dot(a_scr[...], w_ref[...])
    if has_bias:
        acc = acc + b_ref[...]
    if act == "sigmoid":
        acc = 1.0 / (1.0 + jnp.exp(-acc))
    o_ref[...] = acc.astype(o_ref.dtype)


def token_matmul(x, w, *, rows=None, x_cols=None, prologue=None, mod=None, mod_idx=None, group_of=None,
                 norm_g=None, bias=None, act=None, out_dtype=F32, name="token_matmul"):
    rows = x.shape[0] if rows is None else rows
    kdim, n = w.shape
    kw, kblk = (x.shape[1], 0) if x_cols is None else x_cols
    assert kw == kdim
    tm = _pick(rows, (1024, 512, 256))
    tn = _pick(n, (2048, 1536, 1024, 512, 256, 128))
    in_specs = [pl.BlockSpec((tm, kdim), lambda i, j: (i, kblk))]
    args = [x]
    if prologue == "modulate":
        tm_group = group_of(tm)
        for slot in mod_idx:
            in_specs.append(pl.BlockSpec((1, 1, kdim), lambda i, j, s=slot: (tm_group(i) * 6 + s, 0, 0)))
            args.append(mod)
    elif prologue == "rmsnorm":
        in_specs.append(pl.BlockSpec((1, kdim), lambda i, j: (0, 0)))
        args.append(norm_g.reshape(1, kdim))
    in_specs.append(pl.BlockSpec((kdim, tn), lambda i, j: (0, j)))
    args.append(w)
    if bias is not None:
        in_specs.append(pl.BlockSpec((1, tn), lambda i, j: (0, j)))
        args.append(bias.reshape(1, n))
    return pl.pallas_call(
        functools.partial(_mm_kernel, prologue=prologue, has_bias=bias is not None, act=act),
        grid=(rows // tm, n // tn),
        in_specs=in_specs,
        out_specs=pl.BlockSpec((tm, tn), lambda i, j: (i, j)),
        out_shape=jax.ShapeDtypeStruct((rows, n), out_dtype),
        scratch_shapes=[pltpu.VMEM((tm, kdim), BF16)],
        compiler_params=_cparams(("parallel", "arbitrary")),
        name=name,
    )(*args)


def _res_ln_kernel(x_ref, yt_ref, gate_ref, g_ref, b_ref, o_ref, *, alpha):
    z = alpha * x_ref[...] + gate_ref[0] * yt_ref[0].T
    mu = jnp.mean(z, axis=-1, keepdims=True)
    zc = z - mu
    var = jnp.mean(zc * zc, axis=-1, keepdims=True)
    o_ref[...] = (zc * lax.rsqrt(var + LN_EPS)) * g_ref[...] + b_ref[...]


def residual_ln(x, yt, mod, gate_slot, group_of, ln_g, ln_b, alpha):
    nch, d, tm = yt.shape
    rows = nch * tm
    tm_group = group_of(tm)
    return pl.pallas_call(
        functools.partial(_res_ln_kernel, alpha=alpha),
        grid=(nch,),
        in_specs=[pl.BlockSpec((tm, d), lambda i: (i, 0)),
                  pl.BlockSpec((1, d, tm), lambda i: (i, 0, 0)),
                  pl.BlockSpec((1, 1, d), lambda i: (tm_group(i) * 6 + gate_slot, 0, 0)),
                  pl.BlockSpec((1, d), lambda i: (0, 0)),
                  pl.BlockSpec((1, d), lambda i: (0, 0))],
        out_specs=pl.BlockSpec((tm, d), lambda i: (i, 0)),
        out_shape=jax.ShapeDtypeStruct((rows, d), F32),
        compiler_params=_cparams(("parallel",)),
        name="residual_ln",
    )(x, yt, mod, ln_g.reshape(1, d), ln_b.reshape(1, d))


def _qk_prep_kernel(x_ref, c32_ref, sa_ref, sb_ref, c64_ref, s64_ref, gq_ref, gk_ref, o_ref, *, n_r32, n_rms_q):
    for j in range(x_ref.shape[1] // LANE):
        cols = slice(j * LANE, (j + 1) * LANE)
        x = x_ref[:, cols].astype(F32)
        if j < n_r32:
            y = x * c32_ref[...] + pltpu.roll(x, 32, 1) * sa_ref[...] + pltpu.roll(x, 96, 1) * sb_ref[...]
        else:
            g = gq_ref[...] if j < n_r32 + n_rms_q else gk_ref[...]
            ms = jnp.mean(x * x, axis=-1, keepdims=True)
            y = (x * lax.rsqrt(ms + RMS_EPS)) * g
            y = y * c64_ref[...] + pltpu.roll(y, 64, 1) * s64_ref[...]
        o_ref[:, cols] = y.astype(o_ref.dtype)


def qk_prep(x, group_idx, n_blocks, n_r32, n_rms_q, tables, gq, gk, lat_rows, seq, name):
    rows = x.shape[0]
    tm = ROPE_TM
    width = n_blocks * LANE
    lat_blocks, per_seq = lat_rows // tm, seq // tm
    tab_spec = pl.BlockSpec((tm, LANE), lambda i: (jnp.where(i < lat_blocks, i % per_seq, per_seq), 0))
    vec_spec = pl.BlockSpec((1, LANE), lambda i: (0, 0))
    return pl.pallas_call(
        functools.partial(_qk_prep_kernel, n_r32=n_r32, n_rms_q=n_rms_q),
        grid=(rows // tm,),
        in_specs=[pl.BlockSpec((tm, width), lambda i: (i, group_idx))] + [tab_spec] * 5 + [vec_spec] * 2,
        out_specs=pl.BlockSpec((tm, width), lambda i: (i, 0)),
        out_shape=jax.ShapeDtypeStruct((rows, width), BF16),
        compiler_params=_cparams(("parallel",)),
        name=name,
    )(x, *tables, gq.reshape(1, LANE), gk.reshape(1, LANE))


def _attn_kernel(*refs, n_parts, has_lat, window, has_bias, bias_per_head, has_sink, scale, n_lat, n_sub, hps,
                 per_head):
    it = iter(refs)
    q_refs = [next(it) for _ in range(n_parts)]
    if has_lat:
        kl_refs = [next(it) for _ in range(n_parts)]
        vl_ref = next(it)
    kc_refs = [next(it) for _ in range(n_parts)]
    vc_ref = next(it)
    bias_ref = next(it) if has_bias else None
    sink_ref = next(it) if has_sink else None
    refs = list(it)
    o_ref = refs[-1]

    if has_lat and window is not None:
        tq, back, wk = window
        start = pl.multiple_of(jnp.clip(pl.program_id(2) * tq - back, 0, n_lat - wk), LANE)
        key_rows = pl.ds(start, wk)
    else:
        key_rows = slice(None)
    sub = o_ref.shape[0] // n_sub
    for hh in range(hps):
        lanes = slice(hh * LANE, (hh + 1) * LANE)

        def cols(j):
            return lanes if per_head[j] else slice(None)

        kcs = [jnp.concatenate([k[:, cols(j)] for j, k in enumerate(kc_refs)], axis=1)]
        vc = vc_ref[:, cols(n_parts)]
        if has_lat:
            kls = [jnp.concatenate([k[key_rows, cols(j)] for j, k in enumerate(kl_refs)], axis=1)]
            vl = vl_ref[key_rows, cols(n_parts)]
        if has_sink:
            sink = sink_ref[pl.program_id(1) * hps + hh]
        for sb in range(n_sub):
            rows = slice(sb * sub, (sb + 1) * sub)
            qs = [jnp.concatenate([q[rows, lanes] for q in q_refs], axis=1)]
            fold = not has_bias and not has_sink
            post = (lambda t: t) if fold else (lambda t: t * scale)
            expo = (lambda t: jnp.exp2(t * (scale * math.log2(math.e)))) if fold else jnp.exp
            s2 = post(functools.reduce(jnp.add, [_dot_nt(q, k) for q, k in zip(qs, kcs)]))
            m = jnp.max(s2, axis=-1, keepdims=True)
            if has_lat:
                s1 = post(functools.reduce(jnp.add, [_dot_nt(q, k) for q, k in zip(qs, kls)]))
                if has_bias:
                    s1 = s1 + bias_ref[0, hh if bias_per_head else 0, rows, :]
                m = jnp.maximum(m, jnp.max(s1, axis=-1, keepdims=True))
            if has_sink:
                m = jnp.maximum(m, sink)
            p2 = expo(s2 - m)
            l = jnp.sum(p2, axis=-1, keepdims=True)
            o = _dot(p2.astype(BF16), vc)
            if has_lat:
                p1 = expo(s1 - m)
                l = l + jnp.sum(p1, axis=-1, keepdims=True)
                o = o + _dot(p1.astype(BF16), vl)
            if has_sink:
                l = l + jnp.exp(sink - m)
            o_ref[rows, lanes] = (o / l).astype(o_ref.dtype)


def attention(q_srcs, k_srcs, v_src, *, heads, scale, bsz, seq, nctx, t_lat, mode, window=None, bias=None, sink=None,
              heads_per_step=1, name="attention"):
    lat = mode == "lat"
    hps = heads_per_step
    n_parts = len(q_srcs)
    tq = (window[0] if window is not None else DENSE_TQ) if lat else nctx
    nqb = (seq if lat else nctx) // tq
    ctx0 = t_lat // nctx
    kv_srcs = list(k_srcs) + [v_src]
    per_head = tuple(grp == 1 for _, _, grp in kv_srcs)
    assert heads % hps == 0 and all(first % hps == 0 for _, first in q_srcs)
    assert hps == 1 or all(grp == 0 or (grp == 1 and first % hps == 0) for _, first, grp in kv_srcs)

    def kv_spec(rows, row_blk, first, grp):
        if hps > 1 and grp == 1:
            return pl.BlockSpec((rows, hps * LANE), lambda bi, hg, i: (row_blk(bi), first // hps + hg))
        return pl.BlockSpec((rows, LANE), lambda bi, hg, i: (row_blk(bi), first + (hg // grp if grp else 0)))

    def q_row(bi, i):
        return bi * nqb + i if lat else t_lat // tq + bi

    in_specs, args = [], []
    for arr, first in q_srcs:
        in_specs.append(pl.BlockSpec((tq, hps * LANE), lambda bi, hg, i, f=first: (q_row(bi, i), f // hps + hg)))
        args.append(arr)
    if lat:
        for arr, first, grp in kv_srcs:
            in_specs.append(kv_spec(seq, lambda bi: bi, first, grp))
            args.append(arr)
    for arr, first, grp in kv_srcs:
        in_specs.append(kv_spec(nctx, lambda bi: ctx0 + bi, first, grp))
        args.append(arr)
    bias_per_head = bias is not None and bias.shape[1] != 1
    if bias is not None:
        in_specs.append(pl.BlockSpec(
            (1, hps if bias_per_head else 1, tq, window[2]),
            lambda bi, hg, i: (jnp.where(i == 0, 0, jnp.where(i == nqb - 1, 2, 1)), hg if bias_per_head else 0, 0, 0)))
        args.append(bias)
    if sink is not None:
        in_specs.append(pl.BlockSpec(memory_space=pltpu.SMEM))
        args.append(sink.astype(F32))
    if window is not None:
        assert seq % tq == 0 and nqb >= 2 and seq >= window[2]
    return pl.pallas_call(
        functools.partial(_attn_kernel, n_parts=n_parts, has_lat=lat, window=window, has_bias=bias is not None,
                          bias_per_head=bias_per_head, has_sink=sink is not None, scale=scale, n_lat=seq,
                          n_sub=DENSE_SUB if (lat and window is None) else 1, hps=hps, per_head=per_head),
        grid=(bsz, heads // hps, nqb),
        in_specs=in_specs,
        out_specs=pl.BlockSpec((tq, hps * LANE), lambda bi, hg, i: (bi * nqb + i, hg)),
        out_shape=jax.ShapeDtypeStruct((bsz * nqb * tq, heads * LANE), BF16),
        compiler_params=_cparams(("parallel", "parallel", "arbitrary")),
        name=name,
    )(*args)


def _window_q_offsets(geom):
    tq, back, wk = geom
    return (0, back, wk - tq)


def band_bias_table():
    tq, _, wk = (B_TQ, B_BACK, B_WK)
    kpos = np.arange(wk)[None, :]
    tabs = []
    for off in _window_q_offsets((B_TQ, B_BACK, B_WK)):
        qpos = off + np.arange(tq)[:, None]
        tabs.append(np.where(np.abs(qpos - kpos) <= B_WINDOW, 0.0, NEG_INF))
    return jnp.asarray(np.stack(tabs)[:, None], F32)


def neighbourhood_bias_table(rpb, rows):
    tq, back, wk = (C_TQ, C_BACK, C_WK)
    assert rows >= wk // GRID_W and rows >= NA_ROWS
    nqb = rows * GRID_W // tq
    nq_rows, nk_rows = tq // GRID_W, wk // GRID_W
    cols = np.arange(GRID_W)
    col_start = np.clip(cols - NA_COLS // 2, 0, GRID_W - NA_COLS)
    col_ok = (cols[None, :] >= col_start[:, None]) & (cols[None, :] < col_start[:, None] + NA_COLS)
    col_off = np.clip(cols[None, :] - cols[:, None] + (NA_COLS - 1), 0, 2 * NA_COLS - 2)
    onehot_c = jnp.asarray(col_off[..., None] == np.arange(2 * NA_COLS - 1), F32)
    tabs = []
    for cls, off in enumerate(_window_q_offsets((C_TQ, C_BACK, C_WK))):
        blk = (0, 1, nqb - 1)[cls]
        w0 = int(np.clip(blk * tq - back, 0, rows * GRID_W - wk)) // GRID_W
        assert w0 * GRID_W + off == blk * tq and off % GRID_W == 0
        q_row = w0 + off // GRID_W + np.arange(nq_rows)
        k_row = w0 + np.arange(nk_rows)
        row_start = np.clip(q_row - NA_ROWS // 2, 0, rows - NA_ROWS)
        row_ok = (k_row[None, :] >= row_start[:, None]) & (k_row[None, :] < row_start[:, None] + NA_ROWS)
        row_off = np.clip(k_row[None, :] - q_row[:, None] + (NA_ROWS - 1), 0, 2 * NA_ROWS - 2)
        by_row = rpb[:, row_off, :].astype(F32)
        vals = jnp.einsum("hrsc,qkc->hrqsk", by_row, onehot_c, precision=lax.Precision.HIGHEST)
        ok = row_ok[:, None, :, None] & col_ok[None, :, None, :]
        tabs.append(jnp.where(jnp.asarray(ok)[None], vals, NEG_INF).reshape(rpb.shape[0], tq, wk))
    return jnp.stack(tabs)


def _merge_out_ln_kernel(*refs, alpha, n_branch, lat_blocks):
    o_refs = refs[:n_branch]
    refs = refs[n_branch:]
    if lat_blocks is not None:
        oc_refs = refs[:n_branch]
        refs = refs[n_branch:]
        is_ctx = pl.program_id(0) >= lat_blocks
    wbr_refs = refs[:n_branch]
    g_ref, wout_ref, x_ref, gate_ref, lng_ref, lnb_ref, shf_ref, scf_ref, out_ref, ht_ref = refs[n_branch:]
    d = x_ref.shape[1]
    y = None
    for r, (o_ref, wbr_ref) in enumerate(zip(o_refs, wbr_refs)):
        o = o_ref[...]
        if lat_blocks is not None:
            o = jnp.where(is_ctx, oc_refs[r][...], o)
        t = g_ref[:, r * d:(r + 1) * d].astype(F32) * _dot(o, wbr_ref[...])
        y = t if y is None else y + t
    mix = _dot(y.astype(BF16), wout_ref[...])
    z = alpha * x_ref[...] + gate_ref[0] * mix
    mu = jnp.mean(z, axis=-1, keepdims=True)
    zc = z - mu
    var = jnp.mean(zc * zc, axis=-1, keepdims=True)
    out = (zc * lax.rsqrt(var + LN_EPS)) * lng_ref[...] + lnb_ref[...]
    out_ref[...] = out
    ht_ref[0] = (out * (1.0 + scf_ref[0]) + shf_ref[0]).T.astype(BF16)


def merge_out_ln(branches, branches_ctx, w_branch, gates, w_out, x, mod, gate_slot, ffn_mod_idx, group_of, ln_g, ln_b,
                 alpha):
    rows = gates.shape[0]
    nbr = len(branches)
    d = w_out.shape[0]
    tm = _pick(rows, (256,))
    per_chunk = PEER_CHUNK // tm
    tm_group = group_of(tm)
    resident = pl.Buffered(1)
    lat_blocks = branches[0].shape[0] // tm
    last = lat_blocks - 1
    branch_specs = [pl.BlockSpec((tm, o.shape[1]), lambda i: (jnp.minimum(i, last), 0)) for o in branches]
    if branches_ctx is not None:
        assert rows == branches[0].shape[0] + branches_ctx[0].shape[0]
        branch_specs += [pl.BlockSpec((tm, o.shape[1]), lambda i: (jnp.maximum(i - lat_blocks, 0), 0))
                         for o in branches_ctx]
    else:
        assert rows == branches[0].shape[0]
    return pl.pallas_call(
        functools.partial(_merge_out_ln_kernel, alpha=alpha, n_branch=nbr,
                          lat_blocks=None if branches_ctx is None else lat_blocks),
        grid=(rows // tm,),
        in_specs=branch_specs + [
                  pl.BlockSpec(w.shape, lambda i: (0, 0), pipeline_mode=resident) for w in w_branch] + [
                  pl.BlockSpec((tm, nbr * d), lambda i: (i, 0)),
                  pl.BlockSpec((d, d), lambda i: (0, 0), pipeline_mode=resident),
                  pl.BlockSpec((tm, d), lambda i: (i, 0)),
                  pl.BlockSpec((1, 1, d), lambda i: (tm_group(i) * 6 + gate_slot, 0, 0)),
                  pl.BlockSpec((1, d), lambda i: (0, 0)),
                  pl.BlockSpec((1, d), lambda i: (0, 0))] + [
                  pl.BlockSpec((1, 1, d), lambda i, s=slot: (tm_group(i) * 6 + s, 0, 0)) for slot in ffn_mod_idx],
        out_specs=[pl.BlockSpec((tm, d), lambda i: (i, 0)),
                   pl.BlockSpec((1, d, tm), lambda i: (i // per_chunk, 0, i % per_chunk))],
        out_shape=[jax.ShapeDtypeStruct((rows, d), F32),
                   jax.ShapeDtypeStruct((rows // PEER_CHUNK, d, PEER_CHUNK), BF16)],
        compiler_params=_cparams(("parallel",)),
        name="merge_out_ln",
    )(*branches, *(branches_ctx or []), *w_branch, gates, w_out, x, mod, ln_g.reshape(1, d), ln_b.reshape(1, d),
      mod, mod)


def _peer_pairs():
    return [(a, b) for a in range(PEER_TOPK) for b in range(PEER_TOPK) if (a + 1) * (b + 1) <= PEER_TOPK]


def _peer_topk_kernel(q_ref, keys_ref, cnt_ref, e0_ref, rank_ref, e1_ref, s0_scr):
    tops = [[[None] * PEER_HEADS for _ in range(PEER_TOPK)] for _ in range(2)]
    for h in range(PEER_HEADS):
        for p in range(2):
            hp = 2 * h + p
            qs = q_ref[:, hp * PEER_KEYS:(hp + 1) * PEER_KEYS]
            s = _dot_nt(keys_ref[hp], qs)
            work = s
            rank = jnp.full(s.shape, float(PEER_TOPK), F32)
            for r in range(PEER_TOPK):
                m = jnp.max(work, axis=0, keepdims=True)
                tops[p][r][h] = m
                if p == 1:
                    hit = work == m
                    rank = jnp.where(hit, float(r), rank)
                    work = jnp.where(hit, -jnp.inf, work)
                elif r + 1 < PEER_TOPK:
                    work = jnp.where(work == m, -jnp.inf, work)
            if p == 0:
                s0_scr[h] = s
            else:
                rank_ref[h, 0] = rank.astype(BF16)
                e1_ref[h, 0] = jnp.exp(s - tops[1][0][h]).astype(BF16)
    top = [[jnp.concatenate(tops[p][r], axis=0) for r in range(PEER_TOPK)] for p in range(2)]
    cands = [top[0][a] + top[1][b] for a, b in _peer_pairs()]
    work = list(cands)
    for r in range(PEER_TOPK):
        m = functools.reduce(jnp.maximum, work)
        if r + 1 < PEER_TOPK:
            work = [jnp.where(w == m, -jnp.inf, w) for w in work]
    tau = m
    cmax = cands[0]
    z = functools.reduce(jnp.add, [jnp.where(c >= tau, jnp.exp(c - cmax), 0.0) for c in cands])
    rz = 1.0 / z
    b_dense = 4
    a_sparse = PEER_TOPK // (b_dense + 1)
    cnt_hi = [functools.reduce(jnp.add, [jnp.where(top[0][a] + top[1][b] >= tau, 1.0, 0.0)
                                          for b in range(b_dense, PEER_TOPK)]) for a in range(a_sparse)]
    for h in range(PEER_HEADS):
        s0 = s0_scr[h]
        tau_h = tau[h:h + 1]
        cnt = None
        for b in range(b_dense):
            inc = jnp.where(s0 + top[1][b][h:h + 1] >= tau_h, 1.0, 0.0)
            cnt = inc if cnt is None else cnt + inc
        for a in range(a_sparse):
            cnt = cnt + jnp.where(s0 == top[0][a][h:h + 1], cnt_hi[a][h:h + 1], 0.0)
        cnt_ref[h, 0] = cnt
        e0_ref[h, 0] = jnp.exp(s0 - top[0][0][h:h + 1]) * rz[h:h + 1]


def peer_topk(q, keys):
    rows = q.shape[0]
    tt = PEER_TOPK_TILE
    per_chunk = PEER_CHUNK // tt
    nch = rows // PEER_CHUNK
    big = pl.BlockSpec((PEER_HEADS, 1, PEER_KEYS, tt), lambda i: (0, i // per_chunk, 0, i % per_chunk))
    shape_f32 = jax.ShapeDtypeStruct((PEER_HEADS, nch, PEER_KEYS, PEER_CHUNK), F32)
    shape_bf16 = jax.ShapeDtypeStruct((PEER_HEADS, nch, PEER_KEYS, PEER_CHUNK), BF16)
    return pl.pallas_call(
        _peer_topk_kernel,
        grid=(rows // tt,),
        in_specs=[pl.BlockSpec((tt, q.shape[1]), lambda i: (i, 0)),
                  pl.BlockSpec(keys.shape, lambda i: (0, 0, 0))],
        out_specs=[big, big, big, big],
        out_shape=[shape_f32, shape_f32, shape_bf16, shape_bf16],
        scratch_shapes=[pltpu.VMEM((PEER_HEADS, PEER_KEYS, tt), F32)],
        compiler_params=_cparams(("parallel",)),
        name="peer_topk",
    )(q, keys)


def _modulate_t_kernel(x_ref, sh_ref, sc_ref, o_ref):
    h = x_ref[...] * (1.0 + sc_ref[0]) + sh_ref[0]
    o_ref[0] = h.T.astype(BF16)


def modulate_transposed(x, mod, mod_idx, group_of):
    rows, d = x.shape
    tt = PEER_CHUNK
    tt_group = group_of(tt)
    mod_specs = [pl.BlockSpec((1, 1, d), lambda i, s=slot: (tt_group(i) * 6 + s, 0, 0)) for slot in mod_idx]
    return pl.pallas_call(
        _modulate_t_kernel,
        grid=(rows // tt,),
        in_specs=[pl.BlockSpec((tt, d), lambda i: (i, 0))] + mod_specs,
        out_specs=pl.BlockSpec((1, d, tt), lambda i: (i, 0, 0)),
        out_shape=jax.ShapeDtypeStruct((rows // tt, d, tt), BF16),
        compiler_params=_cparams(("parallel",)),
        name="modulate_transposed",
    )(x, mod, mod)


def _peer_dense_kernel(ht_ref, u_ref, vt_ref, cnt_ref, e0_ref, rank_ref, e1_ref, o_ref, a_scr, w_scr, p_scr, *, n_e,
                       n_pairs):
    s = pl.program_id(0)
    e = s % n_e
    cur = s % 2
    prev = 1 - cur

    @pl.when(jnp.logical_and(s > 0, (s - 1) % n_e == 0))
    def _():
        o_ref[...] = jnp.zeros_like(o_ref)
    i_per_eb = PEER_EB // PEER_KEYS
    tc = w_scr.shape[-1]

    def gate_and_down():
        for ii in range(i_per_eb):
            sl = slice(ii * PEER_KEYS, (ii + 1) * PEER_KEYS)
            a = a_scr[prev, sl, :]
            act = 0.5 * a * (1.0 + lax.erf(a * (1.0 / math.sqrt(2.0))))
            p_scr[sl, :] = act.astype(BF16) * w_scr[prev, sl, :]
        o_ref[0] += _dot(vt_ref[0], p_scr[...])

    def up_and_weights():
        a_scr[cur] = _dot(u_ref[...], ht_ref[0])
        for ii in range(i_per_eb):
            i = e * i_per_eb + ii
            cnt = [jnp.broadcast_to(cnt_ref[h, 0, pl.ds(i, 1), :], (BF16_ROWS, tc)).astype(BF16)
                   for h in range(PEER_HEADS)]
            e0 = [jnp.broadcast_to(e0_ref[h, 0, pl.ds(i, 1), :], (BF16_ROWS, tc)).astype(BF16)
                  for h in range(PEER_HEADS)]
            for g in range(PEER_KEYS // BF16_ROWS):
                rows = slice(g * BF16_ROWS, (g + 1) * BF16_ROWS)
                w = None
                for h in range(PEER_HEADS):
                    e1 = e1_ref[h, 0, rows, :]
                    t = jnp.where(rank_ref[h, 0, rows, :] < cnt[h], e1, jnp.zeros_like(e1)) * e0[h]
                    w = t if w is None else w + t
                w_scr[cur, ii * PEER_KEYS + g * BF16_ROWS:ii * PEER_KEYS + (g + 1) * BF16_ROWS, :] = w

    @pl.when(s == 0)
    def _():
        up_and_weights()

    @pl.when(jnp.logical_and(s > 0, s < n_pairs))
    def _():
        gate_and_down()
        up_and_weights()

    @pl.when(s == n_pairs)
    def _():
        gate_and_down()


def peer_dense(ht, u, vt, cnt, e0, rank, e1):
    nch, d, tc = ht.shape
    n_e = u.shape[0] // PEER_EB
    n_pairs = nch * n_e

    def cur_chunk(s):
        return jnp.minimum(s, n_pairs - 1) // n_e

    def prev_chunk(s):
        return jnp.maximum(s - 1, 0) // n_e

    sel = pl.BlockSpec((PEER_HEADS, 1, PEER_KEYS, tc), lambda s: (0, cur_chunk(s), 0, 0))
    return pl.pallas_call(
        functools.partial(_peer_dense_kernel, n_e=n_e, n_pairs=n_pairs),
        grid=(n_pairs + 1,),
        in_specs=[pl.BlockSpec((1, d, tc), lambda s: (cur_chunk(s), 0, 0)),
                  pl.BlockSpec((PEER_EB, d), lambda s: (jnp.minimum(s, n_pairs - 1) % n_e, 0)),
                  pl.BlockSpec((1, d, PEER_EB), lambda s: (jnp.maximum(s - 1, 0) % n_e, 0, 0)),
                  sel, sel, sel, sel],
        out_specs=pl.BlockSpec((1, d, tc), lambda s: (prev_chunk(s), 0, 0)),
        out_shape=jax.ShapeDtypeStruct((nch, d, tc), F32),
        scratch_shapes=[pltpu.VMEM((2, PEER_EB, tc), F32), pltpu.VMEM((2, PEER_EB, tc), BF16),
                        pltpu.VMEM((PEER_EB, tc), BF16)],
        compiler_params=_cparams(("arbitrary",)),
        name="peer_dense",
    )(ht, u, vt, cnt, e0, rank, e1)


def _axial_angles(n, rot_dim):
    t = jnp.arange(n)
    rows = (t // GRID_W).astype(F32)
    cols = (t % GRID_W).astype(F32)
    nf = rot_dim // 4
    inv = ROPE_THETA ** (-jnp.arange(nf, dtype=F32) / nf)
    ang = jnp.concatenate([rows[:, None] * inv, cols[:, None] * inv], axis=-1)
    return jnp.cos(ang), jnp.sin(ang)


def _rope_tables(seq):
    assert A_ROPE == B_DIM == LANE // 2 and D_DIM == LANE

    def pad(t, fill):
        return jnp.concatenate([t, jnp.full((ROPE_TM, LANE), fill, F32)], axis=0)

    cos, sin = _axial_angles(seq, B_DIM)
    z = jnp.zeros_like(sin)
    c32 = jnp.concatenate([cos, cos, cos, cos], axis=1)
    sa = jnp.concatenate([z, sin, z, sin], axis=1)
    sb = jnp.concatenate([-sin, z, -sin, z], axis=1)
    cos, sin = _axial_angles(seq, D_DIM)
    c64 = jnp.concatenate([cos, cos], axis=1)
    s64 = jnp.concatenate([-sin, sin], axis=1)
    return [pad(c32, 1.0), pad(sa, 0.0), pad(sb, 0.0), pad(c64, 1.0), pad(s64, 0.0)]


def _half_block(t, first_half, axis):
    z = jnp.zeros_like(t)
    return jnp.concatenate([t, z] if first_half else [z, t], axis=axis)


def _in_proj_weight(w_in):
    d = w_in.shape[0]
    o = np.cumsum((0,) + IN_SIZES)
    part = [w_in[:, o[i]:o[i + 1]] for i in range(len(IN_SIZES))]
    per_kv = B_HEADS // B_KV_HEADS
    bq = [_half_block(part[3][:, h * B_DIM:(h + 1) * B_DIM], h // per_kv == 0, 1) for h in range(B_HEADS)]
    cols = bq + [_half_block(part[2], True, 1), part[4], part[9], part[10], part[6], part[7], part[8], part[1],
                 part[5], part[0], part[11], jnp.zeros((d, LANE), w_in.dtype)]
    out = jnp.concatenate(cols, axis=1).astype(BF16)
    assert out.shape[1] == U_BLOCKS * LANE
    return out


def _mla_up_weights(w_uq, w_ukv):
    dq, dkv = A_NOPE + A_ROPE, A_NOPE + A_V
    q_nope = [w_uq[:, h * dq:h * dq + A_NOPE] for h in range(A_HEADS)]
    q_rope = [_half_block(w_uq[:, h * dq + A_NOPE:(h + 1) * dq], True, 1) for h in range(A_HEADS)]
    k_nope = [w_ukv[:, h * dkv:h * dkv + A_NOPE] for h in range(A_HEADS)]
    v = [w_ukv[:, h * dkv + A_NOPE:(h + 1) * dkv] for h in range(A_HEADS)]
    return (jnp.concatenate(q_nope + q_rope, axis=1).astype(BF16), jnp.concatenate(k_nope + v, axis=1).astype(BF16))


def _branch_weights(w_branch):
    per_kv = B_HEADS // B_KV_HEADS
    wb = jnp.concatenate([_half_block(w_branch[1][h * B_DIM:(h + 1) * B_DIM], h // per_kv == 0, 0)
                          for h in range(B_HEADS)], axis=0)
    return [w_branch[0].astype(BF16), wb.astype(BF16), w_branch[2].astype(BF16), w_branch[3].astype(BF16)]


def kernel(x, c, ctx, c_ctx, w_ada, b_ada, w_in, a_q_norm, a_kv_norm, a_w_uq, a_w_ukv, b_sink, c_rpb, d_q_norm, d_k_norm, w_branch, w_gate, b_gate, w_out, ln1_g, ln1_b, ln2_g, ln2_b, peer_w_q, peer_sub_keys, peer_u, peer_v):
    bsz, seq, d = x.shape
    nctx = ctx.shape[1]
    depth = w_in.shape[0]
    t_lat, t_ctx = bsz * seq, bsz * nctx
    t_all = t_lat + t_ctx
    rows_grid = seq // GRID_W
    alpha = (2 * depth) ** 0.25

    tables = _rope_tables(seq)
    band_bias = band_bias_table()
    dims = dict(bsz=bsz, seq=seq, nctx=nctx, t_lat=t_lat)

    n_groups = -(-(bsz + 1) // 8) * 8
    cc = jnp.zeros((n_groups, d), F32).at[:bsz].set(c).at[bsz].set(c_ctx)

    def group_of(tm):
        lat_blocks, per_batch = t_lat // tm, seq // tm
        assert t_lat % tm == 0 and seq % tm == 0
        return lambda i: jnp.where(i < lat_blocks, i // per_batch, bsz)

    xs = jnp.concatenate([x.reshape(t_lat, d), ctx.reshape(t_ctx, d)], axis=0)

    for l in range(depth):
        need_ctx = l < depth - 1
        rows_out = t_all if need_ctx else t_lat
        mod = ada_modulation(cc, w_ada, b_ada[l], l).reshape(n_groups * 6, 1, d)

        u = token_matmul(xs, _in_proj_weight(w_in[l]), prologue="modulate", mod=mod, mod_idx=(0, 1),
                         group_of=group_of, out_dtype=BF16, name="in_proj")
        w_gate_l = jnp.transpose(w_gate[l], (1, 0, 2)).reshape(d, N_BRANCH * d).astype(BF16)
        gates = token_matmul(xs, w_gate_l, rows=rows_out, prologue="modulate", mod=mod, mod_idx=(0, 1),
                             group_of=group_of, bias=b_gate[l].reshape(-1), act="sigmoid", out_dtype=BF16,
                             name="branch_gates")

        w_uq, w_ukv = _mla_up_weights(a_w_uq[l], a_w_ukv[l])
        qa = token_matmul(u, w_uq, x_cols=(A_Q_LORA, U_CQ * LANE // A_Q_LORA), prologue="rmsnorm", norm_g=a_q_norm[l], out_dtype=BF16,
                          name="mla_q_up")
        kva = token_matmul(u, w_ukv, x_cols=(A_KV_LORA, U_CKV), prologue="rmsnorm", norm_g=a_kv_norm[l],
                           out_dtype=BF16, name="mla_kv_up")
        ur = qk_prep(u, 0, U_PREP_N, U_PREP_R32, U_PREP_RMSQ, tables, d_q_norm[l], d_k_norm[l], t_lat, seq, "qk_prep")
        qr = qk_prep(qa, 1, A_HEADS, A_HEADS, 0, tables, d_q_norm[l], d_k_norm[l], t_lat, seq, "mla_q_rope")
        mixers = {
            "mla": dict(q_srcs=[(qa, 0), (qr, 0)], k_srcs=[(kva, 0, 1), (ur, R_KR, 0)], v_src=(kva, A_HEADS, 1),
                        heads=A_HEADS, scale=A_SCALE),
            "swa": dict(q_srcs=[(ur, R_BQ)], k_srcs=[(ur, R_BK, 0)], v_src=(u, U_BV, 0), heads=B_HEADS,
                        scale=B_SCALE, sink=b_sink[l], heads_per_step=B_HEADS),
            "na": dict(q_srcs=[(u, U_CQ_)], k_srcs=[(u, U_CK, 1)], v_src=(u, U_CV, 1), heads=C_HEADS,
                       scale=C_SCALE, heads_per_step=C_HEADS),
            "gqa": dict(q_srcs=[(ur, R_DQ)], k_srcs=[(ur, R_DK, D_HEADS // D_KV_HEADS)],
                        v_src=(u, U_DV, D_HEADS // D_KV_HEADS), heads=D_HEADS, scale=D_SCALE),
        }
        windows = {"swa": dict(window=(B_TQ, B_BACK, B_WK), bias=band_bias),
                   "na": dict(window=(C_TQ, C_BACK, C_WK), bias=neighbourhood_bias_table(c_rpb[l], rows_grid))}
        branches = [attention(mode="lat", name=mixer + "_attention", **cfg, **windows.get(mixer, {}), **dims)
                    for mixer, cfg in mixers.items()]
        branches_ctx = [attention(mode="ctx", name=mixer + "_attention_ctx", **cfg, **dims)
                        for mixer, cfg in mixers.items()] if need_ctx else None

        xs, ht = merge_out_ln(branches, branches_ctx, _branch_weights(w_branch[l]), gates, w_out[l].astype(BF16), xs,
                              mod, 2, (3, 4), group_of, ln1_g[l], ln1_b[l], alpha)

        pq = token_matmul(xs, peer_w_q[l].astype(BF16), prologue="modulate", mod=mod, mod_idx=(3, 4),
                          group_of=group_of, out_dtype=BF16, name="peer_query")
        keys = peer_sub_keys[l].reshape(2 * PEER_HEADS, PEER_KEYS, PEER_QDIM // 2).astype(BF16)
        cnt, e0, rank, e1 = peer_topk(pq, keys)
        n_exp = peer_u.shape[1]
        vt = jnp.transpose(peer_v[l].astype(BF16).reshape(n_exp // PEER_EB, PEER_EB, d), (0, 2, 1))
        ffn_t = peer_dense(ht, peer_u[l].astype(BF16), vt, cnt, e0, rank, e1)
        xs = residual_ln(xs, ffn_t, mod, 5, group_of, ln2_g[l], ln2_b[l], alpha)

    return xs[:t_lat].reshape(bsz, seq, d)
```
